```python
import math
import jax
import jax.numpy as jnp
from jax import lax
import numpy as np

D_MODEL = 1024
BATCH = 8
SEQ = 4096
DEPTH = 2

CTX_LEN = 256
GRID_W = 64
N_MIXERS = 4
GROUP_W = D_MODEL // N_MIXERS
HEADS = 4
HEAD_V = GROUP_W // HEADS
HEAD_QK = HEAD_V // 2
N_DIR = 2
D_FF = 4 * D_MODEL
N_MOD = 6
CHUNK = 64
EPS = 1e-6
GN_EPS = 64e-5
M_INIT = -1e30
ML_SIZES = (HEADS * HEAD_QK, HEADS * HEAD_QK, GROUP_W, GROUP_W, N_DIR * HEADS, N_DIR * HEADS)
RW_DECAY_LORA = 64
RW_ICLR_LORA = 64
RW_GATE_LORA = 128
RW_DECAY_SCALE = math.exp(-0.5)
RW_SIZES = (GROUP_W, GROUP_W, GROUP_W, RW_DECAY_LORA, RW_ICLR_LORA, RW_GATE_LORA)
GLA_GATE_LORA = 16
GLA_LOGIT_NORM = 16.0
GL_SIZES = (HEADS * HEAD_QK, HEADS * HEAD_QK, GROUP_W, GLA_GATE_LORA, GROUP_W)
GD_CONV = 3
GD_SIZES = (3 * GROUP_W, GROUP_W, N_DIR * HEADS, N_DIR * HEADS)
ML_W = sum(ML_SIZES)
RW_W = sum(RW_SIZES)
GL_W = sum(GL_SIZES)
GD_W = sum(GD_SIZES)
P_IN = ML_W + RW_W + GL_W + GD_W

kernel_name = 'hybrid_mlstm_rwkv7_gla_gdn_prefix_block'


def _split(z, sizes):
    idx, acc = [], 0
    for s in sizes[:-1]:
        acc += s
        idx.append(acc)
    return jnp.split(z, idx, axis=-1)


def _rms(x, gain):
    xf = x.astype(jnp.float32)
    y = xf * lax.rsqrt(jnp.mean(jnp.square(xf), -1, keepdims=True) + EPS)
    return (y * gain.astype(jnp.float32)).astype(x.dtype)


def _head_rms(h, gain):
    y = h * lax.rsqrt(jnp.mean(jnp.square(h), -1, keepdims=True) + EPS)
    return y.reshape(h.shape[0], h.shape[1], -1) * gain


def _head_groupnorm(h, gain, bias):
    d = h - jnp.mean(h, -1, keepdims=True)
    y = d * lax.rsqrt(jnp.mean(jnp.square(d), -1, keepdims=True) + GN_EPS)
    return y.reshape(h.shape[0], h.shape[1], -1) * gain + bias


def _l2n(t):
    return t * lax.rsqrt(jnp.sum(jnp.square(t), -1, keepdims=True) + EPS)


def _mlp(h, w1, w2):
    return jnp.square(jax.nn.relu(h @ w1)) @ w2


def _tri(k):
    return jnp.tril(jnp.ones((CHUNK, CHUNK), dtype=bool), k)


def _chunk(t):
    b, l, h, d = t.shape
    return t.reshape(b, l // CHUNK, CHUNK, h, d).transpose(0, 3, 1, 2, 4)


def _unchunk(t):
    b, h, n, c, d = t.shape
    return t.transpose(0, 2, 3, 1, 4).reshape(b, n * c, h, d)


def _to_n(t):
    return jnp.moveaxis(t, 2, 0)


def _from_n(t):
    return jnp.moveaxis(t, 0, 2)


def _rev(t, n_ctx):
    return jnp.concatenate([jnp.flip(t[:, :n_ctx], 1), jnp.flip(t[:, n_ctx:], 1)], axis=1)


def _bidir(scan_fn, n_ctx, fwd_args, bwd_args):
    y_fwd = scan_fn(*fwd_args)
    y_bwd = scan_fn(*[_rev(t, n_ctx) for t in bwd_args])
    return y_fwd + _rev(y_bwd, n_ctx)


def _token_shift(z, n_ctx):
    L = z.shape[1]
    pos = jnp.arange(L)
    zp = jnp.pad(z, ((0, 0), (1, 0), (0, 0)))[:, :-1]
    zn = jnp.pad(z, ((0, 0), (0, 1), (0, 0)))[:, 1:]
    keep_p = ((pos != 0) & (pos != n_ctx))[None, :, None]
    keep_n = ((pos != L - 1) & (pos != n_ctx - 1))[None, :, None]
    return jnp.where(keep_p, zp, 0), jnp.where(keep_n, zn, 0)


def _grid_conv(t, width, w):
    b, l, ch = t.shape
    img = t.reshape(b, l // width, width, ch)
    out = lax.conv_general_dilated(img, w[:, :, None, :].astype(t.dtype), window_strides=(1, 1), padding='SAME',
                                   dimension_numbers=('NHWC', 'HWIO', 'NHWC'), feature_group_count=ch)
    return out.reshape(b, l, ch)


def _mlstm_scan(q, k, v, ig, fg):
    bsz, _, nh, dk = q.shape
    dv = v.shape[-1]
    q = _chunk(q) * dk ** -0.5
    k, v = _chunk(k), _chunk(v)
    ig = _chunk(ig[..., None])[..., 0]
    b = jnp.cumsum(jax.nn.log_sigmoid(_chunk(fg[..., None])[..., 0]), axis=-1)
    b_end = b[..., -1]
    w_end = b_end[..., None] - b + ig
    m_loc = jnp.max(w_end, -1)
    e_end = jnp.exp(w_end - m_loc[..., None])
    c_loc = jnp.einsum('bhnc,bhnck,bhncv->bhnkv', e_end, k, v)
    n_loc = jnp.einsum('bhnc,bhnck->bhnk', e_end, k)

    def step(carry, xs):
        c_s, n_s, m_s = carry
        c_l, n_l, m_l, bl = xs
        m_new = jnp.maximum(bl + m_s, m_l)
        f_s = jnp.exp(bl + m_s - m_new)
        f_l = jnp.exp(m_l - m_new)
        c_new = f_s[..., None, None] * c_s + f_l[..., None, None] * c_l
        n_new = f_s[..., None] * n_s + f_l[..., None] * n_l
        return (c_new, n_new, m_new), (c_s, n_s, m_s)

    init = (jnp.zeros((bsz, nh, dk, dv), jnp.float32), jnp.zeros((bsz, nh, dk), jnp.float32),
            jnp.full((bsz, nh), M_INIT, jnp.float32))
    _, (c0, n0, m0) = lax.scan(step, init, (_to_n(c_loc), _to_n(n_loc), _to_n(m_loc), _to_n(b_end)))
    c0, n0, m0 = _from_n(c0), _from_n(n0), _from_n(m0)
    log_d = jnp.where(_tri(0), b[..., :, None] - b[..., None, :] + ig[..., None, :], -jnp.inf)
    m_prev = b + m0[..., None]
    m_i = jnp.maximum(m_prev, jnp.max(log_d, -1))
    s = jnp.einsum('bhnik,bhnjk->bhnij', q, k) * jnp.exp(log_d - m_i[..., None])
    e_prev = jnp.exp(m_prev - m_i)
    num = jnp.einsum('bhnij,bhnjv->bhniv', s, v) + e_prev[..., None] * jnp.einsum('bhnik,bhnkv->bhniv', q, c0)
    den = jnp.sum(s, -1) + e_prev * jnp.einsum('bhnik,bhnk->bhni', q, n0)
    h = num / jnp.maximum(jnp.abs(den), jnp.exp(-m_i))[..., None]
    return _unchunk(h)


def _rwkv7_scan(r, log_w, k, v, a, b):
    bsz, _, nh, dk = r.shape
    dv = v.shape[-1]
    r, log_w, k, v, a, b = (_chunk(t) for t in (r, log_w, k, v, a, b))
    g = jnp.cumsum(log_w, axis=3)
    g_end = g[..., -1:, :]
    r_h = r * jnp.exp(g)
    a_h = a * jnp.exp(-g)
    k_h = k * jnp.exp(-g)
    b_h = b * jnp.exp(g - log_w)
    m_ab = jnp.where(_tri(-1), jnp.einsum('bhnid,bhnjd->bhnij', b_h, a_h), 0.0)
    m_bk = jnp.where(_tri(-1), jnp.einsum('bhnid,bhnjd->bhnij', b_h, k_h), 0.0)
    rhs = jnp.concatenate([b_h, jnp.einsum('bhnij,bhnjv->bhniv', m_bk, v)], -1)
    sol = lax.linalg.triangular_solve(-m_ab, rhs, left_side=True, lower=True, unit_diagonal=True)
    w_mat, u0 = sol[..., :dk], sol[..., dk:]
    a_end = a * jnp.exp(g_end - g)
    k_end = k * jnp.exp(g_end - g)

    def step(s, xs):
        w_c, u_c, a_c, k_c, v_c, ge = xs
        u_t = u_c + jnp.einsum('bhck,bhkv->bhcv', w_c, s)
        s_new = (jnp.exp(ge)[..., None] * s + jnp.einsum('bhck,bhcv->bhkv', a_c, u_t)
                 + jnp.einsum('bhck,bhcv->bhkv', k_c, v_c))
        return s_new, s

    xs = tuple(_to_n(t) for t in (w_mat, u0, a_end, k_end, v, g_end[..., 0, :]))
    _, s0 = lax.scan(step, jnp.zeros((bsz, nh, dk, dv), jnp.float32), xs)
    s0 = _from_n(s0)
    u = u0 + jnp.einsum('bhnck,bhnkv->bhncv', w_mat, s0)
    att_a = jnp.where(_tri(0), jnp.einsum('bhnid,bhnjd->bhnij', r_h, a_h), 0.0)
    att_k = jnp.where(_tri(0), jnp.einsum('bhnid,bhnjd->bhnij', r_h, k_h), 0.0)
    y = (jnp.einsum('bhnik,bhnkv->bhniv', r_h, s0) + jnp.einsum('bhnij,bhnjv->bhniv', att_a, u)
         + jnp.einsum('bhnij,bhnjv->bhniv', att_k, v))
    return _unchunk(y)


def _gla_scan(q, k, v, log_a):
    bsz, _, nh, dk = q.shape
    dv = v.shape[-1]
    q = _chunk(q) * dk ** -0.5
    k, v = _chunk(k), _chunk(v)
    g = jnp.cumsum(_chunk(log_a), axis=3)
    g_end = g[..., -1, :]
    g_mid = g[..., CHUNK // 2:CHUNK // 2 + 1, :]
    att = jnp.einsum('bhnik,bhnjk->bhnij', q * jnp.exp(g - g_mid), k * jnp.exp(g_mid - g))
    att = jnp.where(_tri(0), att, 0.0)
    s_loc = jnp.einsum('bhnck,bhncv->bhnkv', k * jnp.exp(g_end[..., None, :] - g), v)

    def step(s, xs):
        s_l, ge = xs
        return jnp.exp(ge)[..., None] * s + s_l, s

    _, s0 = lax.scan(step, jnp.zeros((bsz, nh, dk, dv), jnp.float32), (_to_n(s_loc), _to_n(g_end)))
    s0 = _from_n(s0)
    o = jnp.einsum('bhnik,bhnkv->bhniv', q * jnp.exp(g), s0) + jnp.einsum('bhnij,bhnjv->bhniv', att, v)
    return _unchunk(o)


def _gdn_scan(q, k, v, beta, log_a):
    bsz, _, nh, dk = q.shape
    dv = v.shape[-1]
    q = _chunk(q) * dk ** -0.5
    k, v = _chunk(k), _chunk(v)
    beta = _chunk(beta[..., None])
    g = jnp.cumsum(_chunk(log_a[..., None]), axis=3)
    decay = jnp.exp(jnp.where(_tri(0), g - jnp.swapaxes(g, -1, -2), -jnp.inf))
    m = jnp.where(_tri(-1), beta * jnp.einsum('bhnik,bhnjk->bhnij', k, k) * decay, 0.0)
    rhs = jnp.concatenate([v * beta, k * (beta * jnp.exp(g))], -1)
    sol = lax.linalg.triangular_solve(m, rhs, left_side=True, lower=True, unit_diagonal=True)
    u, w = sol[..., :dv], sol[..., dv:]
    k_end = k * jnp.exp(g[..., -1:, :] - g)

    def step(s, xs):
        u_c, w_c, k_c, ge = xs
        v_new = u_c - jnp.einsum('bhck,bhkv->bhcv', w_c, s)
        return jnp.exp(ge)[..., None, None] * s + jnp.einsum('bhck,bhcv->bhkv', k_c, v_new), s

    xs = (_to_n(u), _to_n(w), _to_n(k_end), _to_n(g[..., -1, 0]))
    _, s0 = lax.scan(step, jnp.zeros((bsz, nh, dk, dv), jnp.float32), xs)
    s0 = _from_n(s0)
    v_new = u - jnp.einsum('bhnck,bhnkv->bhncv', w, s0)
    att = jnp.einsum('bhnik,bhnjk->bhnij', q, k) * decay
    o = jnp.einsum('bhnik,bhnkv->bhniv', q * jnp.exp(g), s0) + jnp.einsum('bhnij,bhnjv->bhniv', att, v_new)
    return _unchunk(o)


def _mlstm_mixer(z, n_ctx, ig_b, fg_b, norm_w):
    bsz, L, _ = z.shape
    q, k, v, o, ig, fg = _split(z, ML_SIZES)
    q = q.reshape(bsz, L, HEADS, HEAD_QK)
    k = k.reshape(bsz, L, HEADS, HEAD_QK)
    v = v.reshape(bsz, L, HEADS, HEAD_V)
    ig = ig.reshape(bsz, L, N_DIR, HEADS) + ig_b
    fg = fg.reshape(bsz, L, N_DIR, HEADS) + fg_b
    h = _bidir(_mlstm_scan, n_ctx, (q, k, v, ig[:, :, 0], fg[:, :, 0]), (q, k, v, ig[:, :, 1], fg[:, :, 1]))
    return _head_rms(h, norm_w) * jax.nn.sigmoid(o)


def _rwkv_mixer(z, n_ctx, mu_prev, mu_next, w0, w_up, a0, a_up, g_up, k_k, k_a, r_k, gn_w, gn_b):
    bsz, L, _ = z.shape
    z_prev, z_next = _token_shift(z, n_ctx)
    z = z + mu_prev * (z_prev - z) + mu_next * (z_next - z)
    r, k, v, wl, al, gl = _split(z, RW_SIZES)

    def heads(t):
        return t.reshape(bsz, L, HEADS, HEAD_V)

    kk = _l2n(heads(k * k_k))
    wl = jnp.tanh(wl)
    g = jax.nn.sigmoid(gl) @ g_up
    dir_args, k_sum = [], 0.0
    for d in range(N_DIR):
        log_w = -RW_DECAY_SCALE * jax.nn.sigmoid(w0[d] + wl @ w_up[d])
        a = jax.nn.sigmoid(a0[d] + al @ a_up[d])
        kt = k * (1 + (a - 1) * k_a)
        k_sum = k_sum + kt
        dir_args.append((heads(r), heads(log_w), heads(kt), heads(v), kk * heads(a), -kk))
    h = _bidir(_rwkv7_scan, n_ctx, dir_args[0], dir_args[1])
    y = _head_groupnorm(h, gn_w, gn_b)
    bonus = jnp.sum(heads(r * k_sum * r_k), -1, keepdims=True) * heads(v)
    return (y + bonus.reshape(bsz, L, GROUP_W)) * g


def _gla_mixer(z, n_ctx, gate_up, gate_b, norm_w):
    bsz, L, _ = z.shape
    q, k, v, al, og = _split(z, GL_SIZES)
    q = q.reshape(bsz, L, HEADS, HEAD_QK)
    k = k.reshape(bsz, L, HEADS, HEAD_QK)
    v = v.reshape(bsz, L, HEADS, HEAD_V)
    log_a = [(jax.nn.log_sigmoid(al @ gate_up[d] + gate_b[d]) / GLA_LOGIT_NORM).reshape(bsz, L, HEADS, HEAD_QK)
             for d in range(N_DIR)]
    h = _bidir(_gla_scan, n_ctx, (q, k, v, log_a[0]), (q, k, v, log_a[1]))
    return _head_rms(h, norm_w) * jax.nn.silu(og)


def _gdn_mixer(z, n_ctx, conv_w, a_log, dt_bias, norm_w):
    bsz, L, _ = z.shape
    qkv, zg, bl, al = _split(z, GD_SIZES)
    qkv = jax.nn.silu(jnp.concatenate([_grid_conv(qkv[:, :n_ctx], n_ctx, conv_w),
                                       _grid_conv(qkv[:, n_ctx:], GRID_W, conv_w)], axis=1))
    q, k, v = (t.reshape(bsz, L, HEADS, HEAD_V) for t in jnp.split(qkv, 3, axis=-1))
    q, k = _l2n(q), _l2n(k)
    beta = jax.nn.sigmoid(bl.reshape(bsz, L, N_DIR, HEADS))
    log_a = -jnp.exp(a_log) * jax.nn.softplus(al.reshape(bsz, L, N_DIR, HEADS) + dt_bias)
    h = _bidir(_gdn_scan, n_ctx, (q, k, v, beta[:, :, 0], log_a[:, :, 0]), (q, k, v, beta[:, :, 1], log_a[:, :, 1]))
    return _head_rms(h, norm_w) * jax.nn.silu(zg)


def _normal(key, shape, scale):
    return jax.random.normal(key, shape, jnp.float32) * scale


def setup_inputs(seed: int = 0) -> dict:
    key = jax.random.key(seed)
    keys = list(jax.random.split(key, 40))
    nk = keys.pop
    D, L2 = D_MODEL, DEPTH
    dt = jnp.exp(jax.random.uniform(nk(), (L2, N_DIR, HEADS), jnp.float32, math.log(1e-3), math.log(0.1)))
    return {
        'x': _normal(nk(), (BATCH, SEQ, D), 1.0),
        'c': _normal(nk(), (BATCH, D), 1.0),
        'ctx': _normal(nk(), (BATCH, CTX_LEN, D), 1.0),
        'c_ctx': _normal(nk(), (D,), 1.0),
        'ada_w': _normal(nk(), (L2, D, N_MOD * D), 0.5 * D ** -0.5),
        'ada_b': _normal(nk(), (L2, N_MOD * D), 0.02),
        'norm1_w': 1.0 + _normal(nk(), (L2, D), 0.02),
        'norm2_w': 1.0 + _normal(nk(), (L2, D), 0.02),
        'w_in': _normal(nk(), (L2, D, P_IN), D ** -0.5),
        'w_out': _normal(nk(), (L2, D, D), D ** -0.5),
        'ml_ig_b': _normal(nk(), (L2, N_DIR, HEADS), 0.1),
        'ml_fg_b': jnp.linspace(3.0, 6.0, HEADS) + _normal(nk(), (L2, N_DIR, HEADS), 0.1),
        'ml_norm_w': 1.0 + _normal(nk(), (L2, GROUP_W), 0.02),
        'rw_mu_prev': jax.random.uniform(nk(), (L2, RW_W), jnp.float32, 0.0, 0.5),
        'rw_mu_next': jax.random.uniform(nk(), (L2, RW_W), jnp.float32, 0.0, 0.5),
        'rw_w0': jax.random.uniform(nk(), (L2, N_DIR, GROUP_W), jnp.float32, -3.0, 3.0),
        'rw_w_up': _normal(nk(), (L2, N_DIR, RW_DECAY_LORA, GROUP_W), RW_DECAY_LORA ** -0.5),
        'rw_a0': _normal(nk(), (L2, N_DIR, GROUP_W), 0.1),
        'rw_a_up': _normal(nk(), (L2, N_DIR, RW_ICLR_LORA, GROUP_W), RW_ICLR_LORA ** -0.5),
        'rw_g_up': _normal(nk(), (L2, RW_GATE_LORA, GROUP_W), RW_GATE_LORA ** -0.5),
        'rw_k_k': 0.85 + _normal(nk(), (L2, GROUP_W), 0.02),
        'rw_k_a': 1.0 + _normal(nk(), (L2, GROUP_W), 0.02),
        'rw_r_k': _normal(nk(), (L2, GROUP_W), 0.1),
        'rw_gn_w': 1.0 + _normal(nk(), (L2, GROUP_W), 0.02),
        'rw_gn_b': _normal(nk(), (L2, GROUP_W), 0.02),
        'gl_gate_up': _normal(nk(), (L2, N_DIR, GLA_GATE_LORA, HEADS * HEAD_QK), GLA_GATE_LORA ** -0.5),
        'gl_gate_b': _normal(nk(), (L2, N_DIR, HEADS * HEAD_QK), 0.1),
        'gl_norm_w': 1.0 + _normal(nk(), (L2, GROUP_W), 0.02),
        'gd_conv_w': _normal(nk(), (L2, GD_CONV, GD_CONV, 3 * GROUP_W), (GD_CONV * GD_CONV) ** -0.5),
        'gd_a_log': jnp.log(jax.random.uniform(nk(), (L2, N_DIR, HEADS), jnp.float32, 1.0, 16.0)),
        'gd_dt_bias': dt + jnp.log(-jnp.expm1(-dt)),
        'gd_norm_w': 1.0 + _normal(nk(), (L2, GROUP_W), 0.02),
        'mlp_w1': _normal(nk(), (L2, D, D_FF), D ** -0.5),
        'mlp_w2': _normal(nk(), (L2, D_FF, D), D_FF ** -0.5),
        'final_norm_w': 1.0 + _normal(nk(), (D,), 0.02),
    }


def reference(x, c, ctx, c_ctx, ada_w, ada_b, norm1_w, norm2_w, w_in, w_out, ml_ig_b, ml_fg_b, ml_norm_w,
              rw_mu_prev, rw_mu_next, rw_w0, rw_w_up, rw_a0, rw_a_up, rw_g_up, rw_k_k, rw_k_a, rw_r_k,
              rw_gn_w, rw_gn_b, gl_gate_up, gl_gate_b, gl_norm_w, gd_conv_w, gd_a_log, gd_dt_bias, gd_norm_w,
              mlp_w1, mlp_w2, final_norm_w):
    n_ctx = ctx.shape[1]
    for layer in range(DEPTH):
        mod_x = jax.nn.silu(c) @ ada_w[layer] + ada_b[layer]
        mod_c = jax.nn.silu(c_ctx) @ ada_w[layer] + ada_b[layer]
        shift1_x, scale1_x, gate1_x, shift2_x, scale2_x, gate2_x = jnp.split(mod_x[:, None, :], N_MOD, axis=-1)
        shift1_c, scale1_c, gate1_c, shift2_c, scale2_c, gate2_c = jnp.split(mod_c, N_MOD, axis=-1)
        hx = _rms(x, norm1_w[layer]) * (1 + scale1_x) + shift1_x
        hc = _rms(ctx, norm1_w[layer]) * (1 + scale1_c) + shift1_c
        z = (jnp.concatenate([hc, hx], axis=1) @ w_in[layer]).astype(jnp.float32)
        z_ml, z_rw, z_gl, z_gd = _split(z, (ML_W, RW_W, GL_W, GD_W))
        mixed = jnp.concatenate([
            _mlstm_mixer(z_ml, n_ctx, ml_ig_b[layer], ml_fg_b[layer], ml_norm_w[layer]),
            _rwkv_mixer(z_rw, n_ctx, rw_mu_prev[layer], rw_mu_next[layer], rw_w0[layer], rw_w_up[layer],
                        rw_a0[layer], rw_a_up[layer], rw_g_up[layer], rw_k_k[layer], rw_k_a[layer],
                        rw_r_k[layer], rw_gn_w[layer], rw_gn_b[layer]),
            _gla_mixer(z_gl, n_ctx, gl_gate_up[layer], gl_gate_b[layer], gl_norm_w[layer]),
            _gdn_mixer(z_gd, n_ctx, gd_conv_w[layer], gd_a_log[layer], gd_dt_bias[layer], gd_norm_w[layer]),
        ], axis=-1).astype(x.dtype)
        x = x + gate1_x * (mixed[:, n_ctx:] @ w_out[layer])
        x = x + gate2_x * _mlp(_rms(x, norm2_w[layer]) * (1 + scale2_x) + shift2_x, mlp_w1[layer], mlp_w2[layer])
        if layer < DEPTH - 1:
            ctx = ctx + gate1_c * (mixed[:, :n_ctx] @ w_out[layer])
            ctx = ctx + gate2_c * _mlp(_rms(ctx, norm2_w[layer]) * (1 + scale2_c) + shift2_c,
                                       mlp_w1[layer], mlp_w2[layer])
    return _rms(x, final_norm_w)
```

```python
import functools

import jax
import jax.numpy as jnp
from jax import lax
from jax.experimental import pallas as pl
from jax.experimental.pallas import tpu as pltpu

F32 = jnp.float32
BF16 = jnp.bfloat16
HI = lax.Precision.HIGHEST

CHUNK = 64
HEADS = 4
HEAD_V = 64
HEAD_QK = 32
GROUP_W = HEADS * HEAD_V
QK_W = HEADS * HEAD_QK
N_DIR = 2
EPS = 1e-6
GN_EPS = 64e-5
M_INIT = -1e30
RW_DECAY_SCALE = 0.6065306597126334
GLA_LOGIT_NORM = 16.0
MISC_W = 128
MISC_ML_IG, MISC_ML_FG, MISC_GL_AL, MISC_GD_BL, MISC_GD_AL = 0, 8, 16, 32, 40
VMEM_LIMIT = 56 * 1024 * 1024


def _mm(a, b, prec=None):
    return lax.dot_general(a, b, (((1,), (0,)), ((), ())), precision=prec, preferred_element_type=F32)


def _mm_nt(a, b, prec=None):
    return lax.dot_general(a, b, (((1,), (1,)), ((), ())), precision=prec, preferred_element_type=F32)


def _mm_tn(a, b, prec=None):
    return lax.dot_general(a, b, (((0,), (0,)), ((), ())), precision=prec, preferred_element_type=F32)


def _bf(x):
    return x.astype(BF16)


def _iota(shape, dim):
    return lax.broadcasted_iota(jnp.int32, shape, dim)


def _sigmoid(x):
    return 1.0 / (1.0 + jnp.exp(-x))


def _softplus(x):
    return jnp.maximum(x, 0.0) + jnp.log1p(jnp.exp(-jnp.abs(x)))


def _log_sigmoid(x):
    return -_softplus(-x)


def _silu(x):
    return x * _sigmoid(x)


def _time_mask(d, width, strict):
    i = _iota((CHUNK, width), 0)
    j = _iota((CHUNK, width), 1) & (CHUNK - 1)
    if d == 0:
        return (j < i) if strict else (j <= i)
    return (j > i) if strict else (j >= i)


def _cumsum_mat(d):
    return jnp.where(_time_mask(d, CHUNK, False), 1.0, 0.0).astype(F32)


def _eye_cat(width):
    i = _iota((CHUNK, width), 0)
    j = _iota((CHUNK, width), 1) & (CHUNK - 1)
    return jnp.where(i == j, 1.0, 0.0).astype(F32)


def _bd_mask(rows, cols, row_shift, col_shift, col_and=None):
    r = _iota((rows, cols), 0) >> row_shift
    c = _iota((rows, cols), 1)
    if col_and is not None:
        c = c & col_and
    return r == (c >> col_shift)


def _stack_bd(x, col_shift):
    w = x.shape[1]
    m = _bd_mask(HEADS * CHUNK, w, 6, col_shift)
    return jnp.where(m, jnp.concatenate([x] * HEADS, axis=0), 0.0)


def _expand_cols(misc, col0):
    e = jnp.where(_iota((MISC_W, GROUP_W), 0) == (_iota((MISC_W, GROUP_W), 1) >> 6) + col0, 1.0, 0.0)
    return _mm(misc, e.astype(F32), HI)


def _to_cols(x):
    ones = jnp.ones((CHUNK, CHUNK), F32)
    return _mm(ones, x * _eye_cat(GROUP_W), HI)


def _block_sum(x, prec=HI):
    w = x.shape[1]
    ones_bd = jnp.where(_bd_mask(w, w, 6, 6), 1.0, 0.0).astype(F32)
    return _mm(x, ones_bd, prec)


def _block_max(x):
    blk = _iota(x.shape, 1) >> 6
    out = jnp.zeros_like(x)
    for h in range(HEADS):
        sel = blk == h
        mh = jnp.max(jnp.where(sel, x, -jnp.inf), axis=1, keepdims=True)
        out = jnp.where(sel, mh, out)
    return out


def _inv_unit(n):
    r = _eye_cat(GROUP_W) + n
    p = n
    for it in range(5):
        pbd = _stack_bd(p, 6)
        if it == 0:
            p = _mm(p, pbd, HI)
        else:
            pr = _mm(jnp.concatenate([p, r], axis=0), pbd, HI)
            p = pr[:CHUNK]
            r = r + pr[CHUNK:]
    return r + _mm(r, _stack_bd(p, 6), HI)


def _l2n_heads(t):
    return t * lax.rsqrt(_block_sum(t * t) + EPS)


def _ml_chunk(d, xs, cs, st):
    z, misc = xs
    (gb,) = cs
    s_mat, m_row = st
    last = CHUNK - 1 if d == 0 else 0
    q = z[:, 0:QK_W] * (HEAD_QK ** -0.5)
    k = z[:, QK_W:2 * QK_W]
    v = z[:, 2 * QK_W:2 * QK_W + GROUP_W]
    ig = _expand_cols(misc, MISC_ML_IG + HEADS * d) + gb[d:d + 1, :]
    fg = _expand_cols(misc, MISC_ML_FG + HEADS * d) + gb[2 + d:3 + d, :]
    bcum = _mm(_cumsum_mat(d), _log_sigmoid(fg), HI)
    b_end = bcum[last:last + 1, :]
    w_end = b_end - bcum + ig
    m_loc = jnp.max(w_end, axis=0, keepdims=True)
    e_end = jnp.exp(w_end - m_loc)
    bd2 = _bd_mask(QK_W, 2 * GROUP_W, 5, 6, GROUP_W - 1)
    loc = jnp.where(bd2, _mm_tn(_bf(k), _bf(jnp.concatenate([v * e_end, e_end], axis=1))), 0.0)
    m_prev_s = b_end + m_row
    m_new = jnp.maximum(m_prev_s, m_loc)
    f_s = jnp.exp(m_prev_s - m_new)
    f_l = jnp.exp(m_loc - m_new)
    s_new = jnp.concatenate([f_s, f_s], axis=1) * s_mat + jnp.concatenate([f_l, f_l], axis=1) * loc
    tm0 = _time_mask(d, GROUP_W, False)
    logd = jnp.where(tm0, bcum + _to_cols(ig - bcum), -jnp.inf)
    m_prev = bcum + m_row
    m_i = jnp.maximum(m_prev, _block_max(logd))
    a = _mm_nt(_bf(q), _bf(_stack_bd(k, 5)))
    sm = a * jnp.exp(logd - m_i)
    e_prev = jnp.exp(m_prev - m_i)
    inter = _mm(_bf(q), _bf(s_mat))
    ones_bd = jnp.where(_bd_mask(GROUP_W, GROUP_W, 6, 6), 1.0, 0.0).astype(F32)
    intra = _mm(_bf(sm), _bf(jnp.concatenate([_stack_bd(v, 6), ones_bd], axis=1)))
    num = intra[:, :GROUP_W] + e_prev * inter[:, :GROUP_W]
    den = intra[:, GROUP_W:] + e_prev * inter[:, GROUP_W:]
    h = num / jnp.maximum(jnp.abs(den), jnp.exp(-m_i))
    return h, (s_new, m_new)


def _gl_chunk(d, xs, cs, st):
    z, misc = xs
    gup, gb = cs
    (s_mat,) = st
    last = CHUNK - 1 if d == 0 else 0
    mid = CHUNK // 2 if d == 0 else CHUNK - 1 - CHUNK // 2
    q = z[:, 0:QK_W] * (HEAD_QK ** -0.5)
    k = z[:, QK_W:2 * QK_W]
    v = z[:, 2 * QK_W:2 * QK_W + GROUP_W]
    la = _log_sigmoid(_mm(misc, gup[d], HI) + gb[d:d + 1, :]) * (1.0 / GLA_LOGIT_NORM)
    g = _mm(_cumsum_mat(d), la, HI)
    g_end = g[last:last + 1, :]
    g_mid = g[mid:mid + 1, :]
    qd = q * jnp.exp(g - g_mid)
    kd = k * jnp.exp(g_mid - g)
    att = jnp.where(_time_mask(d, GROUP_W, False), _mm_nt(_bf(qd), _bf(_stack_bd(kd, 5))), 0.0)
    bd = _bd_mask(QK_W, GROUP_W, 5, 6)
    s_loc = jnp.where(bd, _mm_tn(_bf(k * jnp.exp(g_end - g)), _bf(v)), 0.0)
    dec = jnp.exp(_mm_tn(la, jnp.ones((CHUNK, GROUP_W), F32), HI))
    s_new = dec * s_mat + s_loc
    o = _mm(_bf(q * jnp.exp(g)), _bf(s_mat)) + _mm(_bf(att), _bf(_stack_bd(v, 6)))
    return o, (s_new,)


def _gd_chunk(d, xs, cs, st):
    qkv, misc = xs
    (gp,) = cs
    (s_mat,) = st
    last = CHUNK - 1 if d == 0 else 0
    q = qkv[:, 0:GROUP_W] * (HEAD_V ** -0.5)
    k = qkv[:, GROUP_W:2 * GROUP_W]
    v = qkv[:, 2 * GROUP_W:3 * GROUP_W]
    beta = _sigmoid(_expand_cols(misc, MISC_GD_BL + HEADS * d))
    la = -jnp.exp(gp[d:d + 1, :]) * _softplus(_expand_cols(misc, MISC_GD_AL + HEADS * d) + gp[2 + d:3 + d, :])
    g = _mm(_cumsum_mat(d), la, HI)
    g_end = g[last:last + 1, :]
    dec0 = jnp.where(_time_mask(d, GROUP_W, False), jnp.exp(g - _to_cols(g)), 0.0)
    kst = _bf(_stack_bd(k, 6))
    x2 = _mm_nt(_bf(jnp.concatenate([k, q], axis=0)), kst)
    kk, qk = x2[:CHUNK], x2[CHUNK:]
    mm = jnp.where(_time_mask(d, GROUP_W, True), beta * kk * dec0, 0.0)
    ainv = _inv_unit(-mm)
    rhs = jnp.concatenate([_stack_bd(v * beta, 6), _stack_bd(k * (beta * jnp.exp(g)), 6)], axis=1)
    sol = _mm(ainv, rhs, HI)
    u, w = sol[:, :GROUP_W], sol[:, GROUP_W:]
    v_new = u - _mm(_bf(w), _bf(s_mat))
    k_end = k * jnp.exp(g_end - g)
    bd = _bd_mask(GROUP_W, GROUP_W, 6, 6)
    s_new = jnp.exp(g_end) * s_mat + jnp.where(bd, _mm_tn(_bf(k_end), _bf(v_new)), 0.0)
    o = _mm(_bf(q * jnp.exp(g)), _bf(s_mat)) + _mm(_bf(qk * dec0), _bf(_stack_bd(v_new, 6)))
    return o, (s_new,)


def _rw_chunk(d, xs, cs, st):
    base, dirp = xs
    (rp,) = cs
    (s_mat,) = st
    last = CHUNK - 1 if d == 0 else 0
    r = base[:, 0:GROUP_W]
    k = base[:, GROUP_W:2 * GROUP_W]
    v = base[:, 2 * GROUP_W:3 * GROUP_W]
    kk = base[:, 3 * GROUP_W:4 * GROUP_W]
    lw = dirp[:, 0:GROUP_W]
    a = dirp[:, GROUP_W:2 * GROUP_W]
    k_a = rp[5:6, :]
    kt = k * (1.0 + (a - 1.0) * k_a)
    ap = kk * a
    g = _mm(_cumsum_mat(d), lw, HI)
    g_end = g[last:last + 1, :]
    eg = jnp.exp(g)
    eng = jnp.exp(-g)
    r_h = r * eg
    a_h = ap * eng
    k_h = kt * eng
    b_h = -kk * jnp.exp(g - lw)
    x = _mm_nt(_bf(jnp.concatenate([b_h, r_h], axis=0)),
               _bf(jnp.concatenate([_stack_bd(a_h, 6), _stack_bd(k_h, 6)], axis=0)))
    tm1 = _time_mask(d, GROUP_W, True)
    tm0 = _time_mask(d, GROUP_W, False)
    m_ab = jnp.where(tm1, x[:CHUNK, :GROUP_W], 0.0)
    m_bk = jnp.where(tm1, x[:CHUNK, GROUP_W:], 0.0)
    att_a = jnp.where(tm0, x[CHUNK:, :GROUP_W], 0.0)
    att_k = jnp.where(tm0, x[CHUNK:, GROUP_W:], 0.0)
    vbd = _stack_bd(v, 6)
    rhs2 = _mm(_bf(m_bk), _bf(vbd))
    ainv = _inv_unit(m_ab)
    sol = _mm(ainv, jnp.concatenate([_stack_bd(b_h, 6), _stack_bd(rhs2, 6)], axis=1), HI)
    w_mat, u0 = sol[:, :GROUP_W], sol[:, GROUP_W:]
    u = u0 + _mm(_bf(w_mat), _bf(s_mat))
    ge_col = jnp.exp(_mm_tn(lw, jnp.ones((CHUNK, GROUP_W), F32), HI))
    dec_end = jnp.exp(g_end - g)
    bd = _bd_mask(GROUP_W, GROUP_W, 6, 6)
    upd = _mm_tn(_bf(jnp.concatenate([ap * dec_end, kt * dec_end], axis=0)), _bf(jnp.concatenate([u, v], axis=0)))
    s_new = ge_col * s_mat + jnp.where(bd, upd, 0.0)
    y = _mm(_bf(jnp.concatenate([r_h, att_a, att_k], axis=1)),
            _bf(jnp.concatenate([s_mat, _stack_bd(u, 6), vbd], axis=0)))
    return y, (s_new,)


def _scan_kernel(chunk_fn, n_in, n_cst, state_init, n_ctx_chunks, n_chunks, *refs):
    in_refs = (refs[:n_in], refs[n_in:2 * n_in])
    cst_refs = refs[2 * n_in:2 * n_in + n_cst]
    o_ref = refs[2 * n_in + n_cst]
    n_st = len(state_init)
    st_refs = refs[2 * n_in + n_cst + 1:]
    s = pl.program_id(1)

    @pl.when(s == 0)
    def _():
        o_ref[...] = jnp.zeros(o_ref.shape, o_ref.dtype)
        for d in range(N_DIR):
            for i, (shape, val) in enumerate(state_init):
                st_refs[d * n_st + i][...] = jnp.full(shape, val, F32)

    cs = tuple(c[...] for c in cst_refs)
    chunk_b = jnp.where(s < n_ctx_chunks, n_ctx_chunks - 1 - s, n_chunks - 1 + n_ctx_chunks - s)
    for d in range(N_DIR):
        xs = tuple(r[0] for r in in_refs[d])
        st = tuple(st_refs[d * n_st + i][...] for i in range(n_st))
        y, st_new = chunk_fn(d, xs, cs, st)
        for i in range(n_st):
            st_refs[d * n_st + i][...] = st_new[i]
        row = pl.multiple_of((s if d == 0 else chunk_b) * CHUNK, CHUNK)
        o_ref[0, pl.ds(row, CHUNK), :] += y


def _bidir_scan(chunk_fn, ins, csts, state_init, n_ctx):
    bsz, seq = ins[0].shape[0], ins[0].shape[1]
    n_chunks, n_ctx_chunks = seq // CHUNK, n_ctx // CHUNK

    def idx_f(b, s):
        return (b, s, 0)

    def idx_b(b, s):
        return (b, jnp.where(s < n_ctx_chunks, n_ctx_chunks - 1 - s, n_chunks - 1 + n_ctx_chunks - s), 0)

    in_specs = []
    for idx in (idx_f, idx_b):
        for a in ins:
            in_specs.append(pl.BlockSpec((1, CHUNK, a.shape[2]), idx))
    for c in csts:
        in_specs.append(pl.BlockSpec(c.shape, lambda b, s, nd=c.ndim: (0,) * nd))
    scratch = [pltpu.VMEM(shape, F32) for _ in range(N_DIR) for (shape, _) in state_init]
    kern = functools.partial(_scan_kernel, chunk_fn, len(ins), len(csts), state_init, n_ctx_chunks, n_chunks)
    return pl.pallas_call(
        kern,
        grid=(bsz, n_chunks),
        in_specs=in_specs,
        out_specs=pl.BlockSpec((1, seq, GROUP_W), lambda b, s: (b, 0, 0)),
        out_shape=jax.ShapeDtypeStruct((bsz, seq, GROUP_W), F32),
        scratch_shapes=scratch,
        compiler_params=pltpu.CompilerParams(dimension_semantics=("arbitrary", "arbitrary"),
                                             vmem_limit_bytes=VMEM_LIMIT),
    )(*ins, *ins, *csts)


def _ada_kernel(c_ref, w_ref, b_ref, o_ref):
    o_ref[0] = _mm(_bf(_silu(c_ref[...])), _bf(w_ref[0])) + b_ref[0]


def _ada_mod(cc, ada_w, ada_b):
    depth, d_model, n6 = ada_w.shape
    tn = 1536
    return pl.pallas_call(
        _ada_kernel,
        grid=(depth, n6 // tn),
        in_specs=[pl.BlockSpec(cc.shape, lambda l, j: (0, 0)),
                  pl.BlockSpec((1, d_model, tn), lambda l, j: (l, 0, j)),
                  pl.BlockSpec((1, 1, tn), lambda l, j: (l, 0, j))],
        out_specs=pl.BlockSpec((1, cc.shape[0], tn), lambda l, j: (l, 0, j)),
        out_shape=jax.ShapeDtypeStruct((depth, cc.shape[0], n6), F32),
        compiler_params=pltpu.CompilerParams(dimension_semantics=("arbitrary", "arbitrary"),
                                             vmem_limit_bytes=VMEM_LIMIT),
    )(cc, ada_w, ada_b.reshape(depth, 1, n6))


def _rms_rows(x, gain):
    return x * lax.rsqrt(jnp.mean(x * x, axis=-1, keepdims=True) + EPS) * gain


def _pick_mod(modx_ref, modc_ref, is_ctx, row):
    return jnp.where(is_ctx, modc_ref[row:row + 1, :], modx_ref[0, row:row + 1, :])


def _in_proj_kernel(n_ctx, tm, widths, tok_ref, modx_ref, modc_ref, nw_ref, w_ref, *out_refs):
    t = pl.program_id(1)
    x = tok_ref[0]
    is_ctx = (t * tm + _iota((tm, 1), 0)) < n_ctx
    shift = _pick_mod(modx_ref, modc_ref, is_ctx, 0)
    scale = _pick_mod(modx_ref, modc_ref, is_ctx, 1)
    h = _bf(_rms_rows(x, nw_ref[...]) * (1.0 + scale) + shift)
    c0 = 0
    for o_ref, w in zip(out_refs, widths):
        o_ref[0] = _mm(h, w_ref[:, c0:c0 + w])
        c0 += w


def _in_proj(tok, modx, modc, nw, w_p, widths, n_ctx, tm):
    bsz, seq, d_model = tok.shape
    kern = functools.partial(_in_proj_kernel, n_ctx, tm, widths)
    return pl.pallas_call(
        kern,
        grid=(bsz, seq // tm),
        in_specs=[pl.BlockSpec((1, tm, d_model), lambda b, t: (b, t, 0)),
                  pl.BlockSpec((1, 8, d_model), lambda b, t: (b, 0, 0)),
                  pl.BlockSpec((8, d_model), lambda b, t: (0, 0)),
                  pl.BlockSpec((1, d_model), lambda b, t: (0, 0)),
                  pl.BlockSpec(w_p.shape, lambda b, t: (0, 0), pipeline_mode=pl.Buffered(1))],
        out_specs=[pl.BlockSpec((1, tm, w), lambda b, t: (b, t, 0)) for w in widths],
        out_shape=[jax.ShapeDtypeStruct((bsz, seq, w), F32) for w in widths],
        compiler_params=pltpu.CompilerParams(dimension_semantics=("arbitrary", "arbitrary"),
                                             vmem_limit_bytes=VMEM_LIMIT),
    )(tok, modx, modc, nw, w_p)


def _shift_rows(x, prev_row, next_row):
    row = _iota(x.shape, 0)
    dn = jnp.where(row == 0, prev_row, pltpu.roll(x, 1, 0))
    up = jnp.where(row == CHUNK - 1, next_row, pltpu.roll(x, CHUNK - 1, 0))
    return dn, up


def _rw_prep_kernel(n_ctx_chunks, n_chunks, z_ref, zp_ref, zn_ref, mu_ref, rp_ref, lora_ref, gup_ref,
                    base_ref, d0_ref, d1_ref, post_ref):
    n = pl.program_id(1)
    z = z_ref[0]
    keep_p = jnp.where((n != 0) & (n != n_ctx_chunks), 1.0, 0.0)
    keep_n = jnp.where((n != n_chunks - 1) & (n != n_ctx_chunks - 1), 1.0, 0.0)
    z_prev, z_next = _shift_rows(z, zp_ref[0, 7:8, :] * keep_p, zn_ref[0, 0:1, :] * keep_n)
    zm = z + mu_ref[0:1, :] * (z_prev - z) + mu_ref[1:2, :] * (z_next - z)
    r = zm[:, 0:GROUP_W]
    k = zm[:, GROUP_W:2 * GROUP_W]
    v = zm[:, 2 * GROUP_W:3 * GROUP_W]
    lo = zm[:, 3 * GROUP_W:3 * GROUP_W + 128]
    gl = zm[:, 3 * GROUP_W + 128:4 * GROUP_W]
    lo = jnp.where(_iota(lo.shape, 1) < 64, jnp.tanh(lo), lo)
    pre = _mm(lo, lora_ref[...], HI)
    rp = rp_ref[...]
    kk = _l2n_heads(k * rp[4:5, :])
    lw0 = -RW_DECAY_SCALE * _sigmoid(rp[0:1, :] + pre[:, 0:GROUP_W])
    a0 = _sigmoid(rp[2:3, :] + pre[:, GROUP_W:2 * GROUP_W])
    lw1 = -RW_DECAY_SCALE * _sigmoid(rp[1:2, :] + pre[:, 2 * GROUP_W:3 * GROUP_W])
    a1 = _sigmoid(rp[3:4, :] + pre[:, 3 * GROUP_W:4 * GROUP_W])
    k_a = rp[5:6, :]
    k_sum = k * (1.0 + (a0 - 1.0) * k_a) + k * (1.0 + (a1 - 1.0) * k_a)
    gate = _mm(_bf(_sigmoid(gl)), _bf(gup_ref[...]))
    bonus = _block_sum(r * k_sum * rp[6:7, :]) * v
    base_ref[0] = jnp.concatenate([r, k, v, kk], axis=1)
    d0_ref[0] = jnp.concatenate([lw0, a0], axis=1)
    d1_ref[0] = jnp.concatenate([lw1, a1], axis=1)
    post_ref[0] = jnp.concatenate([gate, bonus], axis=1)


def _rw_prep(z_rw, mu, rp, lora, gup, n_ctx):
    bsz, seq, w = z_rw.shape
    n_chunks, n_ctx_chunks = seq // CHUNK, n_ctx // CHUNK
    sub = CHUNK // 8
    kern = functools.partial(_rw_prep_kernel, n_ctx_chunks, n_chunks)
    full = lambda a: pl.BlockSpec(a.shape, lambda b, n, nd=a.ndim: (0,) * nd)
    return pl.pallas_call(
        kern,
        grid=(bsz, n_chunks),
        in_specs=[pl.BlockSpec((1, CHUNK, w), lambda b, n: (b, n, 0)),
                  pl.BlockSpec((1, 8, w), lambda b, n: (b, jnp.maximum(n * sub - 1, 0), 0)),
                  pl.BlockSpec((1, 8, w), lambda b, n: (b, jnp.minimum((n + 1) * sub, n_chunks * sub - 1), 0)),
                  full(mu), full(rp), full(lora), full(gup)],
        out_specs=[pl.BlockSpec((1, CHUNK, ow), lambda b, n: (b, n, 0)) for ow in (4 * GROUP_W, 2 * GROUP_W, 2 * GROUP_W, 2 * GROUP_W)],
        out_shape=[jax.ShapeDtypeStruct((bsz, seq, ow), F32) for ow in (4 * GROUP_W, 2 * GROUP_W, 2 * GROUP_W, 2 * GROUP_W)],
        compiler_params=pltpu.CompilerParams(dimension_semantics=("arbitrary", "arbitrary"),
                                             vmem_limit_bytes=VMEM_LIMIT),
    )(z_rw, z_rw, z_rw, mu, rp, lora, gup)


def _gd_prep_kernel(n_ctx_chunks, n_chunks, zc_ref, zu_ref, zd_ref, cw_ref, o_ref):
    n = pl.program_id(1)
    cw = cw_ref[...]
    cur, up, down = zc_ref[0], zu_ref[0], zd_ref[0]
    is_ctx = n < n_ctx_chunks
    ctx_p = jnp.where(is_ctx & (n > 0), 1.0, 0.0)
    ctx_n = jnp.where(is_ctx & (n < n_ctx_chunks - 1), 1.0, 0.0)
    has_up = jnp.where((n > n_ctx_chunks), 1.0, 0.0)
    has_dn = jnp.where((n >= n_ctx_chunks) & (n < n_chunks - 1), 1.0, 0.0)
    zero = jnp.zeros((1, cur.shape[1]), F32)
    c_l, c_r = _shift_rows(cur, up[CHUNK - 1:CHUNK, :] * ctx_p, down[0:1, :] * ctx_n)
    acc = cw[3:4, :] * c_l + cw[4:5, :] * cur + cw[5:6, :] * c_r
    u_l, u_r = _shift_rows(up, zero, zero)
    acc = acc + has_up * (cw[0:1, :] * u_l + cw[1:2, :] * up + cw[2:3, :] * u_r)
    d_l, d_r = _shift_rows(down, zero, zero)
    acc = acc + has_dn * (cw[6:7, :] * d_l + cw[7:8, :] * down + cw[8:9, :] * d_r)
    y = _silu(acc)
    q = _l2n_heads(y[:, 0:GROUP_W])
    k = _l2n_heads(y[:, GROUP_W:2 * GROUP_W])
    o_ref[0] = jnp.concatenate([q, k, y[:, 2 * GROUP_W:]], axis=1)


def _gd_prep(z_gd, cw, n_ctx):
    bsz, seq, _ = z_gd.shape
    w = 3 * GROUP_W
    n_chunks, n_ctx_chunks = seq // CHUNK, n_ctx // CHUNK
    kern = functools.partial(_gd_prep_kernel, n_ctx_chunks, n_chunks)
    return pl.pallas_call(
        kern,
        grid=(bsz, n_chunks),
        in_specs=[pl.BlockSpec((1, CHUNK, w), lambda b, n: (b, n, 0)),
                  pl.BlockSpec((1, CHUNK, w), lambda b, n: (b, jnp.maximum(n - 1, 0), 0)),
                  pl.BlockSpec((1, CHUNK, w), lambda b, n: (b, jnp.minimum(n + 1, n_chunks - 1), 0)),
                  pl.BlockSpec(cw.shape, lambda b, n: (0, 0))],
        out_specs=pl.BlockSpec((1, CHUNK, w), lambda b, n: (b, n, 0)),
        out_shape=jax.ShapeDtypeStruct((bsz, seq, w), F32),
        compiler_params=pltpu.CompilerParams(dimension_semantics=("arbitrary", "arbitrary"),
                                             vmem_limit_bytes=VMEM_LIMIT),
    )(z_gd, z_gd, z_gd, cw)


def _head_rms(h, gain):
    return h * lax.rsqrt(_block_sum(h * h) * (1.0 / HEAD_V) + EPS) * gain


def _head_groupnorm(h, gain, bias):
    dlt = h - _block_sum(h) * (1.0 / HEAD_V)
    return dlt * lax.rsqrt(_block_sum(dlt * dlt) * (1.0 / HEAD_V) + GN_EPS) * gain + bias


def _out_mlp_kernel(n_ctx, tm, ff_chunk, final, tok_ref, hml_ref, hrw_ref, hgl_ref, hgd_ref, oml_ref, prw_ref,
                    ogl_ref, zgd_ref, modx_ref, modc_ref, np_ref, n2_ref, fn_ref, wo_ref, w1_ref, w2_ref, o_ref):
    t = pl.program_id(1)
    x = tok_ref[0]
    is_ctx = (t * tm + _iota((tm, 1), 0)) < n_ctx
    np_ = np_ref[...]
    prw = prw_ref[0]
    mixed = jnp.concatenate([
        _head_rms(hml_ref[0], np_[0:1, :]) * _sigmoid(oml_ref[0]),
        (_head_groupnorm(hrw_ref[0], np_[1:2, :], np_[2:3, :]) + prw[:, GROUP_W:]) * prw[:, :GROUP_W],
        _head_rms(hgl_ref[0], np_[3:4, :]) * _silu(ogl_ref[0]),
        _head_rms(hgd_ref[0], np_[4:5, :]) * _silu(zgd_ref[0]),
    ], axis=1)
    x = x + _pick_mod(modx_ref, modc_ref, is_ctx, 2) * _mm(_bf(mixed), wo_ref[...])
    shift = _pick_mod(modx_ref, modc_ref, is_ctx, 3)
    scale = _pick_mod(modx_ref, modc_ref, is_ctx, 4)
    h = _bf(_rms_rows(x, n2_ref[...]) * (1.0 + scale) + shift)
    d_ff = w1_ref.shape[1]
    acc = jnp.zeros(x.shape, F32)
    for c0 in range(0, d_ff, ff_chunk):
        a = jnp.maximum(_mm(h, w1_ref[:, c0:c0 + ff_chunk]), 0.0)
        acc = acc + _mm(_bf(a * a), w2_ref[c0:c0 + ff_chunk, :])
    x = x + _pick_mod(modx_ref, modc_ref, is_ctx, 5) * acc
    if final:
        x = _rms_rows(x, fn_ref[...])
    o_ref[0] = x


def _out_mlp(tok, h_ml, h_rw, h_gl, h_gd, z_ml, rw_post, z_gl, z_gd, modx, modc, normp, n2, fnw, wo, w1, w2,
             n_ctx, tm, final):
    bsz, seq, d_model = tok.shape
    kern = functools.partial(_out_mlp_kernel, n_ctx, tm, 1024, final)
    row = lambda w, j=0: pl.BlockSpec((1, tm, w), lambda b, t, j=j: (b, t, j))
    const = lambda a: pl.BlockSpec(a.shape, lambda b, t, nd=a.ndim: (0,) * nd)
    wconst = lambda a: pl.BlockSpec(a.shape, lambda b, t: (0, 0), pipeline_mode=pl.Buffered(1))
    return pl.pallas_call(
        kern,
        grid=(bsz, seq // tm),
        in_specs=[row(d_model), row(GROUP_W), row(GROUP_W), row(GROUP_W), row(GROUP_W),
                  row(GROUP_W, 2), row(2 * GROUP_W), row(GROUP_W, 2), row(GROUP_W, 3),
                  pl.BlockSpec((1, 8, d_model), lambda b, t: (b, 0, 0)), const(modc), const(normp), const(n2),
                  const(fnw), wconst(wo), wconst(w1), wconst(w2)],
        out_specs=row(d_model),
        out_shape=jax.ShapeDtypeStruct((bsz, seq, d_model), F32),
        compiler_params=pltpu.CompilerParams(dimension_semantics=("arbitrary", "arbitrary"),
                                             vmem_limit_bytes=VMEM_LIMIT),
    )(tok, h_ml, h_rw, h_gl, h_gd, z_ml, rw_post, z_gl, z_gd, modx, modc, normp, n2, fnw, wo, w1, w2)


def _pad_rows(a, rows=8):
    return jnp.concatenate([a, jnp.zeros((rows - a.shape[0],) + a.shape[1:], a.dtype)], axis=0)


def _per_head_row(p):
    return jnp.repeat(p, HEAD_V, axis=-1)


def _permute_w_in(w):
    d_model = w.shape[0]
    ml_w, rw_w, gl_w = 784, 1024, 784
    o_ml, o_rw, o_gl, o_gd = 0, ml_w, ml_w + rw_w, ml_w + rw_w + gl_w
    misc = jnp.concatenate([
        w[:, o_ml + 768:o_ml + 784],
        w[:, o_gl + 512:o_gl + 528],
        w[:, o_gd + 1024:o_gd + 1040],
        jnp.zeros((d_model, MISC_W - 48), w.dtype)], axis=1)
    return jnp.concatenate([
        w[:, o_ml:o_ml + 768],
        w[:, o_rw:o_rw + 1024],
        w[:, o_gl:o_gl + 512], w[:, o_gl + 528:o_gl + 784],
        w[:, o_gd:o_gd + 1024],
        misc], axis=1)


Z_WIDTHS = (768, 1024, 768, 1024, MISC_W)


def _mix_layer(z_ml, z_rw, z_gl, z_gd, z_misc, p, n_ctx):
    s_ml = (((QK_W, 2 * GROUP_W), 0.0), ((1, GROUP_W), M_INIT))
    h_ml = _bidir_scan(_ml_chunk, [z_ml, z_misc], [p['ml_gb']], s_ml, n_ctx)
    h_gl = _bidir_scan(_gl_chunk, [z_gl, z_misc], [p['gl_gup'], p['gl_gb']], (((QK_W, GROUP_W), 0.0),), n_ctx)
    qkv = _gd_prep(z_gd, p['gd_cw'], n_ctx)
    h_gd = _bidir_scan(_gd_chunk, [qkv, z_misc], [p['gd_gp']], (((GROUP_W, GROUP_W), 0.0),), n_ctx)
    base, d0, d1, rw_post = _rw_prep(z_rw, p['rw_mu'], p['rw_rp'], p['rw_lora'], p['rw_gup'], n_ctx)
    h_rw = _rw_scan(base, d0, d1, p['rw_rp'], n_ctx)
    return h_ml, h_rw, h_gl, h_gd, rw_post


def _rw_scan(base, d0, d1, rp, n_ctx):
    bsz, seq = base.shape[0], base.shape[1]
    n_chunks, n_ctx_chunks = seq // CHUNK, n_ctx // CHUNK

    def idx_f(b, s):
        return (b, s, 0)

    def idx_b(b, s):
        return (b, jnp.where(s < n_ctx_chunks, n_ctx_chunks - 1 - s, n_chunks - 1 + n_ctx_chunks - s), 0)

    state_init = (((GROUP_W, GROUP_W), 0.0),)
    kern = functools.partial(_scan_kernel, _rw_chunk, 2, 1, state_init, n_ctx_chunks, n_chunks)
    return pl.pallas_call(
        kern,
        grid=(bsz, n_chunks),
        in_specs=[pl.BlockSpec((1, CHUNK, 4 * GROUP_W), idx_f), pl.BlockSpec((1, CHUNK, 2 * GROUP_W), idx_f),
                  pl.BlockSpec((1, CHUNK, 4 * GROUP_W), idx_b), pl.BlockSpec((1, CHUNK, 2 * GROUP_W), idx_b),
                  pl.BlockSpec(rp.shape, lambda b, s: (0, 0))],
        out_specs=pl.BlockSpec((1, seq, GROUP_W), lambda b, s: (b, 0, 0)),
        out_shape=jax.ShapeDtypeStruct((bsz, seq, GROUP_W), F32),
        scratch_shapes=[pltpu.VMEM((GROUP_W, GROUP_W), F32) for _ in range(N_DIR)],
        compiler_params=pltpu.CompilerParams(dimension_semantics=("arbitrary", "arbitrary"),
                                             vmem_limit_bytes=VMEM_LIMIT),
    )(base, d0, base, d1, rp)


def _layer_params(layer, w_in, ml_ig_b, ml_fg_b, ml_norm_w, rw_mu_prev, rw_mu_next, rw_w0, rw_w_up, rw_a0,
                  rw_a_up, rw_g_up, rw_k_k, rw_k_a, rw_r_k, rw_gn_w, rw_gn_b, gl_gate_up, gl_gate_b, gl_norm_w,
                  gd_conv_w, gd_a_log, gd_dt_bias, gd_norm_w):
    l = layer
    z64 = jnp.zeros((64, GROUP_W), F32)
    lora = jnp.concatenate([
        jnp.concatenate([rw_w_up[l, 0], z64, rw_w_up[l, 1], z64], axis=1),
        jnp.concatenate([z64, rw_a_up[l, 0], z64, rw_a_up[l, 1]], axis=1)], axis=0)
    gup = jnp.zeros((N_DIR, MISC_W, QK_W), F32).at[:, MISC_GL_AL:MISC_GL_AL + 16, :].set(gl_gate_up[l])
    return {
        'w_in': _bf(_permute_w_in(w_in[l])),
        'ml_gb': _pad_rows(jnp.concatenate([_per_head_row(ml_ig_b[l]), _per_head_row(ml_fg_b[l])], axis=0)),
        'gl_gup': gup,
        'gl_gb': _pad_rows(gl_gate_b[l]),
        'gd_cw': _pad_rows(gd_conv_w[l].reshape(9, 3 * GROUP_W), 16),
        'gd_gp': _pad_rows(jnp.concatenate([_per_head_row(gd_a_log[l]), _per_head_row(gd_dt_bias[l])], axis=0)),
        'rw_mu': _pad_rows(jnp.stack([rw_mu_prev[l], rw_mu_next[l]])),
        'rw_rp': _pad_rows(jnp.stack([rw_w0[l, 0], rw_w0[l, 1], rw_a0[l, 0], rw_a0[l, 1],
                                      rw_k_k[l], rw_k_a[l], rw_r_k[l]])),
        'rw_lora': lora,
        'rw_gup': rw_g_up[l],
        'normp': _pad_rows(jnp.stack([ml_norm_w[l], rw_gn_w[l], rw_gn_b[l], gl_norm_w[l], gd_norm_w[l]])),
    }


def kernel(x, c, ctx, c_ctx, ada_w, ada_b, norm1_w, norm2_w, w_in, w_out, ml_ig_b, ml_fg_b, ml_norm_w, rw_mu_prev, rw_mu_next, rw_w0, rw_w_up, rw_a0, rw_a_up, rw_g_up, rw_k_k, rw_k_a, rw_r_k, rw_gn_w, rw_gn_b, gl_gate_up, gl_gate_b, gl_norm_w, gd_conv_w, gd_a_log, gd_dt_bias, gd_norm_w, mlp_w1, mlp_w2, final_norm_w):
    bsz, seq_x, d_model = x.shape
    n_ctx = ctx.shape[1]
    depth = w_in.shape[0]
    seq = n_ctx + seq_x
    tm = seq // 8 if seq % 64 == 0 else seq
    tok = jnp.concatenate([ctx, x], axis=1)
    cc = _pad_rows(jnp.concatenate([c, c_ctx[None, :]], axis=0), ((bsz + 1 + 7) // 8) * 8)
    mod = _ada_mod(cc, ada_w, ada_b)
    fnw = final_norm_w[None, :]
    for layer in range(depth):
        p = _layer_params(layer, w_in, ml_ig_b, ml_fg_b, ml_norm_w, rw_mu_prev, rw_mu_next, rw_w0, rw_w_up, rw_a0,
                          rw_a_up, rw_g_up, rw_k_k, rw_k_a, rw_r_k, rw_gn_w, rw_gn_b, gl_gate_up, gl_gate_b,
                          gl_norm_w, gd_conv_w, gd_a_log, gd_dt_bias, gd_norm_w)
        modx = jnp.pad(mod[layer, :bsz].reshape(bsz, 6, d_model), ((0, 0), (0, 2), (0, 0)))
        modc = _pad_rows(mod[layer, bsz].reshape(6, d_model))
        z_ml, z_rw, z_gl, z_gd, z_misc = _in_proj(tok, modx, modc, norm1_w[layer][None, :], p['w_in'], Z_WIDTHS,
                                                  n_ctx, tm)
        h_ml, h_rw, h_gl, h_gd, rw_post = _mix_layer(z_ml, z_rw, z_gl, z_gd, z_misc, p, n_ctx)
        tok = _out_mlp(tok, h_ml, h_rw, h_gl, h_gd, z_ml, rw_post, z_gl, z_gd, modx, modc, p['normp'],
                       norm2_w[layer][None, :], fnw, _bf(w_out[layer]), _bf(mlp_w1[layer]), _bf(mlp_w2[layer]),
                       n_ctx, tm, layer == depth - 1)
    return tok[:, n_ctx:]
```

```python
import functools

import jax
import jax.numpy as jnp
from jax import lax
from jax.experimental import pallas as pl
from jax.experimental.pallas import tpu as pltpu

F32 = jnp.float32
BF16 = jnp.bfloat16

CHUNK = 64
HEADS = 4
HEAD_V = 64
HEAD_QK = 32
GROUP_W = HEADS * HEAD_V
QK_W = HEADS * HEAD_QK
N_DIR = 2
EPS = 1e-6
GN_EPS = 64e-5
M_INIT = -1e30
RW_DECAY_SCALE = 0.6065306597126334
GLA_LOGIT_NORM = 16.0
MISC_W = 128
MISC_ML_IG, MISC_ML_FG, MISC_GL_AL, MISC_GD_BL, MISC_GD_AL = 0, 8, 16, 32, 40
VMEM_LIMIT = 56 * 1024 * 1024


def _mm(a, b, prec=None):
    return lax.dot_general(a, b, (((1,), (0,)), ((), ())), precision=prec, preferred_element_type=F32)


def _mm_nt(a, b, prec=None):
    return lax.dot_general(a, b, (((1,), (1,)), ((), ())), precision=prec, preferred_element_type=F32)


def _mm_tn(a, b, prec=None):
    return lax.dot_general(a, b, (((0,), (0,)), ((), ())), precision=prec, preferred_element_type=F32)


def _bf(x):
    return x.astype(BF16)


def _iota(shape, dim):
    return lax.broadcasted_iota(jnp.int32, shape, dim)


def _sigmoid(x):
    return 1.0 / (1.0 + jnp.exp(-x))


def _softplus(x):
    return jnp.maximum(x, 0.0) + jnp.log1p(jnp.exp(-jnp.abs(x)))


def _log_sigmoid(x):
    return -_softplus(-x)


def _silu(x):
    return x * _sigmoid(x)


def _time_mask(d, width, strict):
    i = _iota((CHUNK, width), 0)
    j = _iota((CHUNK, width), 1) & (CHUNK - 1)
    if d == 0:
        return (j < i) if strict else (j <= i)
    return (j > i) if strict else (j >= i)


def _cumsum_mat(d):
    return jnp.where(_time_mask(d, CHUNK, False), 1.0, 0.0).astype(F32)


def _eye_cat(width):
    i = _iota((CHUNK, width), 0)
    j = _iota((CHUNK, width), 1) & (CHUNK - 1)
    return jnp.where(i == j, 1.0, 0.0).astype(F32)


def _bd_mask(rows, cols, row_shift, col_shift, col_and=None):
    r = _iota((rows, cols), 0) >> row_shift
    c = _iota((rows, cols), 1)
    if col_and is not None:
        c = c & col_and
    return r == (c >> col_shift)


def _stack_bd(x, col_shift):
    w = x.shape[1]
    m = _bd_mask(HEADS * CHUNK, w, 6, col_shift)
    return jnp.where(m, jnp.concatenate([x] * HEADS, axis=0), 0.0)


def _split(x, n):
    out, r = [], x
    for i in range(n):
        p = r.astype(BF16)
        out.append(p)
        if i + 1 < n:
            r = r - p.astype(F32)
    return out


def _sel_left(t, x, n=3):
    return _mm(jnp.concatenate([_bf(t)] * n, axis=1), jnp.concatenate(_split(x, n), axis=0))


def _sel_right(x, e, n=3):
    return _mm(jnp.concatenate(_split(x, n), axis=1), jnp.concatenate([_bf(e)] * n, axis=0))


def _mm3(a, b):
    ah, al = _split(a, 2)
    bh, bl = _split(b, 2)
    return _mm(jnp.concatenate([ah, ah, al], axis=1), jnp.concatenate([bh, bl, bh], axis=0))


def _cumsum_t(d, x):
    return _sel_left(_cumsum_mat(d), x)


def _expand_mat(col0s):
    r = _iota((MISC_W, GROUP_W), 0)
    hcol = _iota((MISC_W, GROUP_W), 1) >> 6
    return jnp.concatenate([jnp.where(r == hcol + c0, 1.0, 0.0) for c0 in col0s], axis=1).astype(F32)


def _to_cols(x):
    return _sel_left(jnp.ones((CHUNK, CHUNK), F32), x * _eye_cat(GROUP_W))


def _block_sum(x):
    w = x.shape[1]
    ones_bd = jnp.where(_bd_mask(w, w, 6, 6), 1.0, 0.0).astype(F32)
    return _sel_right(x, ones_bd, 2)


def _block_max(x):
    blk = _iota(x.shape, 1) >> 6
    out = jnp.zeros_like(x)
    for h in range(HEADS):
        sel = blk == h
        mh = jnp.max(jnp.where(sel, x, -jnp.inf), axis=1, keepdims=True)
        out = jnp.where(sel, mh, out)
    return out


def _inv_unit(n):
    r = _eye_cat(GROUP_W) + n
    p = n
    for it in range(5):
        pbd = _stack_bd(p, 6)
        if it == 0:
            p = _mm3(p, pbd)
            yield
        else:
            pr = _mm3(jnp.concatenate([p, r], axis=0), pbd)
            yield
            p = pr[:CHUNK]
            r = r + pr[CHUNK:]
    out = _mm3(r, _stack_bd(p, 6))
    yield
    return r + out


def _l2n_heads(t):
    return t * lax.rsqrt(_block_sum(t * t) + EPS)


def _ml_chunk(d, xs, cs, st):
    z, misc = xs
    (gb,) = cs
    s_mat, m_row = st
    last = CHUNK - 1 if d == 0 else 0
    q = z[:, 0:QK_W] * (HEAD_QK ** -0.5)
    k = z[:, QK_W:2 * QK_W]
    v = z[:, 2 * QK_W:2 * QK_W + GROUP_W]
    t = misc + gb[0:1, :]
    is_fg = (_iota(misc.shape, 1) >> 3) == (MISC_ML_FG >> 3)
    y = jnp.where(is_fg, _log_sigmoid(t), t)
    cum = _cumsum_t(d, y)
    a = _mm_nt(_bf(q), _bf(_stack_bd(k, 5)))
    inter = _mm(_bf(q), _bf(s_mat))
    yield
    ex = _sel_right(jnp.where(is_fg, cum, y), _expand_mat((MISC_ML_IG + HEADS * d, MISC_ML_FG + HEADS * d)))
    yield
    ig, bcum = ex[:, :GROUP_W], ex[:, GROUP_W:]
    b_end = bcum[last:last + 1, :]
    w_end = b_end - bcum + ig
    m_loc = jnp.max(w_end, axis=0, keepdims=True)
    e_end = jnp.exp(w_end - m_loc)
    cols = _to_cols(ig - bcum)
    loc = _mm_tn(_bf(k), _bf(jnp.concatenate([v * e_end, e_end], axis=1)))
    yield
    loc = jnp.where(_bd_mask(QK_W, 2 * GROUP_W, 5, 6, GROUP_W - 1), loc, 0.0)
    m_prev_s = b_end + m_row
    m_new = jnp.maximum(m_prev_s, m_loc)
    f_s = jnp.exp(m_prev_s - m_new)
    f_l = jnp.exp(m_loc - m_new)
    s_new = jnp.concatenate([f_s, f_s], axis=1) * s_mat + jnp.concatenate([f_l, f_l], axis=1) * loc
    logd = jnp.where(_time_mask(d, GROUP_W, False), bcum + cols, -jnp.inf)
    m_prev = bcum + m_row
    m_i = jnp.maximum(m_prev, _block_max(logd))
    sm = a * jnp.exp(logd - m_i)
    e_prev = jnp.exp(m_prev - m_i)
    ones_bd = jnp.where(_bd_mask(GROUP_W, GROUP_W, 6, 6), 1.0, 0.0).astype(F32)
    intra = _mm(_bf(sm), _bf(jnp.concatenate([_stack_bd(v, 6), ones_bd], axis=1)))
    yield
    num = intra[:, :GROUP_W] + e_prev * inter[:, :GROUP_W]
    den = intra[:, GROUP_W:] + e_prev * inter[:, GROUP_W:]
    h = num / jnp.maximum(jnp.abs(den), jnp.exp(-m_i))
    return h, (s_new, m_new)


def _gl_chunk(d, xs, cs, st):
    z, misc = xs
    gup, gb = cs
    (s_mat,) = st
    last = CHUNK - 1 if d == 0 else 0
    mid = CHUNK // 2 if d == 0 else CHUNK - 1 - CHUNK // 2
    q = z[:, 0:QK_W] * (HEAD_QK ** -0.5)
    k = z[:, QK_W:2 * QK_W]
    v = z[:, 2 * QK_W:2 * QK_W + GROUP_W]
    pre = _mm3(misc, gup[d])
    yield
    la = _log_sigmoid(pre + gb[d:d + 1, :]) * (1.0 / GLA_LOGIT_NORM)
    g = _cumsum_t(d, la)
    yield
    g_end = g[last:last + 1, :]
    g_mid = g[mid:mid + 1, :]
    qd = q * jnp.exp(g - g_mid)
    kd = k * jnp.exp(g_mid - g)
    att = _mm_nt(_bf(qd), _bf(_stack_bd(kd, 5)))
    s_loc = _mm_tn(_bf(v), _bf(k * jnp.exp(g_end - g)))
    o_inter = _mm_nt(_bf(q * jnp.exp(g)), _bf(s_mat))
    yield
    att = jnp.where(_time_mask(d, GROUP_W, False), att, 0.0)
    s_new = jnp.exp(g_end) * s_mat + jnp.where(_bd_mask(GROUP_W, QK_W, 6, 5), s_loc, 0.0)
    o_intra = _mm(_bf(att), _bf(_stack_bd(v, 6)))
    yield
    return o_inter + o_intra, (s_new,)


def _gd_chunk(d, xs, cs, st):
    qkv, misc = xs
    (gp,) = cs
    (s_mat,) = st
    last = CHUNK - 1 if d == 0 else 0
    q = qkv[:, 0:GROUP_W] * (HEAD_V ** -0.5)
    k = qkv[:, GROUP_W:2 * GROUP_W]
    v = qkv[:, 2 * GROUP_W:3 * GROUP_W]
    t = misc + gp[0:1, :]
    is_al = (_iota(misc.shape, 1) >> 3) == (MISC_GD_AL >> 3)
    y = jnp.where(is_al, -jnp.exp(gp[1:2, :]) * _softplus(t), _sigmoid(t))
    cum = _cumsum_t(d, y)
    x2 = _mm_nt(_bf(jnp.concatenate([k, q], axis=0)), _bf(_stack_bd(k, 6)))
    yield
    ex = _sel_right(jnp.where(is_al, cum, y), _expand_mat((MISC_GD_BL + HEADS * d, MISC_GD_AL + HEADS * d)))
    yield
    beta, g = ex[:, :GROUP_W], ex[:, GROUP_W:]
    g_end = g[last:last + 1, :]
    cols = _to_cols(g)
    yield
    dec0 = jnp.where(_time_mask(d, GROUP_W, False), jnp.exp(g - cols), 0.0)
    kk, qk = x2[:CHUNK], x2[CHUNK:]
    mm = jnp.where(_time_mask(d, GROUP_W, True), beta * kk * dec0, 0.0)
    ainv = yield from _inv_unit(-mm)
    eg = jnp.exp(g)
    rhs = jnp.concatenate([_stack_bd(v * beta, 6), _stack_bd(k * (beta * eg), 6)], axis=1)
    sol = _mm3(ainv, rhs)
    o_inter = _mm(_bf(q * eg), _bf(s_mat))
    yield
    u, w = sol[:, :GROUP_W], sol[:, GROUP_W:]
    ws = _mm(_bf(w), _bf(s_mat))
    yield
    v_new = u - ws
    k_end = k * jnp.exp(g_end - g)
    upd = _mm_tn(_bf(k_end), _bf(v_new))
    o_intra = _mm(_bf(qk * dec0), _bf(_stack_bd(v_new, 6)))
    yield
    s_new = jnp.exp(g_end) * s_mat + jnp.where(_bd_mask(GROUP_W, GROUP_W, 6, 6), upd, 0.0)
    return o_inter + o_intra, (s_new,)


def _rw_chunk(d, xs, cs, st):
    base, dirp = xs
    (rp,) = cs
    (s_mat,) = st
    last = CHUNK - 1 if d == 0 else 0
    r = base[:, 0:GROUP_W]
    k = base[:, GROUP_W:2 * GROUP_W]
    v = base[:, 2 * GROUP_W:3 * GROUP_W]
    kk = base[:, 3 * GROUP_W:4 * GROUP_W]
    lw = dirp[:, 0:GROUP_W]
    a = dirp[:, GROUP_W:2 * GROUP_W]
    k_a = rp[5:6, :]
    kt = k * (1.0 + (a - 1.0) * k_a)
    ap = kk * a
    g = _cumsum_t(d, lw)
    yield
    g_end = g[last:last + 1, :]
    eg = jnp.exp(g)
    eng = jnp.exp(-g)
    r_h = r * eg
    a_h = ap * eng
    k_h = kt * eng
    b_h = -kk * jnp.exp(g - lw)
    x = _mm_nt(_bf(jnp.concatenate([b_h, r_h], axis=0)),
               _bf(jnp.concatenate([_stack_bd(a_h, 6), _stack_bd(k_h, 6)], axis=0)))
    y_inter = _mm_nt(_bf(r_h), _bf(s_mat))
    yield
    tm1 = _time_mask(d, GROUP_W, True)
    tm0 = _time_mask(d, GROUP_W, False)
    m_ab = jnp.where(tm1, x[:CHUNK, :GROUP_W], 0.0)
    m_bk = jnp.where(tm1, x[:CHUNK, GROUP_W:], 0.0)
    att_a = jnp.where(tm0, x[CHUNK:, :GROUP_W], 0.0)
    att_k = jnp.where(tm0, x[CHUNK:, GROUP_W:], 0.0)
    vbd = _stack_bd(v, 6)
    rhs2 = _mm(_bf(m_bk), _bf(vbd))
    ainv = yield from _inv_unit(m_ab)
    sol = _mm3(ainv, jnp.concatenate([_stack_bd(b_h, 6), _stack_bd(rhs2, 6)], axis=1))
    yield
    w_mat, u0 = sol[:, :GROUP_W], sol[:, GROUP_W:]
    ws = _mm_nt(_bf(w_mat), _bf(s_mat))
    yield
    u = u0 + ws
    dec_end = jnp.exp(g_end - g)
    upd = _mm_tn(_bf(jnp.concatenate([u, v], axis=0)), _bf(jnp.concatenate([ap * dec_end, kt * dec_end], axis=0)))
    y_intra = _mm(_bf(jnp.concatenate([att_a, att_k], axis=1)), _bf(jnp.concatenate([_stack_bd(u, 6), vbd], axis=0)))
    yield
    s_new = jnp.exp(g_end) * s_mat + jnp.where(_bd_mask(GROUP_W, GROUP_W, 6, 6), upd, 0.0)
    return y_inter + y_intra, (s_new,)


def _scan_kernel(chunk_fn, nb, n_in, n_cst, state_init, n_ctx_chunks, n_chunks, *refs):
    in_refs = (refs[:n_in], refs[n_in:2 * n_in])
    cst_refs = refs[2 * n_in:2 * n_in + n_cst]
    o_ref = refs[2 * n_in + n_cst]
    n_st = len(state_init)
    st_refs = refs[2 * n_in + n_cst + 1:]
    s = pl.program_id(1)

    @pl.when(s == 0)
    def _():
        o_ref[...] = jnp.zeros(o_ref.shape, o_ref.dtype)
        for i, (shape, val) in enumerate(state_init):
            st_refs[i][...] = jnp.full((N_DIR * nb,) + shape, val, F32)

    cs = tuple(c[...] for c in cst_refs)
    chunk_b = jnp.where(s < n_ctx_chunks, n_ctx_chunks - 1 - s, n_chunks - 1 + n_ctx_chunks - s)
    chains = [(bi, d) for bi in range(nb) for d in range(N_DIR)]
    gens = []
    for bi, d in chains:
        xs = tuple(r[bi] for r in in_refs[d])
        st = tuple(st_refs[i][d * nb + bi] for i in range(n_st))
        gens.append(chunk_fn(d, xs, cs, st))
    results = [None] * len(chains)
    while any(r is None for r in results):
        for ci, gen in enumerate(gens):
            if results[ci] is None:
                try:
                    next(gen)
                except StopIteration as stop:
                    results[ci] = stop.value
    for (bi, d), (y, st_new) in zip(chains, results):
        for i in range(n_st):
            st_refs[i][d * nb + bi] = st_new[i]
        row = pl.multiple_of((s if d == 0 else chunk_b) * CHUNK, CHUNK)
        o_ref[bi, pl.ds(row, CHUNK), :] += y


def _bidir_scan(chunk_fn, ins_f, ins_b, csts, state_init, n_ctx, nb):
    bsz, seq = ins_f[0].shape[0], ins_f[0].shape[1]
    n_chunks, n_ctx_chunks = seq // CHUNK, n_ctx // CHUNK

    def idx_f(b, s):
        return (b, s, 0)

    def idx_b(b, s):
        return (b, jnp.where(s < n_ctx_chunks, n_ctx_chunks - 1 - s, n_chunks - 1 + n_ctx_chunks - s), 0)

    in_specs = [pl.BlockSpec((nb, CHUNK, a.shape[2]), idx_f) for a in ins_f]
    in_specs += [pl.BlockSpec((nb, CHUNK, a.shape[2]), idx_b) for a in ins_b]
    in_specs += [pl.BlockSpec(c.shape, lambda b, s, nd=c.ndim: (0,) * nd) for c in csts]
    scratch = [pltpu.VMEM((N_DIR * nb,) + shape, F32) for (shape, _) in state_init]
    kern = functools.partial(_scan_kernel, chunk_fn, nb, len(ins_f), len(csts), state_init, n_ctx_chunks, n_chunks)
    return pl.pallas_call(
        kern,
        name=chunk_fn.__name__.strip("_") + "_scan",
        grid=(bsz // nb, n_chunks),
        in_specs=in_specs,
        out_specs=pl.BlockSpec((nb, seq, GROUP_W), lambda b, s: (b, 0, 0)),
        out_shape=jax.ShapeDtypeStruct((bsz, seq, GROUP_W), F32),
        scratch_shapes=scratch,
        compiler_params=pltpu.CompilerParams(dimension_semantics=("arbitrary", "arbitrary"),
                                             vmem_limit_bytes=VMEM_LIMIT),
    )(*ins_f, *ins_b, *csts)


def _ada_kernel(c_ref, w_ref, b_ref, o_ref):
    o_ref[0] = _mm(_bf(_silu(c_ref[...])), _bf(w_ref[0])) + b_ref[0]


def _ada_mod(cc, ada_w, ada_b):
    depth, d_model, n6 = ada_w.shape
    tn = 1536
    return pl.pallas_call(
        _ada_kernel,
        name="ada_mod",
        grid=(depth, n6 // tn),
        in_specs=[pl.BlockSpec(cc.shape, lambda l, j: (0, 0)),
                  pl.BlockSpec((1, d_model, tn), lambda l, j: (l, 0, j)),
                  pl.BlockSpec((1, 1, tn), lambda l, j: (l, 0, j))],
        out_specs=pl.BlockSpec((1, cc.shape[0], tn), lambda l, j: (l, 0, j)),
        out_shape=jax.ShapeDtypeStruct((depth, cc.shape[0], n6), F32),
        compiler_params=pltpu.CompilerParams(dimension_semantics=("arbitrary", "arbitrary"),
                                             vmem_limit_bytes=VMEM_LIMIT),
    )(cc, ada_w, ada_b.reshape(depth, 1, n6))


def _rms_rows(x, gain):
    return x * lax.rsqrt(jnp.mean(x * x, axis=-1, keepdims=True) + EPS) * gain


def _pick_mod(modx_ref, modc_ref, is_ctx, row):
    return jnp.where(is_ctx, modc_ref[row:row + 1, :], modx_ref[0, row:row + 1, :])


def _in_proj_kernel(n_ctx, tm, widths, tok_ref, modx_ref, modc_ref, nw_ref, w_ref, *out_refs):
    t = pl.program_id(1)
    x = tok_ref[0]
    is_ctx = (t * tm + _iota((tm, 1), 0)) < n_ctx
    shift = _pick_mod(modx_ref, modc_ref, is_ctx, 0)
    scale = _pick_mod(modx_ref, modc_ref, is_ctx, 1)
    h = _bf(_rms_rows(x, nw_ref[...]) * (1.0 + scale) + shift)
    c0 = 0
    for o_ref, w in zip(out_refs, widths):
        o_ref[0] = _mm(h, w_ref[:, c0:c0 + w])
        c0 += w


def _in_proj(tok, modx, modc, nw, w_p, widths, n_ctx, tm):
    bsz, seq, d_model = tok.shape
    kern = functools.partial(_in_proj_kernel, n_ctx, tm, widths)
    return pl.pallas_call(
        kern,
        name="in_proj",
        grid=(bsz, seq // tm),
        in_specs=[pl.BlockSpec((1, tm, d_model), lambda b, t: (b, t, 0)),
                  pl.BlockSpec((1, 8, d_model), lambda b, t: (b, 0, 0)),
                  pl.BlockSpec((8, d_model), lambda b, t: (0, 0)),
                  pl.BlockSpec((1, d_model), lambda b, t: (0, 0)),
                  pl.BlockSpec(w_p.shape, lambda b, t: (0, 0), pipeline_mode=pl.Buffered(1))],
        out_specs=[pl.BlockSpec((1, tm, w), lambda b, t: (b, t, 0)) for w in widths],
        out_shape=[jax.ShapeDtypeStruct((bsz, seq, w), F32) for w in widths],
        compiler_params=pltpu.CompilerParams(dimension_semantics=("arbitrary", "arbitrary"),
                                             vmem_limit_bytes=VMEM_LIMIT),
    )(tok, modx, modc, nw, w_p)


def _shift_rows(x, prev_row, next_row):
    row = _iota(x.shape, 0)
    dn = jnp.where(row == 0, prev_row, pltpu.roll(x, 1, 0))
    up = jnp.where(row == CHUNK - 1, next_row, pltpu.roll(x, CHUNK - 1, 0))
    return dn, up


def _rw_prep_kernel(n_ctx_chunks, n_chunks, z_ref, zp_ref, zn_ref, mu_ref, rp_ref, lora_ref, gup_ref,
                    base_ref, d0_ref, d1_ref, post_ref):
    n = pl.program_id(1)
    z = z_ref[0]
    keep_p = jnp.where((n != 0) & (n != n_ctx_chunks), 1.0, 0.0)
    keep_n = jnp.where((n != n_chunks - 1) & (n != n_ctx_chunks - 1), 1.0, 0.0)
    z_prev, z_next = _shift_rows(z, zp_ref[0, 7:8, :] * keep_p, zn_ref[0, 0:1, :] * keep_n)
    zm = z + mu_ref[0:1, :] * (z_prev - z) + mu_ref[1:2, :] * (z_next - z)
    r = zm[:, 0:GROUP_W]
    k = zm[:, GROUP_W:2 * GROUP_W]
    v = zm[:, 2 * GROUP_W:3 * GROUP_W]
    lo = zm[:, 3 * GROUP_W:3 * GROUP_W + 128]
    gl = zm[:, 3 * GROUP_W + 128:4 * GROUP_W]
    lo = jnp.where(_iota(lo.shape, 1) < 64, jnp.tanh(lo), lo)
    pre = _mm3(lo, lora_ref[...])
    rp = rp_ref[...]
    kk = _l2n_heads(k * rp[4:5, :])
    lw0 = -RW_DECAY_SCALE * _sigmoid(rp[0:1, :] + pre[:, 0:GROUP_W])
    a0 = _sigmoid(rp[2:3, :] + pre[:, GROUP_W:2 * GROUP_W])
    lw1 = -RW_DECAY_SCALE * _sigmoid(rp[1:2, :] + pre[:, 2 * GROUP_W:3 * GROUP_W])
    a1 = _sigmoid(rp[3:4, :] + pre[:, 3 * GROUP_W:4 * GROUP_W])
    k_a = rp[5:6, :]
    k_sum = k * (1.0 + (a0 - 1.0) * k_a) + k * (1.0 + (a1 - 1.0) * k_a)
    gate = _mm(_bf(_sigmoid(gl)), _bf(gup_ref[...]))
    bonus = _block_sum(r * k_sum * rp[6:7, :]) * v
    base_ref[0] = jnp.concatenate([r, k, v, kk], axis=1)
    d0_ref[0] = jnp.concatenate([lw0, a0], axis=1)
    d1_ref[0] = jnp.concatenate([lw1, a1], axis=1)
    post_ref[0] = jnp.concatenate([gate, bonus], axis=1)


def _rw_prep(z_rw, mu, rp, lora, gup, n_ctx):
    bsz, seq, w = z_rw.shape
    n_chunks, n_ctx_chunks = seq // CHUNK, n_ctx // CHUNK
    sub = CHUNK // 8
    kern = functools.partial(_rw_prep_kernel, n_ctx_chunks, n_chunks)
    full = lambda a: pl.BlockSpec(a.shape, lambda b, n, nd=a.ndim: (0,) * nd)
    return pl.pallas_call(
        kern,
        name="rw_prep",
        grid=(bsz, n_chunks),
        in_specs=[pl.BlockSpec((1, CHUNK, w), lambda b, n: (b, n, 0)),
                  pl.BlockSpec((1, 8, w), lambda b, n: (b, jnp.maximum(n * sub - 1, 0), 0)),
                  pl.BlockSpec((1, 8, w), lambda b, n: (b, jnp.minimum((n + 1) * sub, n_chunks * sub - 1), 0)),
                  full(mu), full(rp), full(lora), full(gup)],
        out_specs=[pl.BlockSpec((1, CHUNK, ow), lambda b, n: (b, n, 0)) for ow in (4 * GROUP_W, 2 * GROUP_W, 2 * GROUP_W, 2 * GROUP_W)],
        out_shape=[jax.ShapeDtypeStruct((bsz, seq, ow), F32) for ow in (4 * GROUP_W, 2 * GROUP_W, 2 * GROUP_W, 2 * GROUP_W)],
        compiler_params=pltpu.CompilerParams(dimension_semantics=("arbitrary", "arbitrary"),
                                             vmem_limit_bytes=VMEM_LIMIT),
    )(z_rw, z_rw, z_rw, mu, rp, lora, gup)


def _gd_prep_kernel(n_ctx_chunks, n_chunks, zc_ref, zu_ref, zd_ref, cw_ref, o_ref):
    n = pl.program_id(1)
    cw = cw_ref[...]
    cur, up, down = zc_ref[0], zu_ref[0], zd_ref[0]
    is_ctx = n < n_ctx_chunks
    ctx_p = jnp.where(is_ctx & (n > 0), 1.0, 0.0)
    ctx_n = jnp.where(is_ctx & (n < n_ctx_chunks - 1), 1.0, 0.0)
    has_up = jnp.where((n > n_ctx_chunks), 1.0, 0.0)
    has_dn = jnp.where((n >= n_ctx_chunks) & (n < n_chunks - 1), 1.0, 0.0)
    zero = jnp.zeros((1, cur.shape[1]), F32)
    c_l, c_r = _shift_rows(cur, up[CHUNK - 1:CHUNK, :] * ctx_p, down[0:1, :] * ctx_n)
    acc = cw[3:4, :] * c_l + cw[4:5, :] * cur + cw[5:6, :] * c_r
    u_l, u_r = _shift_rows(up, zero, zero)
    acc = acc + has_up * (cw[0:1, :] * u_l + cw[1:2, :] * up + cw[2:3, :] * u_r)
    d_l, d_r = _shift_rows(down, zero, zero)
    acc = acc + has_dn * (cw[6:7, :] * d_l + cw[7:8, :] * down + cw[8:9, :] * d_r)
    y = _silu(acc)
    q = _l2n_heads(y[:, 0:GROUP_W])
    k = _l2n_heads(y[:, GROUP_W:2 * GROUP_W])
    o_ref[0] = jnp.concatenate([q, k, y[:, 2 * GROUP_W:]], axis=1)


def _gd_prep(z_gd, cw, n_ctx):
    bsz, seq, _ = z_gd.shape
    w = 3 * GROUP_W
    n_chunks, n_ctx_chunks = seq // CHUNK, n_ctx // CHUNK
    kern = functools.partial(_gd_prep_kernel, n_ctx_chunks, n_chunks)
    return pl.pallas_call(
        kern,
        name="gd_prep",
        grid=(bsz, n_chunks),
        in_specs=[pl.BlockSpec((1, CHUNK, w), lambda b, n: (b, n, 0)),
                  pl.BlockSpec((1, CHUNK, w), lambda b, n: (b, jnp.maximum(n - 1, 0), 0)),
                  pl.BlockSpec((1, CHUNK, w), lambda b, n: (b, jnp.minimum(n + 1, n_chunks - 1), 0)),
                  pl.BlockSpec(cw.shape, lambda b, n: (0, 0))],
        out_specs=pl.BlockSpec((1, CHUNK, w), lambda b, n: (b, n, 0)),
        out_shape=jax.ShapeDtypeStruct((bsz, seq, w), F32),
        compiler_params=pltpu.CompilerParams(dimension_semantics=("arbitrary", "arbitrary"),
                                             vmem_limit_bytes=VMEM_LIMIT),
    )(z_gd, z_gd, z_gd, cw)


def _head_rms(h, gain):
    return h * lax.rsqrt(_block_sum(h * h) * (1.0 / HEAD_V) + EPS) * gain


def _head_groupnorm(h, gain, bias):
    dlt = h - _block_sum(h) * (1.0 / HEAD_V)
    return dlt * lax.rsqrt(_block_sum(dlt * dlt) * (1.0 / HEAD_V) + GN_EPS) * gain + bias


def _out_mlp_kernel(n_ctx, tm, ff_chunk, final, tok_ref, hml_ref, hrw_ref, hgl_ref, hgd_ref, oml_ref, prw_ref,
                    ogl_ref, zgd_ref, modx_ref, modc_ref, np_ref, n2_ref, fn_ref, wo_ref, w1_ref, w2_ref, o_ref):
    t = pl.program_id(1)
    x = tok_ref[0]
    is_ctx = (t * tm + _iota((tm, 1), 0)) < n_ctx
    np_ = np_ref[...]
    prw = prw_ref[0]
    mixed = jnp.concatenate([
        _head_rms(hml_ref[0], np_[0:1, :]) * _sigmoid(oml_ref[0]),
        (_head_groupnorm(hrw_ref[0], np_[1:2, :], np_[2:3, :]) + prw[:, GROUP_W:]) * prw[:, :GROUP_W],
        _head_rms(hgl_ref[0], np_[3:4, :]) * _silu(ogl_ref[0]),
        _head_rms(hgd_ref[0], np_[4:5, :]) * _silu(zgd_ref[0]),
    ], axis=1)
    x = x + _pick_mod(modx_ref, modc_ref, is_ctx, 2) * _mm(_bf(mixed), wo_ref[...])
    shift = _pick_mod(modx_ref, modc_ref, is_ctx, 3)
    scale = _pick_mod(modx_ref, modc_ref, is_ctx, 4)
    h = _bf(_rms_rows(x, n2_ref[...]) * (1.0 + scale) + shift)
    d_ff = w1_ref.shape[1]
    acc = jnp.zeros(x.shape, F32)
    for c0 in range(0, d_ff, ff_chunk):
        a = jnp.maximum(_mm(h, w1_ref[:, c0:c0 + ff_chunk]), 0.0)
        acc = acc + _mm(_bf(a * a), w2_ref[c0:c0 + ff_chunk, :])
    x = x + _pick_mod(modx_ref, modc_ref, is_ctx, 5) * acc
    if final:
        x = _rms_rows(x, fn_ref[...])
    o_ref[0] = x


def _out_mlp(tok, h_ml, h_rw, h_gl, h_gd, z_ml, rw_post, z_gl, z_gd, modx, modc, normp, n2, fnw, wo, w1, w2,
             n_ctx, tm, final):
    bsz, seq, d_model = tok.shape
    kern = functools.partial(_out_mlp_kernel, n_ctx, tm, 1024, final)
    row = lambda w, j=0: pl.BlockSpec((1, tm, w), lambda b, t, j=j: (b, t, j))
    const = lambda a: pl.BlockSpec(a.shape, lambda b, t, nd=a.ndim: (0,) * nd)
    wconst = lambda a: pl.BlockSpec(a.shape, lambda b, t: (0, 0), pipeline_mode=pl.Buffered(1))
    return pl.pallas_call(
        kern,
        name="out_mlp",
        grid=(bsz, seq // tm),
        in_specs=[row(d_model), row(GROUP_W), row(GROUP_W), row(GROUP_W), row(GROUP_W),
                  row(GROUP_W, 2), row(2 * GROUP_W), row(GROUP_W, 2), row(GROUP_W, 3),
                  pl.BlockSpec((1, 8, d_model), lambda b, t: (b, 0, 0)), const(modc), const(normp), const(n2),
                  const(fnw), wconst(wo), wconst(w1), wconst(w2)],
        out_specs=row(d_model),
        out_shape=jax.ShapeDtypeStruct((bsz, seq, d_model), F32),
        compiler_params=pltpu.CompilerParams(dimension_semantics=("arbitrary", "arbitrary"),
                                             vmem_limit_bytes=VMEM_LIMIT),
    )(tok, h_ml, h_rw, h_gl, h_gd, z_ml, rw_post, z_gl, z_gd, modx, modc, normp, n2, fnw, wo, w1, w2)


def _pad_rows(a, rows=8):
    return jnp.concatenate([a, jnp.zeros((rows - a.shape[0],) + a.shape[1:], a.dtype)], axis=0)


def _misc_row(pairs):
    row = jnp.zeros((MISC_W,), F32)
    for col0, p in pairs:
        row = row.at[col0:col0 + N_DIR * HEADS].set(p.reshape(-1))
    return row


def _permute_w_in(w):
    d_model = w.shape[0]
    ml_w, rw_w, gl_w = 784, 1024, 784
    o_ml, o_rw, o_gl, o_gd = 0, ml_w, ml_w + rw_w, ml_w + rw_w + gl_w
    misc = jnp.concatenate([
        w[:, o_ml + 768:o_ml + 784],
        w[:, o_gl + 512:o_gl + 528],
        w[:, o_gd + 1024:o_gd + 1040],
        jnp.zeros((d_model, MISC_W - 48), w.dtype)], axis=1)
    return jnp.concatenate([
        w[:, o_ml:o_ml + 768],
        w[:, o_rw:o_rw + 1024],
        w[:, o_gl:o_gl + 512], w[:, o_gl + 528:o_gl + 784],
        w[:, o_gd:o_gd + 1024],
        misc], axis=1)


Z_WIDTHS = (768, 1024, 768, 1024, MISC_W)


SCAN_NB = 4


def _mix_layer(z_ml, z_rw, z_gl, z_gd, z_misc, p, n_ctx):
    nb = SCAN_NB if z_ml.shape[0] % SCAN_NB == 0 else 1
    s_ml = (((QK_W, 2 * GROUP_W), 0.0), ((1, GROUP_W), M_INIT))
    h_ml = _bidir_scan(_ml_chunk, [z_ml, z_misc], [z_ml, z_misc], [p['ml_gb']], s_ml, n_ctx, nb)
    h_gl = _bidir_scan(_gl_chunk, [z_gl, z_misc], [z_gl, z_misc], [p['gl_gup'], p['gl_gb']],
                       (((GROUP_W, QK_W), 0.0),), n_ctx, nb)
    qkv = _gd_prep(z_gd, p['gd_cw'], n_ctx)
    h_gd = _bidir_scan(_gd_chunk, [qkv, z_misc], [qkv, z_misc], [p['gd_gp']], (((GROUP_W, GROUP_W), 0.0),),
                       n_ctx, nb)
    base, d0, d1, rw_post = _rw_prep(z_rw, p['rw_mu'], p['rw_rp'], p['rw_lora'], p['rw_gup'], n_ctx)
    h_rw = _bidir_scan(_rw_chunk, [base, d0], [base, d1], [p['rw_rp']], (((GROUP_W, GROUP_W), 0.0),), n_ctx, nb)
    return h_ml, h_rw, h_gl, h_gd, rw_post


def _layer_params(layer, w_in, ml_ig_b, ml_fg_b, ml_norm_w, rw_mu_prev, rw_mu_next, rw_w0, rw_w_up, rw_a0,
                  rw_a_up, rw_g_up, rw_k_k, rw_k_a, rw_r_k, rw_gn_w, rw_gn_b, gl_gate_up, gl_gate_b, gl_norm_w,
                  gd_conv_w, gd_a_log, gd_dt_bias, gd_norm_w):
    l = layer
    z64 = jnp.zeros((64, GROUP_W), F32)
    lora = jnp.concatenate([
        jnp.concatenate([rw_w_up[l, 0], z64, rw_w_up[l, 1], z64], axis=1),
        jnp.concatenate([z64, rw_a_up[l, 0], z64, rw_a_up[l, 1]], axis=1)], axis=0)
    gup = jnp.zeros((N_DIR, MISC_W, QK_W), F32).at[:, MISC_GL_AL:MISC_GL_AL + 16, :].set(gl_gate_up[l])
    return {
        'w_in': _bf(_permute_w_in(w_in[l])),
        'ml_gb': _pad_rows(_misc_row([(MISC_ML_IG, ml_ig_b[l]), (MISC_ML_FG, ml_fg_b[l])])[None, :]),
        'gl_gup': gup,
        'gl_gb': _pad_rows(gl_gate_b[l]),
        'gd_cw': _pad_rows(gd_conv_w[l].reshape(9, 3 * GROUP_W), 16),
        'gd_gp': _pad_rows(jnp.stack([_misc_row([(MISC_GD_AL, gd_dt_bias[l])]), _misc_row([(MISC_GD_AL, gd_a_log[l])])])),
        'rw_mu': _pad_rows(jnp.stack([rw_mu_prev[l], rw_mu_next[l]])),
        'rw_rp': _pad_rows(jnp.stack([rw_w0[l, 0], rw_w0[l, 1], rw_a0[l, 0], rw_a0[l, 1],
                                      rw_k_k[l], rw_k_a[l], rw_r_k[l]])),
        'rw_lora': lora,
        'rw_gup': rw_g_up[l],
        'normp': _pad_rows(jnp.stack([ml_norm_w[l], rw_gn_w[l], rw_gn_b[l], gl_norm_w[l], gd_norm_w[l]])),
    }


def kernel(x, c, ctx, c_ctx, ada_w, ada_b, norm1_w, norm2_w, w_in, w_out, ml_ig_b, ml_fg_b, ml_norm_w, rw_mu_prev, rw_mu_next, rw_w0, rw_w_up, rw_a0, rw_a_up, rw_g_up, rw_k_k, rw_k_a, rw_r_k, rw_gn_w, rw_gn_b, gl_gate_up, gl_gate_b, gl_norm_w, gd_conv_w, gd_a_log, gd_dt_bias, gd_norm_w, mlp_w1, mlp_w2, final_norm_w):
    bsz, seq_x, d_model = x.shape
    n_ctx = ctx.shape[1]
    depth = w_in.shape[0]
    seq = n_ctx + seq_x
    tm = seq // 8 if seq % 64 == 0 else seq
    tok = jnp.concatenate([ctx, x], axis=1)
    cc = _pad_rows(jnp.concatenate([c, c_ctx[None, :]], axis=0), ((bsz + 1 + 7) // 8) * 8)
    mod = _ada_mod(cc, ada_w, ada_b)
    fnw = final_norm_w[None, :]
    for layer in range(depth):
        p = _layer_params(layer, w_in, ml_ig_b, ml_fg_b, ml_norm_w, rw_mu_prev, rw_mu_next, rw_w0, rw_w_up, rw_a0,
                          rw_a_up, rw_g_up, rw_k_k, rw_k_a, rw_r_k, rw_gn_w, rw_gn_b, gl_gate_up, gl_gate_b,
                          gl_norm_w, gd_conv_w, gd_a_log, gd_dt_bias, gd_norm_w)
        modx = jnp.pad(mod[layer, :bsz].reshape(bsz, 6, d_model), ((0, 0), (0, 2), (0, 0)))
        modc = _pad_rows(mod[layer, bsz].reshape(6, d_model))
        z_ml, z_rw, z_gl, z_gd, z_misc = _in_proj(tok, modx, modc, norm1_w[layer][None, :], p['w_in'], Z_WIDTHS,
                                                  n_ctx, tm)
        h_ml, h_rw, h_gl, h_gd, rw_post = _mix_layer(z_ml, z_rw, z_gl, z_gd, z_misc, p, n_ctx)
        tok = _out_mlp(tok, h_ml, h_rw, h_gl, h_gd, z_ml, rw_post, z_gl, z_gd, modx, modc, p['normp'],
                       norm2_w[layer][None, :], fnw, _bf(w_out[layer]), _bf(mlp_w1[layer]), _bf(mlp_w2[layer]),
                       n_ctx, tm, layer == depth - 1)
    return tok[:, n_ctx:]
```

```python
import functools

import jax
import jax.numpy as jnp
from jax import lax
from jax.experimental import pallas as pl
from jax.experimental.pallas import tpu as pltpu

F32 = jnp.float32
BF16 = jnp.bfloat16

CHUNK = 64
HEADS = 4
HEAD_V = 64
HEAD_QK = 32
GROUP_W = HEADS * HEAD_V
QK_W = HEADS * HEAD_QK
N_DIR = 2
EPS = 1e-6
GN_EPS = 64e-5
M_INIT = -1e30
RW_DECAY_SCALE = 0.6065306597126334
GLA_LOGIT_NORM = 16.0
MISC_W = 128
MISC_ML_IG, MISC_ML_FG, MISC_GL_AL, MISC_GD_BL, MISC_GD_AL = 0, 8, 16, 32, 40
VMEM_LIMIT = 56 * 1024 * 1024


def _mm(a, b, prec=None):
    return lax.dot_general(a, b, (((1,), (0,)), ((), ())), precision=prec, preferred_element_type=F32)


def _mm_nt(a, b, prec=None):
    return lax.dot_general(a, b, (((1,), (1,)), ((), ())), precision=prec, preferred_element_type=F32)


def _mm_tn(a, b, prec=None):
    return lax.dot_general(a, b, (((0,), (0,)), ((), ())), precision=prec, preferred_element_type=F32)


def _bf(x):
    return x.astype(BF16)


def _iota(shape, dim):
    return lax.broadcasted_iota(jnp.int32, shape, dim)


def _sigmoid(x):
    return 1.0 / (1.0 + jnp.exp(-x))


def _softplus(x):
    return jnp.maximum(x, 0.0) + jnp.log1p(jnp.exp(-jnp.abs(x)))


def _log_sigmoid(x):
    return -_softplus(-x)


def _silu(x):
    return x * _sigmoid(x)


def _time_mask(d, width, strict):
    i = _iota((CHUNK, width), 0)
    j = _iota((CHUNK, width), 1) & (CHUNK - 1)
    if d == 0:
        return (j < i) if strict else (j <= i)
    return (j > i) if strict else (j >= i)


def _cumsum_mat(d):
    return jnp.where(_time_mask(d, CHUNK, False), 1.0, 0.0).astype(F32)


def _eye_cat(width):
    i = _iota((CHUNK, width), 0)
    j = _iota((CHUNK, width), 1) & (CHUNK - 1)
    return jnp.where(i == j, 1.0, 0.0).astype(F32)


def _bd_mask(rows, cols, row_shift, col_shift, col_and=None):
    r = _iota((rows, cols), 0) >> row_shift
    c = _iota((rows, cols), 1)
    if col_and is not None:
        c = c & col_and
    return r == (c >> col_shift)


def _stack_bd(x, col_shift):
    w = x.shape[1]
    m = _bd_mask(HEADS * CHUNK, w, 6, col_shift)
    return jnp.where(m, jnp.concatenate([_bf(x)] * HEADS, axis=0), jnp.zeros((), BF16))


def _split(x, n):
    out, r = [], x
    for i in range(n):
        p = r.astype(BF16)
        out.append(p)
        if i + 1 < n:
            r = r - p.astype(F32)
    return out


def _sel_left(t, x, n=2):
    return _mm(jnp.concatenate([_bf(t)] * n, axis=1), jnp.concatenate(_split(x, n), axis=0))


def _sel_right(x, e, n=2):
    return _mm(jnp.concatenate(_split(x, n), axis=1), jnp.concatenate([_bf(e)] * n, axis=0))


def _mm3(a, b):
    ah, al = _split(a, 2)
    bh, bl = _split(b, 2)
    return _mm(jnp.concatenate([ah, ah, al], axis=1), jnp.concatenate([bh, bl, bh], axis=0))


def _mm3_bd(a, bs):
    ah, al = _split(a, 2)
    pieces = [_split(b, 2) for b in bs]
    bh = jnp.concatenate([_stack_bd(p[0], 6) for p in pieces], axis=1)
    bl = jnp.concatenate([_stack_bd(p[1], 6) for p in pieces], axis=1)
    return _mm(jnp.concatenate([ah, ah, al], axis=1), jnp.concatenate([bh, bl, bh], axis=0))


def _cumsum_t(d, x):
    return _sel_left(_cumsum_mat(d), x)


def _expand_mat(col0s):
    r = _iota((MISC_W, GROUP_W), 0)
    hcol = _iota((MISC_W, GROUP_W), 1) >> 6
    return jnp.concatenate([jnp.where(r == hcol + c0, 1.0, 0.0) for c0 in col0s], axis=1).astype(F32)


def _to_cols(x):
    return _sel_left(jnp.ones((CHUNK, CHUNK), F32), x * _eye_cat(GROUP_W))


def _block_sum(x):
    w = x.shape[1]
    ones_bd = jnp.where(_bd_mask(w, w, 6, 6), 1.0, 0.0).astype(F32)
    return _sel_right(x, ones_bd, 2)


def _block_max(x):
    blk = _iota(x.shape, 1) >> 6
    out = jnp.zeros_like(x)
    for h in range(HEADS):
        sel = blk == h
        mh = jnp.max(jnp.where(sel, x, -jnp.inf), axis=1, keepdims=True)
        out = jnp.where(sel, mh, out)
    return out


def _inv_unit(n):
    r = _eye_cat(GROUP_W) + n
    p = n
    for it in range(5):
        if it == 0:
            p = _mm3_bd(p, [p])
            yield
        else:
            pr = _mm3_bd(jnp.concatenate([p, r], axis=0), [p])
            yield
            p = pr[:CHUNK]
            r = r + pr[CHUNK:]
    out = _mm3_bd(r, [p])
    yield
    return r + out


def _l2n_heads(t):
    return t * lax.rsqrt(_block_sum(t * t) + EPS)


def _ml_chunk(d, xs, cs, st):
    z, misc = xs
    (gb,) = cs
    s_mat, m_row = st
    last = CHUNK - 1 if d == 0 else 0
    q = z[:, 0:QK_W] * (HEAD_QK ** -0.5)
    k = z[:, QK_W:2 * QK_W]
    v = z[:, 2 * QK_W:2 * QK_W + GROUP_W]
    t = misc + gb[0:1, :]
    is_fg = (_iota(misc.shape, 1) >> 3) == (MISC_ML_FG >> 3)
    y = jnp.where(is_fg, _log_sigmoid(t), t)
    cum = _cumsum_t(d, y)
    a = _mm_nt(_bf(q), _stack_bd(k, 5))
    inter = _mm(_bf(q), _bf(s_mat))
    yield
    ex = _sel_right(jnp.where(is_fg, cum, y), _expand_mat((MISC_ML_IG + HEADS * d, MISC_ML_FG + HEADS * d)))
    yield
    ig, bcum = ex[:, :GROUP_W], ex[:, GROUP_W:]
    b_end = bcum[last:last + 1, :]
    w_end = b_end - bcum + ig
    m_loc = jnp.max(w_end, axis=0, keepdims=True)
    e_end = jnp.exp(w_end - m_loc)
    cols = _to_cols(ig - bcum)
    loc = _mm_tn(_bf(k), _bf(jnp.concatenate([v * e_end, e_end], axis=1)))
    yield
    loc = jnp.where(_bd_mask(QK_W, 2 * GROUP_W, 5, 6, GROUP_W - 1), loc, 0.0)
    m_prev_s = b_end + m_row
    m_new = jnp.maximum(m_prev_s, m_loc)
    f_s = jnp.exp(m_prev_s - m_new)
    f_l = jnp.exp(m_loc - m_new)
    s_new = jnp.concatenate([f_s, f_s], axis=1) * s_mat + jnp.concatenate([f_l, f_l], axis=1) * loc
    logd = jnp.where(_time_mask(d, GROUP_W, False), bcum + cols, -jnp.inf)
    m_prev = bcum + m_row
    m_i = jnp.maximum(m_prev, _block_max(logd))
    sm = a * jnp.exp(logd - m_i)
    e_prev = jnp.exp(m_prev - m_i)
    ones_bd = jnp.where(_bd_mask(GROUP_W, GROUP_W, 6, 6), 1.0, 0.0).astype(BF16)
    intra = _mm(_bf(sm), jnp.concatenate([_stack_bd(v, 6), ones_bd], axis=1))
    yield
    num = intra[:, :GROUP_W] + e_prev * inter[:, :GROUP_W]
    den = intra[:, GROUP_W:] + e_prev * inter[:, GROUP_W:]
    h = num / jnp.maximum(jnp.abs(den), jnp.exp(-m_i))
    return h, (s_new, m_new)


def _gl_chunk(d, xs, cs, st):
    z, misc = xs
    gup, gb = cs
    (s_mat,) = st
    last = CHUNK - 1 if d == 0 else 0
    mid = CHUNK // 2 if d == 0 else CHUNK - 1 - CHUNK // 2
    q = z[:, 0:QK_W] * (HEAD_QK ** -0.5)
    k = z[:, QK_W:2 * QK_W]
    v = z[:, 2 * QK_W:2 * QK_W + GROUP_W]
    pre = _mm3(misc, gup[d])
    yield
    la = _log_sigmoid(pre + gb[d:d + 1, :]) * (1.0 / GLA_LOGIT_NORM)
    g = _cumsum_t(d, la)
    yield
    g_end = g[last:last + 1, :]
    g_mid = g[mid:mid + 1, :]
    qd = q * jnp.exp(g - g_mid)
    kd = k * jnp.exp(g_mid - g)
    att = _mm_nt(_bf(qd), _stack_bd(kd, 5))
    s_loc = _mm_tn(_bf(v), _bf(k * jnp.exp(g_end - g)))
    o_inter = _mm_nt(_bf(q * jnp.exp(g)), _bf(s_mat))
    yield
    att = jnp.where(_time_mask(d, GROUP_W, False), att, 0.0)
    s_new = jnp.exp(g_end) * s_mat + jnp.where(_bd_mask(GROUP_W, QK_W, 6, 5), s_loc, 0.0)
    o_intra = _mm(_bf(att), _stack_bd(v, 6))
    yield
    return o_inter + o_intra, (s_new,)


def _gd_chunk(d, xs, cs, st):
    qkv, misc = xs
    (gp,) = cs
    (s_mat,) = st
    last = CHUNK - 1 if d == 0 else 0
    q = qkv[:, 0:GROUP_W] * (HEAD_V ** -0.5)
    k = qkv[:, GROUP_W:2 * GROUP_W]
    v = qkv[:, 2 * GROUP_W:3 * GROUP_W]
    t = misc + gp[0:1, :]
    is_al = (_iota(misc.shape, 1) >> 3) == (MISC_GD_AL >> 3)
    y = jnp.where(is_al, -jnp.exp(gp[1:2, :]) * _softplus(t), _sigmoid(t))
    cum = _cumsum_t(d, y)
    x2 = _mm_nt(_bf(jnp.concatenate([k, q], axis=0)), _stack_bd(k, 6))
    yield
    ex = _sel_right(jnp.where(is_al, cum, y), _expand_mat((MISC_GD_BL + HEADS * d, MISC_GD_AL + HEADS * d)))
    yield
    beta, g = ex[:, :GROUP_W], ex[:, GROUP_W:]
    g_end = g[last:last + 1, :]
    cols = _to_cols(g)
    yield
    dec0 = jnp.where(_time_mask(d, GROUP_W, False), jnp.exp(g - cols), 0.0)
    kk, qk = x2[:CHUNK], x2[CHUNK:]
    mm = jnp.where(_time_mask(d, GROUP_W, True), beta * kk * dec0, 0.0)
    ainv = yield from _inv_unit(-mm)
    eg = jnp.exp(g)
    sol = _mm3_bd(ainv, [v * beta, k * (beta * eg)])
    o_inter = _mm(_bf(q * eg), _bf(s_mat))
    yield
    u, w = sol[:, :GROUP_W], sol[:, GROUP_W:]
    ws = _mm(_bf(w), _bf(s_mat))
    yield
    v_new = u - ws
    k_end = k * jnp.exp(g_end - g)
    upd = _mm_tn(_bf(k_end), _bf(v_new))
    o_intra = _mm(_bf(qk * dec0), _stack_bd(v_new, 6))
    yield
    s_new = jnp.exp(g_end) * s_mat + jnp.where(_bd_mask(GROUP_W, GROUP_W, 6, 6), upd, 0.0)
    return o_inter + o_intra, (s_new,)


def _rw_chunk(d, xs, cs, st):
    base, dirp = xs
    (rp,) = cs
    (s_mat,) = st
    last = CHUNK - 1 if d == 0 else 0
    r = base[:, 0:GROUP_W]
    k = base[:, GROUP_W:2 * GROUP_W]
    v = base[:, 2 * GROUP_W:3 * GROUP_W]
    kk = base[:, 3 * GROUP_W:4 * GROUP_W]
    lw = dirp[:, 0:GROUP_W]
    a = dirp[:, GROUP_W:2 * GROUP_W]
    k_a = rp[5:6, :]
    kt = k * (1.0 + (a - 1.0) * k_a)
    ap = kk * a
    g = _cumsum_t(d, lw)
    yield
    g_end = g[last:last + 1, :]
    eg = jnp.exp(g)
    eng = jnp.exp(-g)
    r_h = r * eg
    a_h = ap * eng
    k_h = kt * eng
    b_h = -kk * jnp.exp(g - lw)
    x = _mm_nt(_bf(jnp.concatenate([b_h, r_h], axis=0)),
               jnp.concatenate([_stack_bd(a_h, 6), _stack_bd(k_h, 6)], axis=0))
    y_inter = _mm_nt(_bf(r_h), _bf(s_mat))
    yield
    tm1 = _time_mask(d, GROUP_W, True)
    tm0 = _time_mask(d, GROUP_W, False)
    m_ab = jnp.where(tm1, x[:CHUNK, :GROUP_W], 0.0)
    m_bk = jnp.where(tm1, x[:CHUNK, GROUP_W:], 0.0)
    att_a = jnp.where(tm0, x[CHUNK:, :GROUP_W], 0.0)
    att_k = jnp.where(tm0, x[CHUNK:, GROUP_W:], 0.0)
    vbd = _stack_bd(v, 6)
    rhs2 = _mm(_bf(m_bk), vbd)
    ainv = yield from _inv_unit(m_ab)
    sol = _mm3_bd(ainv, [b_h, rhs2])
    yield
    w_mat, u0 = sol[:, :GROUP_W], sol[:, GROUP_W:]
    ws = _mm_nt(_bf(w_mat), _bf(s_mat))
    yield
    u = u0 + ws
    dec_end = jnp.exp(g_end - g)
    upd = _mm_tn(_bf(jnp.concatenate([u, v], axis=0)), _bf(jnp.concatenate([ap * dec_end, kt * dec_end], axis=0)))
    y_intra = _mm(_bf(jnp.concatenate([att_a, att_k], axis=1)), jnp.concatenate([_stack_bd(u, 6), vbd], axis=0))
    yield
    s_new = jnp.exp(g_end) * s_mat + jnp.where(_bd_mask(GROUP_W, GROUP_W, 6, 6), upd, 0.0)
    return y_inter + y_intra, (s_new,)


def _scan_kernel(chunk_fn, nb, n_in, n_cst, state_init, n_ctx_chunks, n_chunks, *refs):
    in_refs = (refs[:n_in], refs[n_in:2 * n_in])
    cst_refs = refs[2 * n_in:2 * n_in + n_cst]
    o_ref = refs[2 * n_in + n_cst]
    n_st = len(state_init)
    st_refs = refs[2 * n_in + n_cst + 1:]
    s = pl.program_id(1)

    @pl.when(s == 0)
    def _():
        o_ref[...] = jnp.zeros(o_ref.shape, o_ref.dtype)
        for i, (shape, val) in enumerate(state_init):
            st_refs[i][...] = jnp.full((N_DIR * nb,) + shape, val, F32)

    cs = tuple(c[...] for c in cst_refs)
    chunk_b = jnp.where(s < n_ctx_chunks, n_ctx_chunks - 1 - s, n_chunks - 1 + n_ctx_chunks - s)
    chains = [(bi, d) for bi in range(nb) for d in range(N_DIR)]
    gens = []
    for bi, d in chains:
        xs = tuple(r[bi] for r in in_refs[d])
        st = tuple(st_refs[i][d * nb + bi] for i in range(n_st))
        gens.append(chunk_fn(d, xs, cs, st))
    results = [None] * len(chains)
    while any(r is None for r in results):
        for ci, gen in enumerate(gens):
            if results[ci] is None:
                try:
                    next(gen)
                except StopIteration as stop:
                    results[ci] = stop.value
    for (bi, d), (y, st_new) in zip(chains, results):
        for i in range(n_st):
            st_refs[i][d * nb + bi] = st_new[i]
        row = pl.multiple_of((s if d == 0 else chunk_b) * CHUNK, CHUNK)
        o_ref[bi, pl.ds(row, CHUNK), :] += y


def _bidir_scan(chunk_fn, ins_f, ins_b, csts, state_init, n_ctx, nb):
    bsz, seq = ins_f[0].shape[0], ins_f[0].shape[1]
    n_chunks, n_ctx_chunks = seq // CHUNK, n_ctx // CHUNK

    def idx_f(b, s):
        return (b, s, 0)

    def idx_b(b, s):
        return (b, jnp.where(s < n_ctx_chunks, n_ctx_chunks - 1 - s, n_chunks - 1 + n_ctx_chunks - s), 0)

    in_specs = [pl.BlockSpec((nb, CHUNK, a.shape[2]), idx_f) for a in ins_f]
    in_specs += [pl.BlockSpec((nb, CHUNK, a.shape[2]), idx_b) for a in ins_b]
    in_specs += [pl.BlockSpec(c.shape, lambda b, s, nd=c.ndim: (0,) * nd) for c in csts]
    scratch = [pltpu.VMEM((N_DIR * nb,) + shape, F32) for (shape, _) in state_init]
    kern = functools.partial(_scan_kernel, chunk_fn, nb, len(ins_f), len(csts), state_init, n_ctx_chunks, n_chunks)
    return pl.pallas_call(
        kern,
        name=chunk_fn.__name__.strip("_") + "_scan",
        grid=(bsz // nb, n_chunks),
        in_specs=in_specs,
        out_specs=pl.BlockSpec((nb, seq, GROUP_W), lambda b, s: (b, 0, 0)),
        out_shape=jax.ShapeDtypeStruct((bsz, seq, GROUP_W), F32),
        scratch_shapes=scratch,
        compiler_params=pltpu.CompilerParams(dimension_semantics=("arbitrary", "arbitrary"),
                                             vmem_limit_bytes=VMEM_LIMIT),
    )(*ins_f, *ins_b, *csts)


def _ada_kernel(c_ref, w_ref, b_ref, o_ref):
    o_ref[0] = _mm(_bf(_silu(c_ref[...])), _bf(w_ref[0])) + b_ref[0]


def _ada_mod(cc, ada_w, ada_b):
    depth, d_model, n6 = ada_w.shape
    tn = 1536
    return pl.pallas_call(
        _ada_kernel,
        name="ada_mod",
        grid=(depth, n6 // tn),
        in_specs=[pl.BlockSpec(cc.shape, lambda l, j: (0, 0)),
                  pl.BlockSpec((1, d_model, tn), lambda l, j: (l, 0, j)),
                  pl.BlockSpec((1, 1, tn), lambda l, j: (l, 0, j))],
        out_specs=pl.BlockSpec((1, cc.shape[0], tn), lambda l, j: (l, 0, j)),
        out_shape=jax.ShapeDtypeStruct((depth, cc.shape[0], n6), F32),
        compiler_params=pltpu.CompilerParams(dimension_semantics=("arbitrary", "arbitrary"),
                                             vmem_limit_bytes=VMEM_LIMIT),
    )(cc, ada_w, ada_b.reshape(depth, 1, n6))


def _rms_rows(x, gain):
    return x * lax.rsqrt(jnp.mean(x * x, axis=-1, keepdims=True) + EPS) * gain


def _pick_mod(modx_ref, modc_ref, is_ctx, row):
    return jnp.where(is_ctx, modc_ref[row:row + 1, :], modx_ref[0, row:row + 1, :])


def _in_proj_kernel(n_ctx, tm, widths, tok_ref, modx_ref, modc_ref, nw_ref, w_ref, *out_refs):
    t = pl.program_id(1)
    x = tok_ref[0]
    is_ctx = (t * tm + _iota((tm, 1), 0)) < n_ctx
    shift = _pick_mod(modx_ref, modc_ref, is_ctx, 0)
    scale = _pick_mod(modx_ref, modc_ref, is_ctx, 1)
    h = _bf(_rms_rows(x, nw_ref[...]) * (1.0 + scale) + shift)
    c0 = 0
    for o_ref, w in zip(out_refs, widths):
        o_ref[0] = _mm(h, w_ref[:, c0:c0 + w])
        c0 += w


def _in_proj(tok, modx, modc, nw, w_p, widths, n_ctx, tm):
    bsz, seq, d_model = tok.shape
    kern = functools.partial(_in_proj_kernel, n_ctx, tm, widths)
    return pl.pallas_call(
        kern,
        name="in_proj",
        grid=(bsz, seq // tm),
        in_specs=[pl.BlockSpec((1, tm, d_model), lambda b, t: (b, t, 0)),
                  pl.BlockSpec((1, 8, d_model), lambda b, t: (b, 0, 0)),
                  pl.BlockSpec((8, d_model), lambda b, t: (0, 0)),
                  pl.BlockSpec((1, d_model), lambda b, t: (0, 0)),
                  pl.BlockSpec(w_p.shape, lambda b, t: (0, 0), pipeline_mode=pl.Buffered(1))],
        out_specs=[pl.BlockSpec((1, tm, w), lambda b, t: (b, t, 0)) for w in widths],
        out_shape=[jax.ShapeDtypeStruct((bsz, seq, w), F32) for w in widths],
        compiler_params=pltpu.CompilerParams(dimension_semantics=("arbitrary", "arbitrary"),
                                             vmem_limit_bytes=VMEM_LIMIT),
    )(tok, modx, modc, nw, w_p)


def _row_tile(seq, cap, mult):
    best = mult
    for r in range(mult, min(seq, cap) + 1, mult):
        if seq % r == 0:
            best = r
    return best


def _rw_prep_kernel(n_ctx, seq, rows, z_ref, zp_ref, zn_ref, mu_ref, rp_ref, lora_ref, gup_ref,
                    base_ref, d0_ref, d1_ref, post_ref):
    t = pl.program_id(1)
    z = z_ref[0]
    row = _iota((rows, 1), 0)
    pos = t * rows + row
    z_prev = jnp.where(row == 0, zp_ref[0, 7:8, :], pltpu.roll(z, 1, 0))
    z_prev = jnp.where((pos != 0) & (pos != n_ctx), z_prev, 0.0)
    z_next = jnp.where(row == rows - 1, zn_ref[0, 0:1, :], pltpu.roll(z, rows - 1, 0))
    z_next = jnp.where((pos != seq - 1) & (pos != n_ctx - 1), z_next, 0.0)
    zm = z + mu_ref[0:1, :] * (z_prev - z) + mu_ref[1:2, :] * (z_next - z)
    r = zm[:, 0:GROUP_W]
    k = zm[:, GROUP_W:2 * GROUP_W]
    v = zm[:, 2 * GROUP_W:3 * GROUP_W]
    lo = zm[:, 3 * GROUP_W:3 * GROUP_W + 128]
    gl = zm[:, 3 * GROUP_W + 128:4 * GROUP_W]
    lo = jnp.where(_iota(lo.shape, 1) < 64, jnp.tanh(lo), lo)
    pre = _mm3(lo, lora_ref[...])
    rp = rp_ref[...]
    kk = _l2n_heads(k * rp[4:5, :])
    lw0 = -RW_DECAY_SCALE * _sigmoid(rp[0:1, :] + pre[:, 0:GROUP_W])
    a0 = _sigmoid(rp[2:3, :] + pre[:, GROUP_W:2 * GROUP_W])
    lw1 = -RW_DECAY_SCALE * _sigmoid(rp[1:2, :] + pre[:, 2 * GROUP_W:3 * GROUP_W])
    a1 = _sigmoid(rp[3:4, :] + pre[:, 3 * GROUP_W:4 * GROUP_W])
    k_a = rp[5:6, :]
    k_sum = k * (1.0 + (a0 - 1.0) * k_a) + k * (1.0 + (a1 - 1.0) * k_a)
    gate = _mm(_bf(_sigmoid(gl)), _bf(gup_ref[...]))
    bonus = _block_sum(r * k_sum * rp[6:7, :]) * v
    base_ref[0] = jnp.concatenate([r, k, v, kk], axis=1)
    d0_ref[0] = jnp.concatenate([lw0, a0], axis=1)
    d1_ref[0] = jnp.concatenate([lw1, a1], axis=1)
    post_ref[0] = jnp.concatenate([gate, bonus], axis=1)


def _rw_prep(z_rw, mu, rp, lora, gup, n_ctx):
    bsz, seq, w = z_rw.shape
    rows = _row_tile(seq, 640, 8)
    sub, n_sub = rows // 8, seq // 8
    kern = functools.partial(_rw_prep_kernel, n_ctx, seq, rows)
    full = lambda a: pl.BlockSpec(a.shape, lambda b, t, nd=a.ndim: (0,) * nd)
    widths = (4 * GROUP_W, 2 * GROUP_W, 2 * GROUP_W, 2 * GROUP_W)
    return pl.pallas_call(
        kern,
        name="rw_prep",
        grid=(bsz, seq // rows),
        in_specs=[pl.BlockSpec((1, rows, w), lambda b, t: (b, t, 0)),
                  pl.BlockSpec((1, 8, w), lambda b, t: (b, jnp.maximum(t * sub - 1, 0), 0)),
                  pl.BlockSpec((1, 8, w), lambda b, t: (b, jnp.minimum((t + 1) * sub, n_sub - 1), 0)),
                  full(mu), full(rp), full(lora), full(gup)],
        out_specs=[pl.BlockSpec((1, rows, ow), lambda b, t: (b, t, 0)) for ow in widths],
        out_shape=[jax.ShapeDtypeStruct((bsz, seq, ow), F32) for ow in widths],
        compiler_params=pltpu.CompilerParams(dimension_semantics=("arbitrary", "arbitrary"),
                                             vmem_limit_bytes=VMEM_LIMIT),
    )(z_rw, z_rw, z_rw, mu, rp, lora, gup)


def _gd_prep_kernel(n_ctx, seq, rows, zc_ref, zu_ref, zd_ref, cw_ref, o_ref):
    t = pl.program_id(1)
    cw = cw_ref[...]
    ext = jnp.concatenate([zu_ref[0], zc_ref[0], zd_ref[0]], axis=0)
    n_ext = rows + 2 * CHUNK
    pos_e = t * rows - CHUNK + _iota((n_ext, 1), 0)
    col = pos_e & (CHUNK - 1)
    src_ctx = pos_e < n_ctx
    has_l = jnp.where(src_ctx, pos_e, col) >= 1
    has_r = jnp.where(src_ctx, pos_e - (n_ctx - 2), col - (CHUNK - 2)) <= 0
    e_l = jnp.where(has_l, pltpu.roll(ext, 1, 0), 0.0)
    e_r = jnp.where(has_r, pltpu.roll(ext, n_ext - 1, 0), 0.0)

    def taps(di):
        lo = CHUNK * (1 + di)
        i = 3 * (1 + di)
        return (cw[i:i + 1, :] * e_l[lo:lo + rows] + cw[i + 1:i + 2, :] * ext[lo:lo + rows]
                + cw[i + 2:i + 3, :] * e_r[lo:lo + rows])

    pos_o = t * rows + _iota((rows, 1), 0)
    acc = taps(0)
    acc = acc + jnp.where(pos_o - CHUNK >= n_ctx, taps(-1), 0.0)
    acc = acc + jnp.where((pos_o >= n_ctx) & (pos_o + CHUNK < seq), taps(1), 0.0)
    y = _silu(acc)
    q = _l2n_heads(y[:, 0:GROUP_W])
    k = _l2n_heads(y[:, GROUP_W:2 * GROUP_W])
    o_ref[0] = jnp.concatenate([q, k, y[:, 2 * GROUP_W:]], axis=1)


def _gd_prep(z_gd, cw, n_ctx):
    bsz, seq, _ = z_gd.shape
    w = 3 * GROUP_W
    rows = _row_tile(seq, 1088, CHUNK)
    per, n_chunks = rows // CHUNK, seq // CHUNK
    kern = functools.partial(_gd_prep_kernel, n_ctx, seq, rows)
    return pl.pallas_call(
        kern,
        name="gd_prep",
        grid=(bsz, seq // rows),
        in_specs=[pl.BlockSpec((1, rows, w), lambda b, t: (b, t, 0)),
                  pl.BlockSpec((1, CHUNK, w), lambda b, t: (b, jnp.maximum(t * per - 1, 0), 0)),
                  pl.BlockSpec((1, CHUNK, w), lambda b, t: (b, jnp.minimum((t + 1) * per, n_chunks - 1), 0)),
                  pl.BlockSpec(cw.shape, lambda b, t: (0, 0))],
        out_specs=pl.BlockSpec((1, rows, w), lambda b, t: (b, t, 0)),
        out_shape=jax.ShapeDtypeStruct((bsz, seq, w), F32),
        compiler_params=pltpu.CompilerParams(dimension_semantics=("arbitrary", "arbitrary"),
                                             vmem_limit_bytes=VMEM_LIMIT),
    )(z_gd, z_gd, z_gd, cw)


def _head_rms(h, gain):
    return h * lax.rsqrt(_block_sum(h * h) * (1.0 / HEAD_V) + EPS) * gain


def _head_groupnorm(h, gain, bias):
    dlt = h - _block_sum(h) * (1.0 / HEAD_V)
    return dlt * lax.rsqrt(_block_sum(dlt * dlt) * (1.0 / HEAD_V) + GN_EPS) * gain + bias


def _out_mlp_kernel(n_ctx, tm, ff_chunk, final, tok_ref, hml_ref, hrw_ref, hgl_ref, hgd_ref, oml_ref, prw_ref,
                    ogl_ref, zgd_ref, modx_ref, modc_ref, np_ref, n2_ref, fn_ref, wo_ref, w1_ref, w2_ref, o_ref):
    t = pl.program_id(1)
    x = tok_ref[0]
    is_ctx = (t * tm + _iota((tm, 1), 0)) < n_ctx
    np_ = np_ref[...]
    prw = prw_ref[0]
    mixed = jnp.concatenate([
        _head_rms(hml_ref[0], np_[0:1, :]) * _sigmoid(oml_ref[0]),
        (_head_groupnorm(hrw_ref[0], np_[1:2, :], np_[2:3, :]) + prw[:, GROUP_W:]) * prw[:, :GROUP_W],
        _head_rms(hgl_ref[0], np_[3:4, :]) * _silu(ogl_ref[0]),
        _head_rms(hgd_ref[0], np_[4:5, :]) * _silu(zgd_ref[0]),
    ], axis=1)
    x = x + _pick_mod(modx_ref, modc_ref, is_ctx, 2) * _mm(_bf(mixed), wo_ref[...])
    shift = _pick_mod(modx_ref, modc_ref, is_ctx, 3)
    scale = _pick_mod(modx_ref, modc_ref, is_ctx, 4)
    h = _bf(_rms_rows(x, n2_ref[...]) * (1.0 + scale) + shift)
    d_ff = w1_ref.shape[1]
    acc = jnp.zeros(x.shape, F32)
    for c0 in range(0, d_ff, ff_chunk):
        a = jnp.maximum(_mm(h, w1_ref[:, c0:c0 + ff_chunk]), 0.0)
        acc = acc + _mm(_bf(a * a), w2_ref[c0:c0 + ff_chunk, :])
    x = x + _pick_mod(modx_ref, modc_ref, is_ctx, 5) * acc
    if final:
        x = _rms_rows(x, fn_ref[...])
    o_ref[0] = x


def _out_mlp(tok, h_ml, h_rw, h_gl, h_gd, z_ml, rw_post, z_gl, z_gd, modx, modc, normp, n2, fnw, wo, w1, w2,
             n_ctx, tm, final):
    bsz, seq, d_model = tok.shape
    kern = functools.partial(_out_mlp_kernel, n_ctx, tm, 1024, final)
    row = lambda w, j=0: pl.BlockSpec((1, tm, w), lambda b, t, j=j: (b, t, j))
    const = lambda a: pl.BlockSpec(a.shape, lambda b, t, nd=a.ndim: (0,) * nd)
    wconst = lambda a: pl.BlockSpec(a.shape, lambda b, t: (0, 0), pipeline_mode=pl.Buffered(1))
    return pl.pallas_call(
        kern,
        name="out_mlp",
        grid=(bsz, seq // tm),
        in_specs=[row(d_model), row(GROUP_W), row(GROUP_W), row(GROUP_W), row(GROUP_W),
                  row(GROUP_W, 2), row(2 * GROUP_W), row(GROUP_W, 2), row(GROUP_W, 3),
                  pl.BlockSpec((1, 8, d_model), lambda b, t: (b, 0, 0)), const(modc), const(normp), const(n2),
                  const(fnw), wconst(wo), wconst(w1), wconst(w2)],
        out_specs=row(d_model),
        out_shape=jax.ShapeDtypeStruct((bsz, seq, d_model), F32),
        compiler_params=pltpu.CompilerParams(dimension_semantics=("arbitrary", "arbitrary"),
                                             vmem_limit_bytes=VMEM_LIMIT),
    )(tok, h_ml, h_rw, h_gl, h_gd, z_ml, rw_post, z_gl, z_gd, modx, modc, normp, n2, fnw, wo, w1, w2)


def _pad_rows(a, rows=8):
    return jnp.concatenate([a, jnp.zeros((rows - a.shape[0],) + a.shape[1:], a.dtype)], axis=0)


def _misc_row(pairs):
    row = jnp.zeros((MISC_W,), F32)
    for col0, p in pairs:
        row = row.at[col0:col0 + N_DIR * HEADS].set(p.reshape(-1))
    return row


def _permute_w_in(w):
    d_model = w.shape[0]
    ml_w, rw_w, gl_w = 784, 1024, 784
    o_ml, o_rw, o_gl, o_gd = 0, ml_w, ml_w + rw_w, ml_w + rw_w + gl_w
    misc = jnp.concatenate([
        w[:, o_ml + 768:o_ml + 784],
        w[:, o_gl + 512:o_gl + 528],
        w[:, o_gd + 1024:o_gd + 1040],
        jnp.zeros((d_model, MISC_W - 48), w.dtype)], axis=1)
    return jnp.concatenate([
        w[:, o_ml:o_ml + 768],
        w[:, o_rw:o_rw + 1024],
        w[:, o_gl:o_gl + 512], w[:, o_gl + 528:o_gl + 784],
        w[:, o_gd:o_gd + 1024],
        misc], axis=1)


Z_WIDTHS = (768, 1024, 768, 1024, MISC_W)


SCAN_NB = 4


def _mix_layer(z_ml, z_rw, z_gl, z_gd, z_misc, p, n_ctx):
    nb = SCAN_NB if z_ml.shape[0] % SCAN_NB == 0 else 1
    s_ml = (((QK_W, 2 * GROUP_W), 0.0), ((1, GROUP_W), M_INIT))
    h_ml = _bidir_scan(_ml_chunk, [z_ml, z_misc], [z_ml, z_misc], [p['ml_gb']], s_ml, n_ctx, nb)
    h_gl = _bidir_scan(_gl_chunk, [z_gl, z_misc], [z_gl, z_misc], [p['gl_gup'], p['gl_gb']],
                       (((GROUP_W, QK_W), 0.0),), n_ctx, nb)
    qkv = _gd_prep(z_gd, p['gd_cw'], n_ctx)
    h_gd = _bidir_scan(_gd_chunk, [qkv, z_misc], [qkv, z_misc], [p['gd_gp']], (((GROUP_W, GROUP_W), 0.0),),
                       n_ctx, nb)
    base, d0, d1, rw_post = _rw_prep(z_rw, p['rw_mu'], p['rw_rp'], p['rw_lora'], p['rw_gup'], n_ctx)
    h_rw = _bidir_scan(_rw_chunk, [base, d0], [base, d1], [p['rw_rp']], (((GROUP_W, GROUP_W), 0.0),), n_ctx, nb)
    return h_ml, h_rw, h_gl, h_gd, rw_post


def _layer_params(layer, w_in, ml_ig_b, ml_fg_b, ml_norm_w, rw_mu_prev, rw_mu_next, rw_w0, rw_w_up, rw_a0,
                  rw_a_up, rw_g_up, rw_k_k, rw_k_a, rw_r_k, rw_gn_w, rw_gn_b, gl_gate_up, gl_gate_b, gl_norm_w,
                  gd_conv_w, gd_a_log, gd_dt_bias, gd_norm_w):
    l = layer
    z64 = jnp.zeros((64, GROUP_W), F32)
    lora = jnp.concatenate([
        jnp.concatenate([rw_w_up[l, 0], z64, rw_w_up[l, 1], z64], axis=1),
        jnp.concatenate([z64, rw_a_up[l, 0], z64, rw_a_up[l, 1]], axis=1)], axis=0)
    gup = jnp.zeros((N_DIR, MISC_W, QK_W), F32).at[:, MISC_GL_AL:MISC_GL_AL + 16, :].set(gl_gate_up[l])
    return {
        'w_in': _bf(_permute_w_in(w_in[l])),
        'ml_gb': _pad_rows(_misc_row([(MISC_ML_IG, ml_ig_b[l]), (MISC_ML_FG, ml_fg_b[l])])[None, :]),
        'gl_gup': gup,
        'gl_gb': _pad_rows(gl_gate_b[l]),
        'gd_cw': _pad_rows(gd_conv_w[l].reshape(9, 3 * GROUP_W), 16),
        'gd_gp': _pad_rows(jnp.stack([_misc_row([(MISC_GD_AL, gd_dt_bias[l])]), _misc_row([(MISC_GD_AL, gd_a_log[l])])])),
        'rw_mu': _pad_rows(jnp.stack([rw_mu_prev[l], rw_mu_next[l]])),
        'rw_rp': _pad_rows(jnp.stack([rw_w0[l, 0], rw_w0[l, 1], rw_a0[l, 0], rw_a0[l, 1],
                                      rw_k_k[l], rw_k_a[l], rw_r_k[l]])),
        'rw_lora': lora,
        'rw_gup': rw_g_up[l],
        'normp': _pad_rows(jnp.stack([ml_norm_w[l], rw_gn_w[l], rw_gn_b[l], gl_norm_w[l], gd_norm_w[l]])),
    }


def kernel(x, c, ctx, c_ctx, ada_w, ada_b, norm1_w, norm2_w, w_in, w_out, ml_ig_b, ml_fg_b, ml_norm_w, rw_mu_prev, rw_mu_next, rw_w0, rw_w_up, rw_a0, rw_a_up, rw_g_up, rw_k_k, rw_k_a, rw_r_k, rw_gn_w, rw_gn_b, gl_gate_up, gl_gate_b, gl_norm_w, gd_conv_w, gd_a_log, gd_dt_bias, gd_norm_w, mlp_w1, mlp_w2, final_norm_w):
    bsz, seq_x, d_model = x.shape
    n_ctx = ctx.shape[1]
    depth = w_in.shape[0]
    seq = n_ctx + seq_x
    tm = seq // 8 if seq % 64 == 0 else seq
    tok = jnp.concatenate([ctx, x], axis=1)
    cc = _pad_rows(jnp.concatenate([c, c_ctx[None, :]], axis=0), ((bsz + 1 + 7) // 8) * 8)
    mod = _ada_mod(cc, ada_w, ada_b)
    fnw = final_norm_w[None, :]
    for layer in range(depth):
        p = _layer_params(layer, w_in, ml_ig_b, ml_fg_b, ml_norm_w, rw_mu_prev, rw_mu_next, rw_w0, rw_w_up, rw_a0,
                          rw_a_up, rw_g_up, rw_k_k, rw_k_a, rw_r_k, rw_gn_w, rw_gn_b, gl_gate_up, gl_gate_b,
                          gl_norm_w, gd_conv_w, gd_a_log, gd_dt_bias, gd_norm_w)
        modx = jnp.pad(mod[layer, :bsz].reshape(bsz, 6, d_model), ((0, 0), (0, 2), (0, 0)))
        modc = _pad_rows(mod[layer, bsz].reshape(6, d_model))
        z_ml, z_rw, z_gl, z_gd, z_misc = _in_proj(tok, modx, modc, norm1_w[layer][None, :], p['w_in'], Z_WIDTHS,
                                                  n_ctx, tm)
        h_ml, h_rw, h_gl, h_gd, rw_post = _mix_layer(z_ml, z_rw, z_gl, z_gd, z_misc, p, n_ctx)
        tok = _out_mlp(tok, h_ml, h_rw, h_gl, h_gd, z_ml, rw_post, z_gl, z_gd, modx, modc, p['normp'],
                       norm2_w[layer][None, :], fnw, _bf(w_out[layer]), _bf(mlp_w1[layer]), _bf(mlp_w2[layer]),
                       n_ctx, tm, layer == depth - 1)
    return tok[:, n_ctx:]
```

```python
import functools

import jax
import jax.numpy as jnp
from jax import lax
from jax.experimental import pallas as pl
from jax.experimental.pallas import tpu as pltpu

F32 = jnp.float32
BF16 = jnp.bfloat16

CHUNK = 64
HEADS = 4
HEAD_V = 64
HEAD_QK = 32
GROUP_W = HEADS * HEAD_V
QK_W = HEADS * HEAD_QK
N_DIR = 2
EPS = 1e-6
GN_EPS = 64e-5
M_INIT = -1e30
RW_DECAY_SCALE = 0.6065306597126334
GLA_LOGIT_NORM = 16.0
MISC_W = 128
MISC_ML_IG, MISC_ML_FG, MISC_GL_AL, MISC_GD_BL, MISC_GD_AL = 0, 8, 16, 32, 40
VMEM_LIMIT = 56 * 1024 * 1024


def _mm(a, b, prec=None):
    return lax.dot_general(a, b, (((1,), (0,)), ((), ())), precision=prec, preferred_element_type=F32)


def _mm_nt(a, b, prec=None):
    return lax.dot_general(a, b, (((1,), (1,)), ((), ())), precision=prec, preferred_element_type=F32)


def _mm_tn(a, b, prec=None):
    return lax.dot_general(a, b, (((0,), (0,)), ((), ())), precision=prec, preferred_element_type=F32)


def _bf(x):
    return x.astype(BF16)


def _iota(shape, dim):
    return lax.broadcasted_iota(jnp.int32, shape, dim)


def _sigmoid(x):
    return 1.0 / (1.0 + jnp.exp(-x))


def _softplus(x):
    return jnp.maximum(x, 0.0) + jnp.log1p(jnp.exp(-jnp.abs(x)))


def _log_sigmoid(x):
    return -_softplus(-x)


def _silu(x):
    return x * _sigmoid(x)


def _time_mask(d, width, strict):
    i = _iota((CHUNK, width), 0)
    j = _iota((CHUNK, width), 1) & (CHUNK - 1)
    if d == 0:
        return (j < i) if strict else (j <= i)
    return (j > i) if strict else (j >= i)


def _cumsum_mat(d):
    return jnp.where(_time_mask(d, CHUNK, False), 1.0, 0.0).astype(F32)


def _eye_cat(width):
    i = _iota((CHUNK, width), 0)
    j = _iota((CHUNK, width), 1) & (CHUNK - 1)
    return jnp.where(i == j, 1.0, 0.0).astype(F32)


def _bd_mask(rows, cols, row_shift, col_shift, col_and=None):
    r = _iota((rows, cols), 0) >> row_shift
    c = _iota((rows, cols), 1)
    if col_and is not None:
        c = c & col_and
    return r == (c >> col_shift)


def _stack_bd(x, col_shift):
    w = x.shape[1]
    m = _bd_mask(HEADS * CHUNK, w, 6, col_shift)
    return _bf(jnp.where(m, jnp.concatenate([x.astype(F32)] * HEADS, axis=0), 0.0))


def _split(x, n):
    out, r = [], x
    for i in range(n):
        p = r.astype(BF16)
        out.append(p)
        if i + 1 < n:
            r = r - p.astype(F32)
    return out


def _sel_left(t, x, n=2):
    return _mm(jnp.concatenate([_bf(t)] * n, axis=1), jnp.concatenate(_split(x, n), axis=0))


def _sel_right(x, e, n=2):
    return _mm(jnp.concatenate(_split(x, n), axis=1), jnp.concatenate([_bf(e)] * n, axis=0))


def _mm3(a, b):
    ah, al = _split(a, 2)
    bh, bl = _split(b, 2)
    return _mm(jnp.concatenate([ah, ah, al], axis=1), jnp.concatenate([bh, bl, bh], axis=0))


def _mm3_bd(a, bs):
    ah, al = _split(a, 2)
    bh = jnp.concatenate([_stack_bd(b, 6) for b in bs], axis=1)
    bl = jnp.concatenate([_stack_bd(b - _bf(b).astype(F32), 6) for b in bs], axis=1)
    return _mm(jnp.concatenate([ah, ah, al], axis=1), jnp.concatenate([bh, bl, bh], axis=0))


def _mm1_bd(a, bs):
    return _mm(_bf(a), jnp.concatenate([_stack_bd(b, 6) for b in bs], axis=1))


INV_HI_STEPS = 4


def _cumsum_t(d, x):
    return _sel_left(_cumsum_mat(d), x)


def _expand_mat(col0s):
    r = _iota((MISC_W, GROUP_W), 0)
    hcol = _iota((MISC_W, GROUP_W), 1) >> 6
    return jnp.concatenate([jnp.where(r == hcol + c0, 1.0, 0.0) for c0 in col0s], axis=1).astype(F32)


def _to_cols(x):
    return _sel_left(jnp.ones((CHUNK, CHUNK), F32), x * _eye_cat(GROUP_W))


def _block_sum(x):
    w = x.shape[1]
    ones_bd = jnp.where(_bd_mask(w, w, 6, 6), 1.0, 0.0).astype(F32)
    return _sel_right(x, ones_bd, 2)


def _block_max(x):
    blk = _iota(x.shape, 1) >> 6
    out = jnp.zeros_like(x)
    for h in range(HEADS):
        sel = blk == h
        mh = jnp.max(jnp.where(sel, x, -jnp.inf), axis=1, keepdims=True)
        out = jnp.where(sel, mh, out)
    return out


def _inv_unit(n):
    r = _eye_cat(GROUP_W) + n
    p = n
    for it in range(5):
        mm = _mm3_bd if it < INV_HI_STEPS else _mm1_bd
        if it == 0:
            p = mm(p, [p])
            yield
        else:
            pr = mm(jnp.concatenate([p, r], axis=0), [p])
            yield
            p = pr[:CHUNK]
            r = r + pr[CHUNK:]
    out = _mm1_bd(r, [p])
    yield
    return r + out


def _l2n_heads(t):
    return t * lax.rsqrt(_block_sum(t * t) + EPS)


def _ml_chunk(d, xs, cs, st):
    z, misc = xs
    (gb,) = cs
    s_mat, m_row = st
    last = CHUNK - 1 if d == 0 else 0
    q = z[:, 0:QK_W] * (HEAD_QK ** -0.5)
    k = z[:, QK_W:2 * QK_W]
    v = z[:, 2 * QK_W:2 * QK_W + GROUP_W]
    t = misc + gb[0:1, :]
    is_fg = (_iota(misc.shape, 1) >> 3) == (MISC_ML_FG >> 3)
    y = jnp.where(is_fg, _log_sigmoid(t), t)
    cum = _cumsum_t(d, y)
    a = _mm_nt(_bf(q), _stack_bd(k, 5))
    inter = _mm(_bf(q), _bf(s_mat))
    yield
    ex = _sel_right(jnp.where(is_fg, cum, y), _expand_mat((MISC_ML_IG + HEADS * d, MISC_ML_FG + HEADS * d)))
    yield
    ig, bcum = ex[:, :GROUP_W], ex[:, GROUP_W:]
    b_end = bcum[last:last + 1, :]
    w_end = b_end - bcum + ig
    m_loc = jnp.max(w_end, axis=0, keepdims=True)
    e_end = jnp.exp(w_end - m_loc)
    cols = _to_cols(ig - bcum)
    loc = _mm_tn(_bf(k), _bf(jnp.concatenate([v * e_end, e_end], axis=1)))
    yield
    loc = jnp.where(_bd_mask(QK_W, 2 * GROUP_W, 5, 6, GROUP_W - 1), loc, 0.0)
    m_prev_s = b_end + m_row
    m_new = jnp.maximum(m_prev_s, m_loc)
    f_s = jnp.exp(m_prev_s - m_new)
    f_l = jnp.exp(m_loc - m_new)
    s_new = jnp.concatenate([f_s, f_s], axis=1) * s_mat + jnp.concatenate([f_l, f_l], axis=1) * loc
    logd = jnp.where(_time_mask(d, GROUP_W, False), bcum + cols, -jnp.inf)
    m_prev = bcum + m_row
    m_i = jnp.maximum(m_prev, _block_max(logd))
    sm = a * jnp.exp(logd - m_i)
    e_prev = jnp.exp(m_prev - m_i)
    ones_bd = jnp.where(_bd_mask(GROUP_W, GROUP_W, 6, 6), 1.0, 0.0).astype(BF16)
    intra = _mm(_bf(sm), jnp.concatenate([_stack_bd(v, 6), ones_bd], axis=1))
    yield
    num = intra[:, :GROUP_W] + e_prev * inter[:, :GROUP_W]
    den = intra[:, GROUP_W:] + e_prev * inter[:, GROUP_W:]
    h = num / jnp.maximum(jnp.abs(den), jnp.exp(-m_i))
    return h, (s_new, m_new)


def _gl_chunk(d, xs, cs, st):
    z, misc = xs
    gup, gb = cs
    (s_mat,) = st
    last = CHUNK - 1 if d == 0 else 0
    mid = CHUNK // 2 if d == 0 else CHUNK - 1 - CHUNK // 2
    q = z[:, 0:QK_W] * (HEAD_QK ** -0.5)
    k = z[:, QK_W:2 * QK_W]
    v = z[:, 2 * QK_W:2 * QK_W + GROUP_W]
    pre = _mm3(misc, gup[d])
    yield
    la = _log_sigmoid(pre + gb[d:d + 1, :]) * (1.0 / GLA_LOGIT_NORM)
    g = _cumsum_t(d, la)
    yield
    g_end = g[last:last + 1, :]
    g_mid = g[mid:mid + 1, :]
    qd = q * jnp.exp(g - g_mid)
    kd = k * jnp.exp(g_mid - g)
    att = _mm_nt(_bf(qd), _stack_bd(kd, 5))
    s_loc = _mm_tn(_bf(v), _bf(k * jnp.exp(g_end - g)))
    o_inter = _mm_nt(_bf(q * jnp.exp(g)), _bf(s_mat))
    yield
    att = jnp.where(_time_mask(d, GROUP_W, False), att, 0.0)
    s_new = jnp.exp(g_end) * s_mat + jnp.where(_bd_mask(GROUP_W, QK_W, 6, 5), s_loc, 0.0)
    o_intra = _mm(_bf(att), _stack_bd(v, 6))
    yield
    return o_inter + o_intra, (s_new,)


def _gd_chunk(d, xs, cs, st):
    qkv, misc = xs
    (gp,) = cs
    (s_mat,) = st
    last = CHUNK - 1 if d == 0 else 0
    q = qkv[:, 0:GROUP_W] * (HEAD_V ** -0.5)
    k = qkv[:, GROUP_W:2 * GROUP_W]
    v = qkv[:, 2 * GROUP_W:3 * GROUP_W]
    t = misc + gp[0:1, :]
    is_al = (_iota(misc.shape, 1) >> 3) == (MISC_GD_AL >> 3)
    y = jnp.where(is_al, -jnp.exp(gp[1:2, :]) * _softplus(t), _sigmoid(t))
    cum = _cumsum_t(d, y)
    x2 = _mm_nt(_bf(jnp.concatenate([k, q], axis=0)), _stack_bd(k, 6))
    yield
    ex = _sel_right(jnp.where(is_al, cum, y), _expand_mat((MISC_GD_BL + HEADS * d, MISC_GD_AL + HEADS * d)))
    yield
    beta, g = ex[:, :GROUP_W], ex[:, GROUP_W:]
    g_end = g[last:last + 1, :]
    cols = _to_cols(g)
    yield
    dec0 = jnp.where(_time_mask(d, GROUP_W, False), jnp.exp(g - cols), 0.0)
    kk, qk = x2[:CHUNK], x2[CHUNK:]
    mm = jnp.where(_time_mask(d, GROUP_W, True), beta * kk * dec0, 0.0)
    ainv = yield from _inv_unit(-mm)
    eg = jnp.exp(g)
    sol = _mm1_bd(ainv, [v * beta, k * (beta * eg)])
    o_inter = _mm(_bf(q * eg), _bf(s_mat))
    yield
    u, w = sol[:, :GROUP_W], sol[:, GROUP_W:]
    ws = _mm(_bf(w), _bf(s_mat))
    yield
    v_new = u - ws
    k_end = k * jnp.exp(g_end - g)
    upd = _mm_tn(_bf(k_end), _bf(v_new))
    o_intra = _mm(_bf(qk * dec0), _stack_bd(v_new, 6))
    yield
    s_new = jnp.exp(g_end) * s_mat + jnp.where(_bd_mask(GROUP_W, GROUP_W, 6, 6), upd, 0.0)
    return o_inter + o_intra, (s_new,)


def _rw_chunk(d, xs, cs, st):
    base, dirp = xs
    (rp,) = cs
    (s_mat,) = st
    last = CHUNK - 1 if d == 0 else 0
    r = base[:, 0:GROUP_W]
    k = base[:, GROUP_W:2 * GROUP_W]
    v = base[:, 2 * GROUP_W:3 * GROUP_W]
    kk = base[:, 3 * GROUP_W:4 * GROUP_W]
    lw = dirp[:, 0:GROUP_W]
    a = dirp[:, GROUP_W:2 * GROUP_W]
    k_a = rp[5:6, :]
    kt = k * (1.0 + (a - 1.0) * k_a)
    ap = kk * a
    g = _cumsum_t(d, lw)
    yield
    g_end = g[last:last + 1, :]
    eg = jnp.exp(g)
    eng = jnp.exp(-g)
    r_h = r * eg
    a_h = ap * eng
    k_h = kt * eng
    b_h = -kk * jnp.exp(g - lw)
    x = _mm_nt(_bf(jnp.concatenate([b_h, r_h], axis=0)),
               jnp.concatenate([_stack_bd(a_h, 6), _stack_bd(k_h, 6)], axis=0))
    y_inter = _mm_nt(_bf(r_h), _bf(s_mat))
    yield
    tm1 = _time_mask(d, GROUP_W, True)
    tm0 = _time_mask(d, GROUP_W, False)
    m_ab = jnp.where(tm1, x[:CHUNK, :GROUP_W], 0.0)
    m_bk = jnp.where(tm1, x[:CHUNK, GROUP_W:], 0.0)
    att_a = jnp.where(tm0, x[CHUNK:, :GROUP_W], 0.0)
    att_k = jnp.where(tm0, x[CHUNK:, GROUP_W:], 0.0)
    vbd = _stack_bd(v, 6)
    rhs2 = _mm(_bf(m_bk), vbd)
    ainv = yield from _inv_unit(m_ab)
    sol = _mm1_bd(ainv, [b_h, rhs2])
    yield
    w_mat, u0 = sol[:, :GROUP_W], sol[:, GROUP_W:]
    ws = _mm_nt(_bf(w_mat), _bf(s_mat))
    yield
    u = u0 + ws
    dec_end = jnp.exp(g_end - g)
    upd = _mm_tn(_bf(jnp.concatenate([u, v], axis=0)), _bf(jnp.concatenate([ap * dec_end, kt * dec_end], axis=0)))
    y_intra = _mm(_bf(jnp.concatenate([att_a, att_k], axis=1)), jnp.concatenate([_stack_bd(u, 6), vbd], axis=0))
    yield
    s_new = jnp.exp(g_end) * s_mat + jnp.where(_bd_mask(GROUP_W, GROUP_W, 6, 6), upd, 0.0)
    return y_inter + y_intra, (s_new,)


def _scan_kernel(chunk_fn, nb, n_in, n_cst, state_init, n_ctx_chunks, n_chunks, *refs):
    in_refs = (refs[:n_in], refs[n_in:2 * n_in])
    cst_refs = refs[2 * n_in:2 * n_in + n_cst]
    o_ref = refs[2 * n_in + n_cst]
    n_st = len(state_init)
    st_refs = refs[2 * n_in + n_cst + 1:]
    s = pl.program_id(1)

    @pl.when(s == 0)
    def _():
        o_ref[...] = jnp.zeros(o_ref.shape, o_ref.dtype)
        for i, (shape, val) in enumerate(state_init):
            st_refs[i][...] = jnp.full((N_DIR * nb,) + shape, val, F32)

    cs = tuple(c[...] for c in cst_refs)
    chunk_b = jnp.where(s < n_ctx_chunks, n_ctx_chunks - 1 - s, n_chunks - 1 + n_ctx_chunks - s)
    chains = [(bi, d) for bi in range(nb) for d in range(N_DIR)]
    gens = []
    for bi, d in chains:
        xs = tuple(r[bi] for r in in_refs[d])
        st = tuple(st_refs[i][d * nb + bi] for i in range(n_st))
        gens.append(chunk_fn(d, xs, cs, st))
    results = [None] * len(chains)
    while any(r is None for r in results):
        for ci, gen in enumerate(gens):
            if results[ci] is None:
                try:
                    next(gen)
                except StopIteration as stop:
                    results[ci] = stop.value
    for (bi, d), (y, st_new) in zip(chains, results):
        for i in range(n_st):
            st_refs[i][d * nb + bi] = st_new[i]
        row = pl.multiple_of((s if d == 0 else chunk_b) * CHUNK, CHUNK)
        o_ref[bi, pl.ds(row, CHUNK), :] += y


def _bidir_scan(chunk_fn, ins_f, ins_b, csts, state_init, n_ctx, nb):
    bsz, seq = ins_f[0].shape[0], ins_f[0].shape[1]
    n_chunks, n_ctx_chunks = seq // CHUNK, n_ctx // CHUNK

    def idx_f(b, s):
        return (b, s, 0)

    def idx_b(b, s):
        return (b, jnp.where(s < n_ctx_chunks, n_ctx_chunks - 1 - s, n_chunks - 1 + n_ctx_chunks - s), 0)

    in_specs = [pl.BlockSpec((nb, CHUNK, a.shape[2]), idx_f) for a in ins_f]
    in_specs += [pl.BlockSpec((nb, CHUNK, a.shape[2]), idx_b) for a in ins_b]
    in_specs += [pl.BlockSpec(c.shape, lambda b, s, nd=c.ndim: (0,) * nd) for c in csts]
    scratch = [pltpu.VMEM((N_DIR * nb,) + shape, F32) for (shape, _) in state_init]
    kern = functools.partial(_scan_kernel, chunk_fn, nb, len(ins_f), len(csts), state_init, n_ctx_chunks, n_chunks)
    return pl.pallas_call(
        kern,
        name=chunk_fn.__name__.strip("_") + "_scan",
        grid=(bsz // nb, n_chunks),
        in_specs=in_specs,
        out_specs=pl.BlockSpec((nb, seq, GROUP_W), lambda b, s: (b, 0, 0)),
        out_shape=jax.ShapeDtypeStruct((bsz, seq, GROUP_W), F32),
        scratch_shapes=scratch,
        compiler_params=pltpu.CompilerParams(dimension_semantics=("arbitrary", "arbitrary"),
                                             vmem_limit_bytes=VMEM_LIMIT),
    )(*ins_f, *ins_b, *csts)


def _ada_kernel(c_ref, w_ref, b_ref, o_ref):
    o_ref[0] = _mm(_bf(_silu(c_ref[...])), _bf(w_ref[0])) + b_ref[0]


def _ada_mod(cc, ada_w, ada_b):
    depth, d_model, n6 = ada_w.shape
    tn = 1536
    return pl.pallas_call(
        _ada_kernel,
        name="ada_mod",
        grid=(depth, n6 // tn),
        in_specs=[pl.BlockSpec(cc.shape, lambda l, j: (0, 0)),
                  pl.BlockSpec((1, d_model, tn), lambda l, j: (l, 0, j)),
                  pl.BlockSpec((1, 1, tn), lambda l, j: (l, 0, j))],
        out_specs=pl.BlockSpec((1, cc.shape[0], tn), lambda l, j: (l, 0, j)),
        out_shape=jax.ShapeDtypeStruct((depth, cc.shape[0], n6), F32),
        compiler_params=pltpu.CompilerParams(dimension_semantics=("arbitrary", "arbitrary"),
                                             vmem_limit_bytes=VMEM_LIMIT),
    )(cc, ada_w, ada_b.reshape(depth, 1, n6))


def _rms_rows(x, gain):
    return x * lax.rsqrt(jnp.mean(x * x, axis=-1, keepdims=True) + EPS) * gain


def _pick_mod(modx_ref, modc_ref, is_ctx, row):
    return jnp.where(is_ctx, modc_ref[row:row + 1, :], modx_ref[0, row:row + 1, :])


def _in_proj_kernel(n_ctx, tm, widths, tok_ref, modx_ref, modc_ref, nw_ref, w_ref, *out_refs):
    t = pl.program_id(1)
    x = tok_ref[0]
    is_ctx = (t * tm + _iota((tm, 1), 0)) < n_ctx
    shift = _pick_mod(modx_ref, modc_ref, is_ctx, 0)
    scale = _pick_mod(modx_ref, modc_ref, is_ctx, 1)
    h = _bf(_rms_rows(x, nw_ref[...]) * (1.0 + scale) + shift)
    c0 = 0
    for o_ref, w in zip(out_refs, widths):
        o_ref[0] = _mm(h, w_ref[:, c0:c0 + w])
        c0 += w


def _in_proj(tok, modx, modc, nw, w_p, widths, n_ctx, tm):
    bsz, seq, d_model = tok.shape
    kern = functools.partial(_in_proj_kernel, n_ctx, tm, widths)
    return pl.pallas_call(
        kern,
        name="in_proj",
        grid=(bsz, seq // tm),
        in_specs=[pl.BlockSpec((1, tm, d_model), lambda b, t: (b, t, 0)),
                  pl.BlockSpec((1, 8, d_model), lambda b, t: (b, 0, 0)),
                  pl.BlockSpec((8, d_model), lambda b, t: (0, 0)),
                  pl.BlockSpec((1, d_model), lambda b, t: (0, 0)),
                  pl.BlockSpec(w_p.shape, lambda b, t: (0, 0), pipeline_mode=pl.Buffered(1))],
        out_specs=[pl.BlockSpec((1, tm, w), lambda b, t: (b, t, 0)) for w in widths],
        out_shape=[jax.ShapeDtypeStruct((bsz, seq, w), F32) for w in widths],
        compiler_params=pltpu.CompilerParams(dimension_semantics=("arbitrary", "arbitrary"),
                                             vmem_limit_bytes=VMEM_LIMIT),
    )(tok, modx, modc, nw, w_p)


def _row_tile(seq, cap, mult):
    best = mult
    for r in range(mult, min(seq, cap) + 1, mult):
        if seq % r == 0:
            best = r
    return best


def _rw_prep_kernel(n_ctx, seq, rows, z_ref, zp_ref, zn_ref, mu_ref, rp_ref, lora_ref, gup_ref,
                    base_ref, d0_ref, d1_ref, post_ref):
    t = pl.program_id(1)
    z = z_ref[0]
    row = _iota((rows, 1), 0)
    pos = t * rows + row
    z_prev = jnp.where(row == 0, zp_ref[0, 7:8, :], pltpu.roll(z, 1, 0))
    z_prev = jnp.where((pos != 0) & (pos != n_ctx), z_prev, 0.0)
    z_next = jnp.where(row == rows - 1, zn_ref[0, 0:1, :], pltpu.roll(z, rows - 1, 0))
    z_next = jnp.where((pos != seq - 1) & (pos != n_ctx - 1), z_next, 0.0)
    zm = z + mu_ref[0:1, :] * (z_prev - z) + mu_ref[1:2, :] * (z_next - z)
    r = zm[:, 0:GROUP_W]
    k = zm[:, GROUP_W:2 * GROUP_W]
    v = zm[:, 2 * GROUP_W:3 * GROUP_W]
    lo = zm[:, 3 * GROUP_W:3 * GROUP_W + 128]
    gl = zm[:, 3 * GROUP_W + 128:4 * GROUP_W]
    lo = jnp.where(_iota(lo.shape, 1) < 64, jnp.tanh(lo), lo)
    pre = _mm3(lo, lora_ref[...])
    rp = rp_ref[...]
    kk = _l2n_heads(k * rp[4:5, :])
    lw0 = -RW_DECAY_SCALE * _sigmoid(rp[0:1, :] + pre[:, 0:GROUP_W])
    a0 = _sigmoid(rp[2:3, :] + pre[:, GROUP_W:2 * GROUP_W])
    lw1 = -RW_DECAY_SCALE * _sigmoid(rp[1:2, :] + pre[:, 2 * GROUP_W:3 * GROUP_W])
    a1 = _sigmoid(rp[3:4, :] + pre[:, 3 * GROUP_W:4 * GROUP_W])
    k_a = rp[5:6, :]
    k_sum = k * (1.0 + (a0 - 1.0) * k_a) + k * (1.0 + (a1 - 1.0) * k_a)
    gate = _mm(_bf(_sigmoid(gl)), _bf(gup_ref[...]))
    bonus = _block_sum(r * k_sum * rp[6:7, :]) * v
    base_ref[0] = jnp.concatenate([r, k, v, kk], axis=1)
    d0_ref[0] = jnp.concatenate([lw0, a0], axis=1)
    d1_ref[0] = jnp.concatenate([lw1, a1], axis=1)
    post_ref[0] = jnp.concatenate([gate, bonus], axis=1)


def _rw_prep(z_rw, mu, rp, lora, gup, n_ctx):
    bsz, seq, w = z_rw.shape
    rows = _row_tile(seq, 640, 8)
    sub, n_sub = rows // 8, seq // 8
    kern = functools.partial(_rw_prep_kernel, n_ctx, seq, rows)
    full = lambda a: pl.BlockSpec(a.shape, lambda b, t, nd=a.ndim: (0,) * nd)
    widths = (4 * GROUP_W, 2 * GROUP_W, 2 * GROUP_W, 2 * GROUP_W)
    return pl.pallas_call(
        kern,
        name="rw_prep",
        grid=(bsz, seq // rows),
        in_specs=[pl.BlockSpec((1, rows, w), lambda b, t: (b, t, 0)),
                  pl.BlockSpec((1, 8, w), lambda b, t: (b, jnp.maximum(t * sub - 1, 0), 0)),
                  pl.BlockSpec((1, 8, w), lambda b, t: (b, jnp.minimum((t + 1) * sub, n_sub - 1), 0)),
                  full(mu), full(rp), full(lora), full(gup)],
        out_specs=[pl.BlockSpec((1, rows, ow), lambda b, t: (b, t, 0)) for ow in widths],
        out_shape=[jax.ShapeDtypeStruct((bsz, seq, ow), F32) for ow in widths],
        compiler_params=pltpu.CompilerParams(dimension_semantics=("arbitrary", "arbitrary"),
                                             vmem_limit_bytes=VMEM_LIMIT),
    )(z_rw, z_rw, z_rw, mu, rp, lora, gup)


def _gd_prep_kernel(n_ctx, seq, rows, zc_ref, zu_ref, zd_ref, cw_ref, o_ref):
    t = pl.program_id(1)
    cw = cw_ref[...]
    ext = jnp.concatenate([zu_ref[0], zc_ref[0], zd_ref[0]], axis=0)
    n_ext = rows + 2 * CHUNK
    pos_e = t * rows - CHUNK + _iota((n_ext, 1), 0)
    col = pos_e & (CHUNK - 1)
    src_ctx = pos_e < n_ctx
    has_l = jnp.where(src_ctx, pos_e, col) >= 1
    has_r = jnp.where(src_ctx, pos_e - (n_ctx - 2), col - (CHUNK - 2)) <= 0
    e_l = jnp.where(has_l, pltpu.roll(ext, 1, 0), 0.0)
    e_r = jnp.where(has_r, pltpu.roll(ext, n_ext - 1, 0), 0.0)

    def taps(di):
        lo = CHUNK * (1 + di)
        i = 3 * (1 + di)
        return (cw[i:i + 1, :] * e_l[lo:lo + rows] + cw[i + 1:i + 2, :] * ext[lo:lo + rows]
                + cw[i + 2:i + 3, :] * e_r[lo:lo + rows])

    pos_o = t * rows + _iota((rows, 1), 0)
    acc = taps(0)
    acc = acc + jnp.where(pos_o - CHUNK >= n_ctx, taps(-1), 0.0)
    acc = acc + jnp.where((pos_o >= n_ctx) & (pos_o + CHUNK < seq), taps(1), 0.0)
    y = _silu(acc)
    q = _l2n_heads(y[:, 0:GROUP_W])
    k = _l2n_heads(y[:, GROUP_W:2 * GROUP_W])
    o_ref[0] = jnp.concatenate([q, k, y[:, 2 * GROUP_W:]], axis=1)


def _gd_prep(z_gd, cw, n_ctx):
    bsz, seq, _ = z_gd.shape
    w = 3 * GROUP_W
    rows = _row_tile(seq, 1088, CHUNK)
    per, n_chunks = rows // CHUNK, seq // CHUNK
    kern = functools.partial(_gd_prep_kernel, n_ctx, seq, rows)
    return pl.pallas_call(
        kern,
        name="gd_prep",
        grid=(bsz, seq // rows),
        in_specs=[pl.BlockSpec((1, rows, w), lambda b, t: (b, t, 0)),
                  pl.BlockSpec((1, CHUNK, w), lambda b, t: (b, jnp.maximum(t * per - 1, 0), 0)),
                  pl.BlockSpec((1, CHUNK, w), lambda b, t: (b, jnp.minimum((t + 1) * per, n_chunks - 1), 0)),
                  pl.BlockSpec(cw.shape, lambda b, t: (0, 0))],
        out_specs=pl.BlockSpec((1, rows, w), lambda b, t: (b, t, 0)),
        out_shape=jax.ShapeDtypeStruct((bsz, seq, w), F32),
        compiler_params=pltpu.CompilerParams(dimension_semantics=("arbitrary", "arbitrary"),
                                             vmem_limit_bytes=VMEM_LIMIT),
    )(z_gd, z_gd, z_gd, cw)


def _head_rms(h, gain):
    return h * lax.rsqrt(_block_sum(h * h) * (1.0 / HEAD_V) + EPS) * gain


def _head_groupnorm(h, gain, bias):
    dlt = h - _block_sum(h) * (1.0 / HEAD_V)
    return dlt * lax.rsqrt(_block_sum(dlt * dlt) * (1.0 / HEAD_V) + GN_EPS) * gain + bias


def _out_mlp_kernel(n_ctx, tm, ff_chunk, final, tok_ref, hml_ref, hrw_ref, hgl_ref, hgd_ref, oml_ref, prw_ref,
                    ogl_ref, zgd_ref, modx_ref, modc_ref, np_ref, n2_ref, fn_ref, wo_ref, w1_ref, w2_ref, o_ref):
    t = pl.program_id(1)
    x = tok_ref[0]
    is_ctx = (t * tm + _iota((tm, 1), 0)) < n_ctx
    np_ = np_ref[...]
    prw = prw_ref[0]
    mixed = jnp.concatenate([
        _head_rms(hml_ref[0], np_[0:1, :]) * _sigmoid(oml_ref[0]),
        (_head_groupnorm(hrw_ref[0], np_[1:2, :], np_[2:3, :]) + prw[:, GROUP_W:]) * prw[:, :GROUP_W],
        _head_rms(hgl_ref[0], np_[3:4, :]) * _silu(ogl_ref[0]),
        _head_rms(hgd_ref[0], np_[4:5, :]) * _silu(zgd_ref[0]),
    ], axis=1)
    x = x + _pick_mod(modx_ref, modc_ref, is_ctx, 2) * _mm(_bf(mixed), wo_ref[...])
    shift = _pick_mod(modx_ref, modc_ref, is_ctx, 3)
    scale = _pick_mod(modx_ref, modc_ref, is_ctx, 4)
    h = _bf(_rms_rows(x, n2_ref[...]) * (1.0 + scale) + shift)
    d_ff = w1_ref.shape[1]
    acc = jnp.zeros(x.shape, F32)
    for c0 in range(0, d_ff, ff_chunk):
        a = jnp.maximum(_mm(h, w1_ref[:, c0:c0 + ff_chunk]), 0.0)
        acc = acc + _mm(_bf(a * a), w2_ref[c0:c0 + ff_chunk, :])
    x = x + _pick_mod(modx_ref, modc_ref, is_ctx, 5) * acc
    if final:
        x = _rms_rows(x, fn_ref[...])
    o_ref[0] = x


def _out_mlp(tok, h_ml, h_rw, h_gl, h_gd, z_ml, rw_post, z_gl, z_gd, modx, modc, normp, n2, fnw, wo, w1, w2,
             n_ctx, tm, final):
    bsz, seq, d_model = tok.shape
    kern = functools.partial(_out_mlp_kernel, n_ctx, tm, 1024, final)
    row = lambda w, j=0: pl.BlockSpec((1, tm, w), lambda b, t, j=j: (b, t, j))
    const = lambda a: pl.BlockSpec(a.shape, lambda b, t, nd=a.ndim: (0,) * nd)
    wconst = lambda a: pl.BlockSpec(a.shape, lambda b, t: (0, 0), pipeline_mode=pl.Buffered(1))
    return pl.pallas_call(
        kern,
        name="out_mlp",
        grid=(bsz, seq // tm),
        in_specs=[row(d_model), row(GROUP_W), row(GROUP_W), row(GROUP_W), row(GROUP_W),
                  row(GROUP_W, 2), row(2 * GROUP_W), row(GROUP_W, 2), row(GROUP_W, 3),
                  pl.BlockSpec((1, 8, d_model), lambda b, t: (b, 0, 0)), const(modc), const(normp), const(n2),
                  const(fnw), wconst(wo), wconst(w1), wconst(w2)],
        out_specs=row(d_model),
        out_shape=jax.ShapeDtypeStruct((bsz, seq, d_model), F32),
        compiler_params=pltpu.CompilerParams(dimension_semantics=("arbitrary", "arbitrary"),
                                             vmem_limit_bytes=VMEM_LIMIT),
    )(tok, h_ml, h_rw, h_gl, h_gd, z_ml, rw_post, z_gl, z_gd, modx, modc, normp, n2, fnw, wo, w1, w2)


def _pad_rows(a, rows=8):
    return jnp.concatenate([a, jnp.zeros((rows - a.shape[0],) + a.shape[1:], a.dtype)], axis=0)


def _misc_row(pairs):
    row = jnp.zeros((MISC_W,), F32)
    for col0, p in pairs:
        row = row.at[col0:col0 + N_DIR * HEADS].set(p.reshape(-1))
    return row


def _permute_w_in(w):
    d_model = w.shape[0]
    ml_w, rw_w, gl_w = 784, 1024, 784
    o_ml, o_rw, o_gl, o_gd = 0, ml_w, ml_w + rw_w, ml_w + rw_w + gl_w
    misc = jnp.concatenate([
        w[:, o_ml + 768:o_ml + 784],
        w[:, o_gl + 512:o_gl + 528],
        w[:, o_gd + 1024:o_gd + 1040],
        jnp.zeros((d_model, MISC_W - 48), w.dtype)], axis=1)
    return jnp.concatenate([
        w[:, o_ml:o_ml + 768],
        w[:, o_rw:o_rw + 1024],
        w[:, o_gl:o_gl + 512], w[:, o_gl + 528:o_gl + 784],
        w[:, o_gd:o_gd + 1024],
        misc], axis=1)


Z_WIDTHS = (768, 1024, 768, 1024, MISC_W)


SCAN_NB = 4


def _mix_layer(z_ml, z_rw, z_gl, z_gd, z_misc, p, n_ctx):
    nb = SCAN_NB if z_ml.shape[0] % SCAN_NB == 0 else 1
    s_ml = (((QK_W, 2 * GROUP_W), 0.0), ((1, GROUP_W), M_INIT))
    h_ml = _bidir_scan(_ml_chunk, [z_ml, z_misc], [z_ml, z_misc], [p['ml_gb']], s_ml, n_ctx, nb)
    h_gl = _bidir_scan(_gl_chunk, [z_gl, z_misc], [z_gl, z_misc], [p['gl_gup'], p['gl_gb']],
                       (((GROUP_W, QK_W), 0.0),), n_ctx, nb)
    qkv = _gd_prep(z_gd, p['gd_cw'], n_ctx)
    h_gd = _bidir_scan(_gd_chunk, [qkv, z_misc], [qkv, z_misc], [p['gd_gp']], (((GROUP_W, GROUP_W), 0.0),),
                       n_ctx, nb)
    base, d0, d1, rw_post = _rw_prep(z_rw, p['rw_mu'], p['rw_rp'], p['rw_lora'], p['rw_gup'], n_ctx)
    h_rw = _bidir_scan(_rw_chunk, [base, d0], [base, d1], [p['rw_rp']], (((GROUP_W, GROUP_W), 0.0),), n_ctx, nb)
    return h_ml, h_rw, h_gl, h_gd, rw_post


def _layer_params(layer, w_in, ml_ig_b, ml_fg_b, ml_norm_w, rw_mu_prev, rw_mu_next, rw_w0, rw_w_up, rw_a0,
                  rw_a_up, rw_g_up, rw_k_k, rw_k_a, rw_r_k, rw_gn_w, rw_gn_b, gl_gate_up, gl_gate_b, gl_norm_w,
                  gd_conv_w, gd_a_log, gd_dt_bias, gd_norm_w):
    l = layer
    z64 = jnp.zeros((64, GROUP_W), F32)
    lora = jnp.concatenate([
        jnp.concatenate([rw_w_up[l, 0], z64, rw_w_up[l, 1], z64], axis=1),
        jnp.concatenate([z64, rw_a_up[l, 0], z64, rw_a_up[l, 1]], axis=1)], axis=0)
    gup = jnp.zeros((N_DIR, MISC_W, QK_W), F32).at[:, MISC_GL_AL:MISC_GL_AL + 16, :].set(gl_gate_up[l])
    return {
        'w_in': _bf(_permute_w_in(w_in[l])),
        'ml_gb': _pad_rows(_misc_row([(MISC_ML_IG, ml_ig_b[l]), (MISC_ML_FG, ml_fg_b[l])])[None, :]),
        'gl_gup': gup,
        'gl_gb': _pad_rows(gl_gate_b[l]),
        'gd_cw': _pad_rows(gd_conv_w[l].reshape(9, 3 * GROUP_W), 16),
        'gd_gp': _pad_rows(jnp.stack([_misc_row([(MISC_GD_AL, gd_dt_bias[l])]), _misc_row([(MISC_GD_AL, gd_a_log[l])])])),
        'rw_mu': _pad_rows(jnp.stack([rw_mu_prev[l], rw_mu_next[l]])),
        'rw_rp': _pad_rows(jnp.stack([rw_w0[l, 0], rw_w0[l, 1], rw_a0[l, 0], rw_a0[l, 1],
                                      rw_k_k[l], rw_k_a[l], rw_r_k[l]])),
        'rw_lora': lora,
        'rw_gup': rw_g_up[l],
        'normp': _pad_rows(jnp.stack([ml_norm_w[l], rw_gn_w[l], rw_gn_b[l], gl_norm_w[l], gd_norm_w[l]])),
    }


def kernel(x, c, ctx, c_ctx, ada_w, ada_b, norm1_w, norm2_w, w_in, w_out, ml_ig_b, ml_fg_b, ml_norm_w, rw_mu_prev, rw_mu_next, rw_w0, rw_w_up, rw_a0, rw_a_up, rw_g_up, rw_k_k, rw_k_a, rw_r_k, rw_gn_w, rw_gn_b, gl_gate_up, gl_gate_b, gl_norm_w, gd_conv_w, gd_a_log, gd_dt_bias, gd_norm_w, mlp_w1, mlp_w2, final_norm_w):
    bsz, seq_x, d_model = x.shape
    n_ctx = ctx.shape[1]
    depth = w_in.shape[0]
    seq = n_ctx + seq_x
    tm = seq // 8 if seq % 64 == 0 else seq
    tok = jnp.concatenate([ctx, x], axis=1)
    cc = _pad_rows(jnp.concatenate([c, c_ctx[None, :]], axis=0), ((bsz + 1 + 7) // 8) * 8)
    mod = _ada_mod(cc, ada_w, ada_b)
    fnw = final_norm_w[None, :]
    for layer in range(depth):
        p = _layer_params(layer, w_in, ml_ig_b, ml_fg_b, ml_norm_w, rw_mu_prev, rw_mu_next, rw_w0, rw_w_up, rw_a0,
                          rw_a_up, rw_g_up, rw_k_k, rw_k_a, rw_r_k, rw_gn_w, rw_gn_b, gl_gate_up, gl_gate_b,
                          gl_norm_w, gd_conv_w, gd_a_log, gd_dt_bias, gd_norm_w)
        modx = jnp.pad(mod[layer, :bsz].reshape(bsz, 6, d_model), ((0, 0), (0, 2), (0, 0)))
        modc = _pad_rows(mod[layer, bsz].reshape(6, d_model))
        z_ml, z_rw, z_gl, z_gd, z_misc = _in_proj(tok, modx, modc, norm1_w[layer][None, :], p['w_in'], Z_WIDTHS,
                                                  n_ctx, tm)
        h_ml, h_rw, h_gl, h_gd, rw_post = _mix_layer(z_ml, z_rw, z_gl, z_gd, z_misc, p, n_ctx)
        tok = _out_mlp(tok, h_ml, h_rw, h_gl, h_gd, z_ml, rw_post, z_gl, z_gd, modx, modc, p['normp'],
                       norm2_w[layer][None, :], fnw, _bf(w_out[layer]), _bf(mlp_w1[layer]), _bf(mlp_w2[layer]),
                       n_ctx, tm, layer == depth - 1)
    return tok[:, n_ctx:]
```

```python
import functools

import jax
import jax.numpy as jnp
from jax import lax
from jax.experimental import pallas as pl
from jax.experimental.pallas import tpu as pltpu

F32 = jnp.float32
BF16 = jnp.bfloat16

CHUNK = 64
HEADS = 4
HEAD_V = 64
HEAD_QK = 32
GROUP_W = HEADS * HEAD_V
QK_W = HEADS * HEAD_QK
N_DIR = 2
EPS = 1e-6
GN_EPS = 64e-5
M_INIT = -1e30
RW_DECAY_SCALE = 0.6065306597126334
GLA_LOGIT_NORM = 16.0
MISC_W = 128
MISC_ML_IG, MISC_ML_FG, MISC_GL_AL, MISC_GD_BL, MISC_GD_AL = 0, 8, 16, 32, 40
VMEM_LIMIT = 56 * 1024 * 1024


def _mm(a, b, prec=None):
    return lax.dot_general(a, b, (((1,), (0,)), ((), ())), precision=prec, preferred_element_type=F32)


def _mm_nt(a, b, prec=None):
    return lax.dot_general(a, b, (((1,), (1,)), ((), ())), precision=prec, preferred_element_type=F32)


def _mm_tn(a, b, prec=None):
    return lax.dot_general(a, b, (((0,), (0,)), ((), ())), precision=prec, preferred_element_type=F32)


def _bf(x):
    return x.astype(BF16)


def _iota(shape, dim):
    return lax.broadcasted_iota(jnp.int32, shape, dim)


def _sigmoid(x):
    return 1.0 / (1.0 + jnp.exp(-x))


def _softplus(x):
    return jnp.maximum(x, 0.0) + jnp.log1p(jnp.exp(-jnp.abs(x)))


def _log_sigmoid(x):
    return -_softplus(-x)


def _silu(x):
    return x * _sigmoid(x)


def _time_mask(d, width, strict):
    i = _iota((CHUNK, width), 0)
    j = _iota((CHUNK, width), 1) & (CHUNK - 1)
    if d == 0:
        return (j < i) if strict else (j <= i)
    return (j > i) if strict else (j >= i)


def _cumsum_mat(d):
    return jnp.where(_time_mask(d, CHUNK, False), 1.0, 0.0).astype(F32)


def _eye_cat(width):
    i = _iota((CHUNK, width), 0)
    j = _iota((CHUNK, width), 1) & (CHUNK - 1)
    return jnp.where(i == j, 1.0, 0.0).astype(F32)


def _bd_mask(rows, cols, row_shift, col_shift, col_and=None):
    r = _iota((rows, cols), 0) >> row_shift
    c = _iota((rows, cols), 1)
    if col_and is not None:
        c = c & col_and
    return r == (c >> col_shift)


def _stack_bd(x, col_shift):
    w = x.shape[1]
    m = _bd_mask(HEADS * CHUNK, w, 6, col_shift)
    return _bf(jnp.where(m, jnp.concatenate([x.astype(F32)] * HEADS, axis=0), 0.0))


def _split(x, n):
    out, r = [], x
    for i in range(n):
        p = r.astype(BF16)
        out.append(p)
        if i + 1 < n:
            r = r - p.astype(F32)
    return out


def _sel_left(t, x, n=2):
    return _mm(jnp.concatenate([_bf(t)] * n, axis=1), jnp.concatenate(_split(x, n), axis=0))


def _sel_right(x, e, n=2):
    return _mm(jnp.concatenate(_split(x, n), axis=1), jnp.concatenate([_bf(e)] * n, axis=0))


def _mm3(a, b):
    ah, al = _split(a, 2)
    bh, bl = _split(b, 2)
    return _mm(jnp.concatenate([ah, ah, al], axis=1), jnp.concatenate([bh, bl, bh], axis=0))


def _mm3_bd(a, bs):
    ah, al = _split(a, 2)
    bh = jnp.concatenate([_stack_bd(b, 6) for b in bs], axis=1)
    bl = jnp.concatenate([_stack_bd(b - _bf(b).astype(F32), 6) for b in bs], axis=1)
    return _mm(jnp.concatenate([ah, ah, al], axis=1), jnp.concatenate([bh, bl, bh], axis=0))


def _mm1_bd(a, bs):
    return _mm(_bf(a), jnp.concatenate([_stack_bd(b, 6) for b in bs], axis=1))


INV_HI_STEPS = 4


def _cumsum_t(d, x):
    return _sel_left(_cumsum_mat(d), x)


def _expand_mat(col0s):
    r = _iota((MISC_W, GROUP_W), 0)
    hcol = _iota((MISC_W, GROUP_W), 1) >> 6
    return jnp.concatenate([jnp.where(r == hcol + c0, 1.0, 0.0) for c0 in col0s], axis=1).astype(F32)


def _to_cols(x):
    return _sel_left(jnp.ones((CHUNK, CHUNK), F32), x * _eye_cat(GROUP_W))


def _block_sum(x):
    w = x.shape[1]
    ones_bd = jnp.where(_bd_mask(w, w, 6, 6), 1.0, 0.0).astype(F32)
    return _sel_right(x, ones_bd, 2)


def _block_max(x):
    blk = _iota(x.shape, 1) >> 6
    out = jnp.zeros_like(x)
    for h in range(HEADS):
        sel = blk == h
        mh = jnp.max(jnp.where(sel, x, -jnp.inf), axis=1, keepdims=True)
        out = jnp.where(sel, mh, out)
    return out


def _inv_unit(n):
    r = _eye_cat(GROUP_W) + n
    p = n
    for it in range(5):
        mm = _mm3_bd if it < INV_HI_STEPS else _mm1_bd
        if it == 0:
            p = mm(p, [p])
            yield
        else:
            pr = mm(jnp.concatenate([p, r], axis=0), [p])
            yield
            p = pr[:CHUNK]
            r = r + pr[CHUNK:]
    out = _mm1_bd(r, [p])
    yield
    return r + out


def _l2n_heads(t):
    return t * lax.rsqrt(_block_sum(t * t) + EPS)


def _ml_chunk(d, xs, cs, st):
    z, misc = xs
    (gb,) = cs
    s_mat, m_row = st
    last = CHUNK - 1 if d == 0 else 0
    q = z[:, 0:QK_W] * (HEAD_QK ** -0.5)
    k = z[:, QK_W:2 * QK_W]
    v = z[:, 2 * QK_W:2 * QK_W + GROUP_W]
    t = misc + gb[0:1, :]
    is_fg = (_iota(misc.shape, 1) >> 3) == (MISC_ML_FG >> 3)
    y = jnp.where(is_fg, _log_sigmoid(t), t)
    cum = _cumsum_t(d, y)
    a = _mm_nt(_bf(q), _stack_bd(k, 5))
    inter = _mm(_bf(q), _bf(s_mat))
    yield
    ex = _sel_right(jnp.where(is_fg, cum, y), _expand_mat((MISC_ML_IG + HEADS * d, MISC_ML_FG + HEADS * d)))
    yield
    ig, bcum = ex[:, :GROUP_W], ex[:, GROUP_W:]
    b_end = bcum[last:last + 1, :]
    w_end = b_end - bcum + ig
    m_loc = jnp.max(w_end, axis=0, keepdims=True)
    e_end = jnp.exp(w_end - m_loc)
    cols = _to_cols(ig - bcum)
    loc = _mm_tn(_bf(k), _bf(jnp.concatenate([v * e_end, e_end], axis=1)))
    yield
    loc = jnp.where(_bd_mask(QK_W, 2 * GROUP_W, 5, 6, GROUP_W - 1), loc, 0.0)
    m_prev_s = b_end + m_row
    m_new = jnp.maximum(m_prev_s, m_loc)
    f_s = jnp.exp(m_prev_s - m_new)
    f_l = jnp.exp(m_loc - m_new)
    s_new = jnp.concatenate([f_s, f_s], axis=1) * s_mat + jnp.concatenate([f_l, f_l], axis=1) * loc
    logd = jnp.where(_time_mask(d, GROUP_W, False), bcum + cols, -jnp.inf)
    m_prev = bcum + m_row
    m_i = jnp.maximum(m_prev, _block_max(logd))
    sm = a * jnp.exp(logd - m_i)
    e_prev = jnp.exp(m_prev - m_i)
    ones_bd = jnp.where(_bd_mask(GROUP_W, GROUP_W, 6, 6), 1.0, 0.0).astype(BF16)
    intra = _mm(_bf(sm), jnp.concatenate([_stack_bd(v, 6), ones_bd], axis=1))
    yield
    num = intra[:, :GROUP_W] + e_prev * inter[:, :GROUP_W]
    den = intra[:, GROUP_W:] + e_prev * inter[:, GROUP_W:]
    h = num / jnp.maximum(jnp.abs(den), jnp.exp(-m_i))
    return h, (s_new, m_new)


def _gl_chunk(d, xs, cs, st):
    z, misc = xs
    gup, gb = cs
    (s_mat,) = st
    last = CHUNK - 1 if d == 0 else 0
    mid = CHUNK // 2 if d == 0 else CHUNK - 1 - CHUNK // 2
    q = z[:, 0:QK_W] * (HEAD_QK ** -0.5)
    k = z[:, QK_W:2 * QK_W]
    v = z[:, 2 * QK_W:2 * QK_W + GROUP_W]
    pre = _mm3(misc, gup[d])
    yield
    la = _log_sigmoid(pre + gb[d:d + 1, :]) * (1.0 / GLA_LOGIT_NORM)
    g = _cumsum_t(d, la)
    yield
    g_end = g[last:last + 1, :]
    g_mid = g[mid:mid + 1, :]
    qd = q * jnp.exp(g - g_mid)
    kd = k * jnp.exp(g_mid - g)
    att = _mm_nt(_bf(qd), _stack_bd(kd, 5))
    s_loc = _mm_tn(_bf(v), _bf(k * jnp.exp(g_end - g)))
    o_inter = _mm_nt(_bf(q * jnp.exp(g)), _bf(s_mat))
    yield
    att = jnp.where(_time_mask(d, GROUP_W, False), att, 0.0)
    s_new = jnp.exp(g_end) * s_mat + jnp.where(_bd_mask(GROUP_W, QK_W, 6, 5), s_loc, 0.0)
    o_intra = _mm(_bf(att), _stack_bd(v, 6))
    yield
    return o_inter + o_intra, (s_new,)


def _gd_chunk(d, xs, cs, st):
    qkv, misc = xs
    (gp,) = cs
    (s_mat,) = st
    last = CHUNK - 1 if d == 0 else 0
    q = qkv[:, 0:GROUP_W] * (HEAD_V ** -0.5)
    k = qkv[:, GROUP_W:2 * GROUP_W]
    v = qkv[:, 2 * GROUP_W:3 * GROUP_W]
    t = misc + gp[0:1, :]
    is_al = (_iota(misc.shape, 1) >> 3) == (MISC_GD_AL >> 3)
    y = jnp.where(is_al, -jnp.exp(gp[1:2, :]) * _softplus(t), _sigmoid(t))
    cum = _cumsum_t(d, y)
    x2 = _mm_nt(_bf(jnp.concatenate([k, q], axis=0)), _stack_bd(k, 6))
    yield
    ex = _sel_right(jnp.where(is_al, cum, y), _expand_mat((MISC_GD_BL + HEADS * d, MISC_GD_AL + HEADS * d)))
    yield
    beta, g = ex[:, :GROUP_W], ex[:, GROUP_W:]
    g_end = g[last:last + 1, :]
    cols = _to_cols(g)
    yield
    dec0 = jnp.where(_time_mask(d, GROUP_W, False), jnp.exp(g - cols), 0.0)
    kk, qk = x2[:CHUNK], x2[CHUNK:]
    mm = jnp.where(_time_mask(d, GROUP_W, True), beta * kk * dec0, 0.0)
    ainv = yield from _inv_unit(-mm)
    eg = jnp.exp(g)
    sol = _mm1_bd(ainv, [v * beta, k * (beta * eg)])
    yield
    u, w = sol[:, :GROUP_W], sol[:, GROUP_W:]
    ws = _mm(_bf(jnp.concatenate([w, q * eg], axis=0)), _bf(s_mat))
    yield
    v_new = u - ws[:CHUNK]
    o_inter = ws[CHUNK:]
    k_end = k * jnp.exp(g_end - g)
    upd = _mm_tn(_bf(k_end), _bf(v_new))
    o_intra = _mm(_bf(qk * dec0), _stack_bd(v_new, 6))
    yield
    s_new = jnp.exp(g_end) * s_mat + jnp.where(_bd_mask(GROUP_W, GROUP_W, 6, 6), upd, 0.0)
    return o_inter + o_intra, (s_new,)


def _rw_chunk(d, xs, cs, st):
    base, dirp = xs
    (rp,) = cs
    (s_mat,) = st
    last = CHUNK - 1 if d == 0 else 0
    r = base[:, 0:GROUP_W]
    k = base[:, GROUP_W:2 * GROUP_W]
    v = base[:, 2 * GROUP_W:3 * GROUP_W]
    kk = base[:, 3 * GROUP_W:4 * GROUP_W]
    lw = dirp[:, 0:GROUP_W]
    a = dirp[:, GROUP_W:2 * GROUP_W]
    k_a = rp[5:6, :]
    kt = k * (1.0 + (a - 1.0) * k_a)
    ap = kk * a
    g = _cumsum_t(d, lw)
    yield
    g_end = g[last:last + 1, :]
    eg = jnp.exp(g)
    eng = jnp.exp(-g)
    r_h = r * eg
    a_h = ap * eng
    k_h = kt * eng
    b_h = -kk * jnp.exp(g - lw)
    x = _mm_nt(_bf(jnp.concatenate([b_h, r_h], axis=0)),
               jnp.concatenate([_stack_bd(a_h, 6), _stack_bd(k_h, 6)], axis=0))
    yield
    tm1 = _time_mask(d, GROUP_W, True)
    tm0 = _time_mask(d, GROUP_W, False)
    m_ab = jnp.where(tm1, x[:CHUNK, :GROUP_W], 0.0)
    m_bk = jnp.where(tm1, x[:CHUNK, GROUP_W:], 0.0)
    att_a = jnp.where(tm0, x[CHUNK:, :GROUP_W], 0.0)
    att_k = jnp.where(tm0, x[CHUNK:, GROUP_W:], 0.0)
    vbd = _stack_bd(v, 6)
    rhs2 = _mm(_bf(m_bk), vbd)
    ainv = yield from _inv_unit(m_ab)
    sol = _mm1_bd(ainv, [b_h, rhs2])
    yield
    w_mat, u0 = sol[:, :GROUP_W], sol[:, GROUP_W:]
    ws = _mm_nt(_bf(jnp.concatenate([w_mat, r_h], axis=0)), _bf(s_mat))
    yield
    u = u0 + ws[:CHUNK]
    y_inter = ws[CHUNK:]
    dec_end = jnp.exp(g_end - g)
    upd = _mm_tn(_bf(jnp.concatenate([u, v], axis=0)), _bf(jnp.concatenate([ap * dec_end, kt * dec_end], axis=0)))
    y_intra = _mm(_bf(jnp.concatenate([att_a, att_k], axis=1)), jnp.concatenate([_stack_bd(u, 6), vbd], axis=0))
    yield
    s_new = jnp.exp(g_end) * s_mat + jnp.where(_bd_mask(GROUP_W, GROUP_W, 6, 6), upd, 0.0)
    return y_inter + y_intra, (s_new,)


def _scan_kernel(chunk_fn, nb, n_in, n_cst, state_init, *refs):
    in_refs = (refs[:n_in], refs[n_in:2 * n_in])
    cst_refs = refs[2 * n_in:2 * n_in + n_cst]
    o_refs = refs[2 * n_in + n_cst:2 * n_in + n_cst + N_DIR]
    n_st = len(state_init)
    st_refs = refs[2 * n_in + n_cst + N_DIR:]

    @pl.when(pl.program_id(1) == 0)
    def _():
        for i, (shape, val) in enumerate(state_init):
            st_refs[i][...] = jnp.full((N_DIR * nb,) + shape, val, F32)

    cs = tuple(c[...] for c in cst_refs)
    chains = [(bi, d) for bi in range(nb) for d in range(N_DIR)]
    gens = []
    for bi, d in chains:
        xs = tuple(r[bi] for r in in_refs[d])
        st = tuple(st_refs[i][d * nb + bi] for i in range(n_st))
        gens.append(chunk_fn(d, xs, cs, st))
    results = [None] * len(chains)
    while any(r is None for r in results):
        for ci, gen in enumerate(gens):
            if results[ci] is None:
                try:
                    next(gen)
                except StopIteration as stop:
                    results[ci] = stop.value
    for (bi, d), (y, st_new) in zip(chains, results):
        for i in range(n_st):
            st_refs[i][d * nb + bi] = st_new[i]
        o_refs[d][bi] = y


def _bidir_scan(chunk_fn, ins_f, ins_b, csts, state_init, n_ctx, nb):
    bsz, seq = ins_f[0].shape[0], ins_f[0].shape[1]
    n_chunks, n_ctx_chunks = seq // CHUNK, n_ctx // CHUNK

    def idx_f(b, s):
        return (b, s, 0)

    def idx_b(b, s):
        return (b, jnp.where(s < n_ctx_chunks, n_ctx_chunks - 1 - s, n_chunks - 1 + n_ctx_chunks - s), 0)

    in_specs = [pl.BlockSpec((nb, CHUNK, a.shape[2]), idx_f) for a in ins_f]
    in_specs += [pl.BlockSpec((nb, CHUNK, a.shape[2]), idx_b) for a in ins_b]
    in_specs += [pl.BlockSpec(c.shape, lambda b, s, nd=c.ndim: (0,) * nd) for c in csts]
    scratch = [pltpu.VMEM((N_DIR * nb,) + shape, F32) for (shape, _) in state_init]
    kern = functools.partial(_scan_kernel, chunk_fn, nb, len(ins_f), len(csts), state_init)
    return pl.pallas_call(
        kern,
        name=chunk_fn.__name__.strip("_") + "_scan",
        grid=(bsz // nb, n_chunks),
        in_specs=in_specs,
        out_specs=[pl.BlockSpec((nb, CHUNK, GROUP_W), idx_f), pl.BlockSpec((nb, CHUNK, GROUP_W), idx_b)],
        out_shape=[jax.ShapeDtypeStruct((bsz, seq, GROUP_W), F32)] * N_DIR,
        scratch_shapes=scratch,
        compiler_params=pltpu.CompilerParams(dimension_semantics=("arbitrary", "arbitrary"),
                                             vmem_limit_bytes=VMEM_LIMIT),
    )(*ins_f, *ins_b, *csts)


def _ada_kernel(c_ref, w_ref, b_ref, o_ref):
    o_ref[0] = _mm(_bf(_silu(c_ref[...])), _bf(w_ref[0])) + b_ref[0]


def _ada_mod(cc, ada_w, ada_b):
    depth, d_model, n6 = ada_w.shape
    tn = 1536
    return pl.pallas_call(
        _ada_kernel,
        name="ada_mod",
        grid=(depth, n6 // tn),
        in_specs=[pl.BlockSpec(cc.shape, lambda l, j: (0, 0)),
                  pl.BlockSpec((1, d_model, tn), lambda l, j: (l, 0, j)),
                  pl.BlockSpec((1, 1, tn), lambda l, j: (l, 0, j))],
        out_specs=pl.BlockSpec((1, cc.shape[0], tn), lambda l, j: (l, 0, j)),
        out_shape=jax.ShapeDtypeStruct((depth, cc.shape[0], n6), F32),
        compiler_params=pltpu.CompilerParams(dimension_semantics=("arbitrary", "arbitrary"),
                                             vmem_limit_bytes=VMEM_LIMIT),
    )(cc, ada_w, ada_b.reshape(depth, 1, n6))


def _rms_rows(x, gain):
    return x * lax.rsqrt(jnp.mean(x * x, axis=-1, keepdims=True) + EPS) * gain


def _pick_mod(modx_ref, modc_ref, is_ctx, row):
    return jnp.where(is_ctx, modc_ref[row:row + 1, :], modx_ref[0, row:row + 1, :])


def _in_proj_kernel(n_ctx, tm, widths, tok_ref, modx_ref, modc_ref, nw_ref, w_ref, *out_refs):
    t = pl.program_id(1)
    x = tok_ref[0]
    is_ctx = (t * tm + _iota((tm, 1), 0)) < n_ctx
    shift = _pick_mod(modx_ref, modc_ref, is_ctx, 0)
    scale = _pick_mod(modx_ref, modc_ref, is_ctx, 1)
    h = _bf(_rms_rows(x, nw_ref[...]) * (1.0 + scale) + shift)
    c0 = 0
    for o_ref, w in zip(out_refs, widths):
        o_ref[0] = _mm(h, w_ref[:, c0:c0 + w])
        c0 += w


def _in_proj(tok, modx, modc, nw, w_p, widths, n_ctx, tm):
    bsz, seq, d_model = tok.shape
    kern = functools.partial(_in_proj_kernel, n_ctx, tm, widths)
    return pl.pallas_call(
        kern,
        name="in_proj",
        grid=(bsz, seq // tm),
        in_specs=[pl.BlockSpec((1, tm, d_model), lambda b, t: (b, t, 0)),
                  pl.BlockSpec((1, 8, d_model), lambda b, t: (b, 0, 0)),
                  pl.BlockSpec((8, d_model), lambda b, t: (0, 0)),
                  pl.BlockSpec((1, d_model), lambda b, t: (0, 0)),
                  pl.BlockSpec(w_p.shape, lambda b, t: (0, 0), pipeline_mode=pl.Buffered(1))],
        out_specs=[pl.BlockSpec((1, tm, w), lambda b, t: (b, t, 0)) for w in widths],
        out_shape=[jax.ShapeDtypeStruct((bsz, seq, w), F32) for w in widths],
        compiler_params=pltpu.CompilerParams(dimension_semantics=("arbitrary", "arbitrary"),
                                             vmem_limit_bytes=VMEM_LIMIT),
    )(tok, modx, modc, nw, w_p)


def _row_tile(seq, cap, mult):
    best = mult
    for r in range(mult, min(seq, cap) + 1, mult):
        if seq % r == 0:
            best = r
    return best


def _rw_prep_kernel(n_ctx, seq, rows, z_ref, zp_ref, zn_ref, mu_ref, rp_ref, lora_ref, gup_ref,
                    base_ref, d0_ref, d1_ref, post_ref):
    t = pl.program_id(1)
    z = z_ref[0]
    row = _iota((rows, 1), 0)
    pos = t * rows + row
    z_prev = jnp.where(row == 0, zp_ref[0, 7:8, :], pltpu.roll(z, 1, 0))
    z_prev = jnp.where((pos != 0) & (pos != n_ctx), z_prev, 0.0)
    z_next = jnp.where(row == rows - 1, zn_ref[0, 0:1, :], pltpu.roll(z, rows - 1, 0))
    z_next = jnp.where((pos != seq - 1) & (pos != n_ctx - 1), z_next, 0.0)
    zm = z + mu_ref[0:1, :] * (z_prev - z) + mu_ref[1:2, :] * (z_next - z)
    r = zm[:, 0:GROUP_W]
    k = zm[:, GROUP_W:2 * GROUP_W]
    v = zm[:, 2 * GROUP_W:3 * GROUP_W]
    lo = zm[:, 3 * GROUP_W:3 * GROUP_W + 128]
    gl = zm[:, 3 * GROUP_W + 128:4 * GROUP_W]
    lo = jnp.where(_iota(lo.shape, 1) < 64, jnp.tanh(lo), lo)
    pre = _mm3(lo, lora_ref[...])
    rp = rp_ref[...]
    kk = _l2n_heads(k * rp[4:5, :])
    lw0 = -RW_DECAY_SCALE * _sigmoid(rp[0:1, :] + pre[:, 0:GROUP_W])
    a0 = _sigmoid(rp[2:3, :] + pre[:, GROUP_W:2 * GROUP_W])
    lw1 = -RW_DECAY_SCALE * _sigmoid(rp[1:2, :] + pre[:, 2 * GROUP_W:3 * GROUP_W])
    a1 = _sigmoid(rp[3:4, :] + pre[:, 3 * GROUP_W:4 * GROUP_W])
    k_a = rp[5:6, :]
    k_sum = k * (1.0 + (a0 - 1.0) * k_a) + k * (1.0 + (a1 - 1.0) * k_a)
    gate = _mm(_bf(_sigmoid(gl)), _bf(gup_ref[...]))
    bonus = _block_sum(r * k_sum * rp[6:7, :]) * v
    base_ref[0] = jnp.concatenate([r, k, v, kk], axis=1)
    d0_ref[0] = jnp.concatenate([lw0, a0], axis=1)
    d1_ref[0] = jnp.concatenate([lw1, a1], axis=1)
    post_ref[0] = jnp.concatenate([gate, bonus], axis=1)


def _rw_prep(z_rw, mu, rp, lora, gup, n_ctx):
    bsz, seq, w = z_rw.shape
    rows = _row_tile(seq, 640, 8)
    sub, n_sub = rows // 8, seq // 8
    kern = functools.partial(_rw_prep_kernel, n_ctx, seq, rows)
    full = lambda a: pl.BlockSpec(a.shape, lambda b, t, nd=a.ndim: (0,) * nd)
    widths = (4 * GROUP_W, 2 * GROUP_W, 2 * GROUP_W, 2 * GROUP_W)
    return pl.pallas_call(
        kern,
        name="rw_prep",
        grid=(bsz, seq // rows),
        in_specs=[pl.BlockSpec((1, rows, w), lambda b, t: (b, t, 0)),
                  pl.BlockSpec((1, 8, w), lambda b, t: (b, jnp.maximum(t * sub - 1, 0), 0)),
                  pl.BlockSpec((1, 8, w), lambda b, t: (b, jnp.minimum((t + 1) * sub, n_sub - 1), 0)),
                  full(mu), full(rp), full(lora), full(gup)],
        out_specs=[pl.BlockSpec((1, rows, ow), lambda b, t: (b, t, 0)) for ow in widths],
        out_shape=[jax.ShapeDtypeStruct((bsz, seq, ow), F32) for ow in widths],
        compiler_params=pltpu.CompilerParams(dimension_semantics=("arbitrary", "arbitrary"),
                                             vmem_limit_bytes=VMEM_LIMIT),
    )(z_rw, z_rw, z_rw, mu, rp, lora, gup)


def _gd_prep_kernel(n_ctx, seq, rows, zc_ref, zu_ref, zd_ref, cw_ref, o_ref):
    t = pl.program_id(1)
    cw = cw_ref[...]
    ext = jnp.concatenate([zu_ref[0], zc_ref[0], zd_ref[0]], axis=0)
    n_ext = rows + 2 * CHUNK
    pos_e = t * rows - CHUNK + _iota((n_ext, 1), 0)
    col = pos_e & (CHUNK - 1)
    src_ctx = pos_e < n_ctx
    has_l = jnp.where(src_ctx, pos_e, col) >= 1
    has_r = jnp.where(src_ctx, pos_e - (n_ctx - 2), col - (CHUNK - 2)) <= 0
    e_l = jnp.where(has_l, pltpu.roll(ext, 1, 0), 0.0)
    e_r = jnp.where(has_r, pltpu.roll(ext, n_ext - 1, 0), 0.0)

    def taps(di):
        lo = CHUNK * (1 + di)
        i = 3 * (1 + di)
        return (cw[i:i + 1, :] * e_l[lo:lo + rows] + cw[i + 1:i + 2, :] * ext[lo:lo + rows]
                + cw[i + 2:i + 3, :] * e_r[lo:lo + rows])

    pos_o = t * rows + _iota((rows, 1), 0)
    acc = taps(0)
    acc = acc + jnp.where(pos_o - CHUNK >= n_ctx, taps(-1), 0.0)
    acc = acc + jnp.where((pos_o >= n_ctx) & (pos_o + CHUNK < seq), taps(1), 0.0)
    y = _silu(acc)
    q = _l2n_heads(y[:, 0:GROUP_W])
    k = _l2n_heads(y[:, GROUP_W:2 * GROUP_W])
    o_ref[0] = jnp.concatenate([q, k, y[:, 2 * GROUP_W:]], axis=1)


def _gd_prep(z_gd, cw, n_ctx):
    bsz, seq, _ = z_gd.shape
    w = 3 * GROUP_W
    rows = _row_tile(seq, 1088, CHUNK)
    per, n_chunks = rows // CHUNK, seq // CHUNK
    kern = functools.partial(_gd_prep_kernel, n_ctx, seq, rows)
    return pl.pallas_call(
        kern,
        name="gd_prep",
        grid=(bsz, seq // rows),
        in_specs=[pl.BlockSpec((1, rows, w), lambda b, t: (b, t, 0)),
                  pl.BlockSpec((1, CHUNK, w), lambda b, t: (b, jnp.maximum(t * per - 1, 0), 0)),
                  pl.BlockSpec((1, CHUNK, w), lambda b, t: (b, jnp.minimum((t + 1) * per, n_chunks - 1), 0)),
                  pl.BlockSpec(cw.shape, lambda b, t: (0, 0))],
        out_specs=pl.BlockSpec((1, rows, w), lambda b, t: (b, t, 0)),
        out_shape=jax.ShapeDtypeStruct((bsz, seq, w), F32),
        compiler_params=pltpu.CompilerParams(dimension_semantics=("arbitrary", "arbitrary"),
                                             vmem_limit_bytes=VMEM_LIMIT),
    )(z_gd, z_gd, z_gd, cw)


def _head_rms(h, gain):
    return h * lax.rsqrt(_block_sum(h * h) * (1.0 / HEAD_V) + EPS) * gain


def _head_groupnorm(h, gain, bias):
    dlt = h - _block_sum(h) * (1.0 / HEAD_V)
    return dlt * lax.rsqrt(_block_sum(dlt * dlt) * (1.0 / HEAD_V) + GN_EPS) * gain + bias


def _out_mlp_kernel(n_ctx, tm, ff_chunk, final, tok_ref, hml_f, hml_b, hrw_f, hrw_b, hgl_f, hgl_b, hgd_f, hgd_b, oml_ref,
                    prw_ref, ogl_ref, zgd_ref, modx_ref, modc_ref, np_ref, n2_ref, fn_ref, wo_ref, w1_ref, w2_ref, o_ref):
    t = pl.program_id(1)
    x = tok_ref[0]
    is_ctx = (t * tm + _iota((tm, 1), 0)) < n_ctx
    np_ = np_ref[...]
    prw = prw_ref[0]
    mixed = jnp.concatenate([
        _head_rms(hml_f[0] + hml_b[0], np_[0:1, :]) * _sigmoid(oml_ref[0]),
        (_head_groupnorm(hrw_f[0] + hrw_b[0], np_[1:2, :], np_[2:3, :]) + prw[:, GROUP_W:]) * prw[:, :GROUP_W],
        _head_rms(hgl_f[0] + hgl_b[0], np_[3:4, :]) * _silu(ogl_ref[0]),
        _head_rms(hgd_f[0] + hgd_b[0], np_[4:5, :]) * _silu(zgd_ref[0]),
    ], axis=1)
    x = x + _pick_mod(modx_ref, modc_ref, is_ctx, 2) * _mm(_bf(mixed), wo_ref[...])
    shift = _pick_mod(modx_ref, modc_ref, is_ctx, 3)
    scale = _pick_mod(modx_ref, modc_ref, is_ctx, 4)
    h = _bf(_rms_rows(x, n2_ref[...]) * (1.0 + scale) + shift)
    d_ff = w1_ref.shape[1]
    acc = jnp.zeros(x.shape, F32)
    for c0 in range(0, d_ff, ff_chunk):
        a = jnp.maximum(_mm(h, w1_ref[:, c0:c0 + ff_chunk]), 0.0)
        acc = acc + _mm(_bf(a * a), w2_ref[c0:c0 + ff_chunk, :])
    x = x + _pick_mod(modx_ref, modc_ref, is_ctx, 5) * acc
    if final:
        x = _rms_rows(x, fn_ref[...])
    o_ref[0] = x


def _out_mlp(tok, hs, z_ml, rw_post, z_gl, z_gd, modx, modc, normp, n2, fnw, wo, w1, w2,
             n_ctx, tm, final):
    bsz, seq, d_model = tok.shape
    kern = functools.partial(_out_mlp_kernel, n_ctx, tm, 1024, final)
    row = lambda w, j=0: pl.BlockSpec((1, tm, w), lambda b, t, j=j: (b, t, j))
    const = lambda a: pl.BlockSpec(a.shape, lambda b, t, nd=a.ndim: (0,) * nd)
    wconst = lambda a: pl.BlockSpec(a.shape, lambda b, t: (0, 0), pipeline_mode=pl.Buffered(1))
    return pl.pallas_call(
        kern,
        name="out_mlp",
        grid=(bsz, seq // tm),
        in_specs=[row(d_model)] + [row(GROUP_W)] * len(hs) + [
                  row(GROUP_W, 2), row(2 * GROUP_W), row(GROUP_W, 2), row(GROUP_W, 3),
                  pl.BlockSpec((1, 8, d_model), lambda b, t: (b, 0, 0)), const(modc), const(normp), const(n2),
                  const(fnw), wconst(wo), wconst(w1), wconst(w2)],
        out_specs=row(d_model),
        out_shape=jax.ShapeDtypeStruct((bsz, seq, d_model), F32),
        compiler_params=pltpu.CompilerParams(dimension_semantics=("arbitrary", "arbitrary"),
                                             vmem_limit_bytes=VMEM_LIMIT),
    )(tok, *hs, z_ml, rw_post, z_gl, z_gd, modx, modc, normp, n2, fnw, wo, w1, w2)


def _pad_rows(a, rows=8):
    return jnp.concatenate([a, jnp.zeros((rows - a.shape[0],) + a.shape[1:], a.dtype)], axis=0)


def _misc_row(pairs):
    row = jnp.zeros((MISC_W,), F32)
    for col0, p in pairs:
        row = row.at[col0:col0 + N_DIR * HEADS].set(p.reshape(-1))
    return row


def _permute_w_in(w):
    d_model = w.shape[0]
    ml_w, rw_w, gl_w = 784, 1024, 784
    o_ml, o_rw, o_gl, o_gd = 0, ml_w, ml_w + rw_w, ml_w + rw_w + gl_w
    misc = jnp.concatenate([
        w[:, o_ml + 768:o_ml + 784],
        w[:, o_gl + 512:o_gl + 528],
        w[:, o_gd + 1024:o_gd + 1040],
        jnp.zeros((d_model, MISC_W - 48), w.dtype)], axis=1)
    return jnp.concatenate([
        w[:, o_ml:o_ml + 768],
        w[:, o_rw:o_rw + 1024],
        w[:, o_gl:o_gl + 512], w[:, o_gl + 528:o_gl + 784],
        w[:, o_gd:o_gd + 1024],
        misc], axis=1)


Z_WIDTHS = (768, 1024, 768, 1024, MISC_W)


SCAN_NB = 8


def _mix_layer(z_ml, z_rw, z_gl, z_gd, z_misc, p, n_ctx):
    nb = SCAN_NB if z_ml.shape[0] % SCAN_NB == 0 else 1
    s_ml = (((QK_W, 2 * GROUP_W), 0.0), ((1, GROUP_W), M_INIT))
    h_ml = _bidir_scan(_ml_chunk, [z_ml, z_misc], [z_ml, z_misc], [p['ml_gb']], s_ml, n_ctx, nb)
    h_gl = _bidir_scan(_gl_chunk, [z_gl, z_misc], [z_gl, z_misc], [p['gl_gup'], p['gl_gb']],
                       (((GROUP_W, QK_W), 0.0),), n_ctx, nb)
    qkv = _gd_prep(z_gd, p['gd_cw'], n_ctx)
    h_gd = _bidir_scan(_gd_chunk, [qkv, z_misc], [qkv, z_misc], [p['gd_gp']], (((GROUP_W, GROUP_W), 0.0),),
                       n_ctx, nb)
    base, d0, d1, rw_post = _rw_prep(z_rw, p['rw_mu'], p['rw_rp'], p['rw_lora'], p['rw_gup'], n_ctx)
    h_rw = _bidir_scan(_rw_chunk, [base, d0], [base, d1], [p['rw_rp']], (((GROUP_W, GROUP_W), 0.0),), n_ctx, nb)
    return (*h_ml, *h_rw, *h_gl, *h_gd), rw_post


def _layer_params(layer, w_in, ml_ig_b, ml_fg_b, ml_norm_w, rw_mu_prev, rw_mu_next, rw_w0, rw_w_up, rw_a0,
                  rw_a_up, rw_g_up, rw_k_k, rw_k_a, rw_r_k, rw_gn_w, rw_gn_b, gl_gate_up, gl_gate_b, gl_norm_w,
                  gd_conv_w, gd_a_log, gd_dt_bias, gd_norm_w):
    l = layer
    z64 = jnp.zeros((64, GROUP_W), F32)
    lora = jnp.concatenate([
        jnp.concatenate([rw_w_up[l, 0], z64, rw_w_up[l, 1], z64], axis=1),
        jnp.concatenate([z64, rw_a_up[l, 0], z64, rw_a_up[l, 1]], axis=1)], axis=0)
    gup = jnp.zeros((N_DIR, MISC_W, QK_W), F32).at[:, MISC_GL_AL:MISC_GL_AL + 16, :].set(gl_gate_up[l])
    return {
        'w_in': _bf(_permute_w_in(w_in[l])),
        'ml_gb': _pad_rows(_misc_row([(MISC_ML_IG, ml_ig_b[l]), (MISC_ML_FG, ml_fg_b[l])])[None, :]),
        'gl_gup': gup,
        'gl_gb': _pad_rows(gl_gate_b[l]),
        'gd_cw': _pad_rows(gd_conv_w[l].reshape(9, 3 * GROUP_W), 16),
        'gd_gp': _pad_rows(jnp.stack([_misc_row([(MISC_GD_AL, gd_dt_bias[l])]), _misc_row([(MISC_GD_AL, gd_a_log[l])])])),
        'rw_mu': _pad_rows(jnp.stack([rw_mu_prev[l], rw_mu_next[l]])),
        'rw_rp': _pad_rows(jnp.stack([rw_w0[l, 0], rw_w0[l, 1], rw_a0[l, 0], rw_a0[l, 1],
                                      rw_k_k[l], rw_k_a[l], rw_r_k[l]])),
        'rw_lora': lora,
        'rw_gup': rw_g_up[l],
        'normp': _pad_rows(jnp.stack([ml_norm_w[l], rw_gn_w[l], rw_gn_b[l], gl_norm_w[l], gd_norm_w[l]])),
    }


def kernel(x, c, ctx, c_ctx, ada_w, ada_b, norm1_w, norm2_w, w_in, w_out, ml_ig_b, ml_fg_b, ml_norm_w, rw_mu_prev, rw_mu_next, rw_w0, rw_w_up, rw_a0, rw_a_up, rw_g_up, rw_k_k, rw_k_a, rw_r_k, rw_gn_w, rw_gn_b, gl_gate_up, gl_gate_b, gl_norm_w, gd_conv_w, gd_a_log, gd_dt_bias, gd_norm_w, mlp_w1, mlp_w2, final_norm_w):
    bsz, seq_x, d_model = x.shape
    n_ctx = ctx.shape[1]
    depth = w_in.shape[0]
    seq = n_ctx + seq_x
    tm = seq // 8 if seq % 64 == 0 else seq
    tok = jnp.concatenate([ctx, x], axis=1)
    cc = _pad_rows(jnp.concatenate([c, c_ctx[None, :]], axis=0), ((bsz + 1 + 7) // 8) * 8)
    mod = _ada_mod(cc, ada_w, ada_b)
    fnw = final_norm_w[None, :]
    for layer in range(depth):
        p = _layer_params(layer, w_in, ml_ig_b, ml_fg_b, ml_norm_w, rw_mu_prev, rw_mu_next, rw_w0, rw_w_up, rw_a0,
                          rw_a_up, rw_g_up, rw_k_k, rw_k_a, rw_r_k, rw_gn_w, rw_gn_b, gl_gate_up, gl_gate_b,
                          gl_norm_w, gd_conv_w, gd_a_log, gd_dt_bias, gd_norm_w)
        modx = jnp.pad(mod[layer, :bsz].reshape(bsz, 6, d_model), ((0, 0), (0, 2), (0, 0)))
        modc = _pad_rows(mod[layer, bsz].reshape(6, d_model))
        z_ml, z_rw, z_gl, z_gd, z_misc = _in_proj(tok, modx, modc, norm1_w[layer][None, :], p['w_in'], Z_WIDTHS,
                                                  n_ctx, tm)
        hs, rw_post = _mix_layer(z_ml, z_rw, z_gl, z_gd, z_misc, p, n_ctx)
        tok = _out_mlp(tok, hs, z_ml, rw_post, z_gl, z_gd, modx, modc, p['normp'],
                       norm2_w[layer][None, :], fnw, _bf(w_out[layer]), _bf(mlp_w1[layer]), _bf(mlp_w2[layer]),
                       n_ctx, tm, layer == depth - 1)
    return tok[:, n_ctx:]
```

```python
import functools

import jax
import jax.numpy as jnp
from jax import lax
from jax.experimental import pallas as pl
from jax.experimental.pallas import tpu as pltpu

F32 = jnp.float32
BF16 = jnp.bfloat16

CHUNK = 64
HEADS = 4
HEAD_V = 64
HEAD_QK = 32
GROUP_W = HEADS * HEAD_V
QK_W = HEADS * HEAD_QK
N_DIR = 2
EPS = 1e-6
GN_EPS = 64e-5
M_INIT = -1e30
RW_DECAY_SCALE = 0.6065306597126334
GLA_LOGIT_NORM = 16.0
MISC_W = 128
MISC_ML_IG, MISC_ML_FG, MISC_GL_AL, MISC_GD_BL, MISC_GD_AL = 0, 8, 16, 32, 40
VMEM_LIMIT = 56 * 1024 * 1024


def _mm(a, b, prec=None):
    return lax.dot_general(a, b, (((1,), (0,)), ((), ())), precision=prec, preferred_element_type=F32)


def _mm_nt(a, b, prec=None):
    return lax.dot_general(a, b, (((1,), (1,)), ((), ())), precision=prec, preferred_element_type=F32)


def _mm_tn(a, b, prec=None):
    return lax.dot_general(a, b, (((0,), (0,)), ((), ())), precision=prec, preferred_element_type=F32)


def _bf(x):
    return x.astype(BF16)


def _iota(shape, dim):
    return lax.broadcasted_iota(jnp.int32, shape, dim)


def _sigmoid(x):
    return 1.0 / (1.0 + jnp.exp(-x))


def _softplus(x):
    return jnp.maximum(x, 0.0) + jnp.log1p(jnp.exp(-jnp.abs(x)))


def _log_sigmoid(x):
    return -_softplus(-x)


def _silu(x):
    return x * _sigmoid(x)


def _time_mask(d, width, strict):
    i = _iota((CHUNK, width), 0)
    j = _iota((CHUNK, width), 1) & (CHUNK - 1)
    if d == 0:
        return (j < i) if strict else (j <= i)
    return (j > i) if strict else (j >= i)


def _cumsum_mat(d):
    return jnp.where(_time_mask(d, CHUNK, False), 1.0, 0.0).astype(F32)


def _eye_cat(width):
    i = _iota((CHUNK, width), 0)
    j = _iota((CHUNK, width), 1) & (CHUNK - 1)
    return jnp.where(i == j, 1.0, 0.0).astype(F32)


def _bd_mask(rows, cols, row_shift, col_shift, col_and=None):
    r = _iota((rows, cols), 0) >> row_shift
    c = _iota((rows, cols), 1)
    if col_and is not None:
        c = c & col_and
    return r == (c >> col_shift)


def _stack_bd(x, col_shift):
    w = x.shape[1]
    m = _bd_mask(HEADS * CHUNK, w, 6, col_shift)
    return _bf(jnp.where(m, jnp.concatenate([x.astype(F32)] * HEADS, axis=0), 0.0))


def _split(x, n):
    out, r = [], x
    for i in range(n):
        p = r.astype(BF16)
        out.append(p)
        if i + 1 < n:
            r = r - p.astype(F32)
    return out


def _sel_left(t, x, n=2):
    return _mm(jnp.concatenate([_bf(t)] * n, axis=1), jnp.concatenate(_split(x, n), axis=0))


def _sel_right(x, e, n=2):
    return _mm(jnp.concatenate(_split(x, n), axis=1), jnp.concatenate([_bf(e)] * n, axis=0))


def _mm3(a, b):
    ah, al = _split(a, 2)
    bh, bl = _split(b, 2)
    return _mm(jnp.concatenate([ah, ah, al], axis=1), jnp.concatenate([bh, bl, bh], axis=0))


PAIR_W = 2 * HEAD_V


def _bd2(x):
    lo = _iota(x.shape, 1) < HEAD_V
    x = x.astype(F32)
    return _bf(jnp.concatenate([jnp.where(lo, x, 0.0), jnp.where(lo, 0.0, x)], axis=0))


def _bd_mm(lhs, rhs, nt=False):
    outs = []
    for g in range(GROUP_W // PAIR_W):
        sl = slice(g * PAIR_W, (g + 1) * PAIR_W)
        l = jnp.concatenate([_bf(a)[:, sl] for a in lhs], axis=1)
        r = jnp.concatenate([_bd2(b[:, sl]) for b in rhs], axis=1 if nt else 0)
        outs.append(_mm_nt(l, r) if nt else _mm(l, r))
    return jnp.concatenate(outs, axis=1)


def _bd_mm3(a, b):
    ah, al = _split(a, 2)
    return _bd_mm([ah, ah, al], [b, b - _bf(b).astype(F32), b])


INV_HI_STEPS = 4


def _cumsum_t(d, x):
    return _sel_left(_cumsum_mat(d), x)


def _expand_mat(col0s):
    r = _iota((MISC_W, GROUP_W), 0)
    hcol = _iota((MISC_W, GROUP_W), 1) >> 6
    return jnp.concatenate([jnp.where(r == hcol + c0, 1.0, 0.0) for c0 in col0s], axis=1).astype(F32)


def _to_cols(x):
    return _sel_left(jnp.ones((CHUNK, CHUNK), F32), x * _eye_cat(GROUP_W))


def _block_sum(x):
    w = x.shape[1]
    ones_bd = jnp.where(_bd_mask(w, w, 6, 6), 1.0, 0.0).astype(F32)
    return _sel_right(x, ones_bd, 2)


def _block_max(x):
    blk = _iota(x.shape, 1) >> 6
    out = jnp.zeros_like(x)
    for h in range(HEADS):
        sel = blk == h
        mh = jnp.max(jnp.where(sel, x, -jnp.inf), axis=1, keepdims=True)
        out = jnp.where(sel, mh, out)
    return out


def _inv_unit(n):
    r = _eye_cat(GROUP_W) + n
    p = n
    for it in range(5):
        mm = _bd_mm3 if it < INV_HI_STEPS else (lambda a, b: _bd_mm([a], [b]))
        if it == 0:
            p = mm(p, p)
            yield
        else:
            pr = mm(jnp.concatenate([p, r], axis=0), p)
            yield
            p = pr[:CHUNK]
            r = r + pr[CHUNK:]
    out = _bd_mm([r], [p])
    yield
    return r + out


def _l2n_heads(t):
    return t * lax.rsqrt(_block_sum(t * t) + EPS)


def _ml_chunk(d, xs, cs, st):
    z, misc = xs
    (gb,) = cs
    s_mat, m_row = st
    last = CHUNK - 1 if d == 0 else 0
    q = z[:, 0:QK_W] * (HEAD_QK ** -0.5)
    k = z[:, QK_W:2 * QK_W]
    v = z[:, 2 * QK_W:2 * QK_W + GROUP_W]
    t = misc + gb[0:1, :]
    is_fg = (_iota(misc.shape, 1) >> 3) == (MISC_ML_FG >> 3)
    y = jnp.where(is_fg, _log_sigmoid(t), t)
    cum = _cumsum_t(d, y)
    a = _mm_nt(_bf(q), _stack_bd(k, 5))
    inter = _mm(_bf(q), _bf(s_mat))
    yield
    ex = _sel_right(jnp.where(is_fg, cum, y), _expand_mat((MISC_ML_IG + HEADS * d, MISC_ML_FG + HEADS * d)))
    yield
    ig, bcum = ex[:, :GROUP_W], ex[:, GROUP_W:]
    b_end = bcum[last:last + 1, :]
    w_end = b_end - bcum + ig
    m_loc = jnp.max(w_end, axis=0, keepdims=True)
    e_end = jnp.exp(w_end - m_loc)
    cols = _to_cols(ig - bcum)
    loc = _mm_tn(_bf(k), _bf(jnp.concatenate([v * e_end, e_end], axis=1)))
    yield
    loc = jnp.where(_bd_mask(QK_W, 2 * GROUP_W, 5, 6, GROUP_W - 1), loc, 0.0)
    m_prev_s = b_end + m_row
    m_new = jnp.maximum(m_prev_s, m_loc)
    f_s = jnp.exp(m_prev_s - m_new)
    f_l = jnp.exp(m_loc - m_new)
    s_new = jnp.concatenate([f_s, f_s], axis=1) * s_mat + jnp.concatenate([f_l, f_l], axis=1) * loc
    logd = jnp.where(_time_mask(d, GROUP_W, False), bcum + cols, -jnp.inf)
    m_prev = bcum + m_row
    m_i = jnp.maximum(m_prev, _block_max(logd))
    sm = a * jnp.exp(logd - m_i)
    e_prev = jnp.exp(m_prev - m_i)
    num = _bd_mm([sm], [v])
    den = _bd_mm([sm], [jnp.ones((CHUNK, GROUP_W), F32)])
    yield
    num = num + e_prev * inter[:, :GROUP_W]
    den = den + e_prev * inter[:, GROUP_W:]
    h = num / jnp.maximum(jnp.abs(den), jnp.exp(-m_i))
    return h, (s_new, m_new)


def _gl_chunk(d, xs, cs, st):
    z, misc = xs
    gup, gb = cs
    (s_mat,) = st
    last = CHUNK - 1 if d == 0 else 0
    mid = CHUNK // 2 if d == 0 else CHUNK - 1 - CHUNK // 2
    q = z[:, 0:QK_W] * (HEAD_QK ** -0.5)
    k = z[:, QK_W:2 * QK_W]
    v = z[:, 2 * QK_W:2 * QK_W + GROUP_W]
    pre = _mm3(misc, gup[d])
    yield
    la = _log_sigmoid(pre + gb[d:d + 1, :]) * (1.0 / GLA_LOGIT_NORM)
    g = _cumsum_t(d, la)
    yield
    g_end = g[last:last + 1, :]
    g_mid = g[mid:mid + 1, :]
    qd = q * jnp.exp(g - g_mid)
    kd = k * jnp.exp(g_mid - g)
    att = _mm_nt(_bf(qd), _stack_bd(kd, 5))
    s_loc = _mm_tn(_bf(v), _bf(k * jnp.exp(g_end - g)))
    o_inter = _mm_nt(_bf(q * jnp.exp(g)), _bf(s_mat))
    yield
    att = jnp.where(_time_mask(d, GROUP_W, False), att, 0.0)
    s_new = jnp.exp(g_end) * s_mat + jnp.where(_bd_mask(GROUP_W, QK_W, 6, 5), s_loc, 0.0)
    o_intra = _bd_mm([att], [v])
    yield
    return o_inter + o_intra, (s_new,)


def _gd_chunk(d, xs, cs, st):
    qkv, misc = xs
    (gp,) = cs
    (s_mat,) = st
    last = CHUNK - 1 if d == 0 else 0
    q = qkv[:, 0:GROUP_W] * (HEAD_V ** -0.5)
    k = qkv[:, GROUP_W:2 * GROUP_W]
    v = qkv[:, 2 * GROUP_W:3 * GROUP_W]
    t = misc + gp[0:1, :]
    is_al = (_iota(misc.shape, 1) >> 3) == (MISC_GD_AL >> 3)
    y = jnp.where(is_al, -jnp.exp(gp[1:2, :]) * _softplus(t), _sigmoid(t))
    cum = _cumsum_t(d, y)
    x2 = _bd_mm([jnp.concatenate([k, q], axis=0)], [k], nt=True)
    yield
    ex = _sel_right(jnp.where(is_al, cum, y), _expand_mat((MISC_GD_BL + HEADS * d, MISC_GD_AL + HEADS * d)))
    yield
    beta, g = ex[:, :GROUP_W], ex[:, GROUP_W:]
    g_end = g[last:last + 1, :]
    cols = _to_cols(g)
    yield
    dec0 = jnp.where(_time_mask(d, GROUP_W, False), jnp.exp(g - cols), 0.0)
    kk, qk = x2[:CHUNK], x2[CHUNK:]
    mm = jnp.where(_time_mask(d, GROUP_W, True), beta * kk * dec0, 0.0)
    ainv = yield from _inv_unit(-mm)
    eg = jnp.exp(g)
    u = _bd_mm([ainv], [v * beta])
    w = _bd_mm([ainv], [k * (beta * eg)])
    yield
    ws = _mm(_bf(jnp.concatenate([w, q * eg], axis=0)), _bf(s_mat))
    yield
    v_new = u - ws[:CHUNK]
    o_inter = ws[CHUNK:]
    k_end = k * jnp.exp(g_end - g)
    upd = _mm_tn(_bf(k_end), _bf(v_new))
    o_intra = _bd_mm([qk * dec0], [v_new])
    yield
    s_new = jnp.exp(g_end) * s_mat + jnp.where(_bd_mask(GROUP_W, GROUP_W, 6, 6), upd, 0.0)
    return o_inter + o_intra, (s_new,)


def _rw_chunk(d, xs, cs, st):
    base, dirp = xs
    (rp,) = cs
    (s_mat,) = st
    last = CHUNK - 1 if d == 0 else 0
    r = base[:, 0:GROUP_W]
    k = base[:, GROUP_W:2 * GROUP_W]
    v = base[:, 2 * GROUP_W:3 * GROUP_W]
    kk = base[:, 3 * GROUP_W:4 * GROUP_W]
    lw = dirp[:, 0:GROUP_W]
    a = dirp[:, GROUP_W:2 * GROUP_W]
    k_a = rp[5:6, :]
    kt = k * (1.0 + (a - 1.0) * k_a)
    ap = kk * a
    g = _cumsum_t(d, lw)
    yield
    g_end = g[last:last + 1, :]
    eg = jnp.exp(g)
    eng = jnp.exp(-g)
    r_h = r * eg
    a_h = ap * eng
    k_h = kt * eng
    b_h = -kk * jnp.exp(g - lw)
    br = jnp.concatenate([b_h, r_h], axis=0)
    xa = _bd_mm([br], [a_h], nt=True)
    xk = _bd_mm([br], [k_h], nt=True)
    yield
    tm1 = _time_mask(d, GROUP_W, True)
    tm0 = _time_mask(d, GROUP_W, False)
    m_ab = jnp.where(tm1, xa[:CHUNK], 0.0)
    m_bk = jnp.where(tm1, xk[:CHUNK], 0.0)
    att_a = jnp.where(tm0, xa[CHUNK:], 0.0)
    att_k = jnp.where(tm0, xk[CHUNK:], 0.0)
    rhs2 = _bd_mm([m_bk], [v])
    ainv = yield from _inv_unit(m_ab)
    w_mat = _bd_mm([ainv], [b_h])
    u0 = _bd_mm([ainv], [rhs2])
    yield
    ws = _mm_nt(_bf(jnp.concatenate([w_mat, r_h], axis=0)), _bf(s_mat))
    yield
    u = u0 + ws[:CHUNK]
    y_inter = ws[CHUNK:]
    dec_end = jnp.exp(g_end - g)
    upd = _mm_tn(_bf(jnp.concatenate([u, v], axis=0)), _bf(jnp.concatenate([ap * dec_end, kt * dec_end], axis=0)))
    y_intra = _bd_mm([att_a, att_k], [u, v])
    yield
    s_new = jnp.exp(g_end) * s_mat + jnp.where(_bd_mask(GROUP_W, GROUP_W, 6, 6), upd, 0.0)
    return y_inter + y_intra, (s_new,)


class _Mixer:
    def __init__(self, chunk_fn, ins_f, ins_b, csts, state_init):
        self.chunk_fn, self.ins_f, self.ins_b, self.csts, self.state_init = chunk_fn, ins_f, ins_b, csts, state_init


def _scan_kernel(layout, nb, *refs):
    pos = 0
    parts = []
    for chunk_fn, n_in, n_cst, state_init in layout:
        ins = (refs[pos:pos + n_in], refs[pos + n_in:pos + 2 * n_in])
        csts = refs[pos + 2 * n_in:pos + 2 * n_in + n_cst]
        pos += 2 * n_in + n_cst
        parts.append([chunk_fn, ins, csts, state_init])
    for p in parts:
        p.append(refs[pos:pos + N_DIR])
        pos += N_DIR
    for p in parts:
        p.append(refs[pos:pos + len(p[3])])
        pos += len(p[3])

    @pl.when(pl.program_id(1) == 0)
    def _():
        for _, _, _, state_init, _, st_refs in parts:
            for i, (shape, val) in enumerate(state_init):
                st_refs[i][...] = jnp.full((N_DIR * nb,) + shape, val, F32)

    chains, gens = [], []
    for bi in range(nb):
        for pi, (chunk_fn, ins, csts, state_init, _, st_refs) in enumerate(parts):
            cs = tuple(c[...] for c in csts)
            for d in range(N_DIR):
                xs = tuple(r[bi] for r in ins[d])
                st = tuple(st_refs[i][d * nb + bi] for i in range(len(state_init)))
                chains.append((pi, bi, d))
                gens.append(chunk_fn(d, xs, cs, st))
    results = [None] * len(chains)
    while any(r is None for r in results):
        for ci, gen in enumerate(gens):
            if results[ci] is None:
                try:
                    next(gen)
                except StopIteration as stop:
                    results[ci] = stop.value
    for (pi, bi, d), (y, st_new) in zip(chains, results):
        o_refs, st_refs = parts[pi][4], parts[pi][5]
        for i, v in enumerate(st_new):
            st_refs[i][d * nb + bi] = v
        o_refs[d][bi] = y


def _bidir_scan(mixers, n_ctx, nb):
    bsz, seq = mixers[0].ins_f[0].shape[0], mixers[0].ins_f[0].shape[1]
    n_chunks, n_ctx_chunks = seq // CHUNK, n_ctx // CHUNK

    def idx_f(b, s):
        return (b, s, 0)

    def idx_b(b, s):
        return (b, jnp.where(s < n_ctx_chunks, n_ctx_chunks - 1 - s, n_chunks - 1 + n_ctx_chunks - s), 0)

    in_specs, args, layout, scratch = [], [], [], []
    for m in mixers:
        in_specs += [pl.BlockSpec((nb, CHUNK, a.shape[2]), idx_f) for a in m.ins_f]
        in_specs += [pl.BlockSpec((nb, CHUNK, a.shape[2]), idx_b) for a in m.ins_b]
        in_specs += [pl.BlockSpec(c.shape, lambda b, s, nd=c.ndim: (0,) * nd) for c in m.csts]
        args += [*m.ins_f, *m.ins_b, *m.csts]
        layout.append((m.chunk_fn, len(m.ins_f), len(m.csts), m.state_init))
        scratch += [pltpu.VMEM((N_DIR * nb,) + shape, F32) for (shape, _) in m.state_init]
    outs = pl.pallas_call(
        functools.partial(_scan_kernel, tuple(layout), nb),
        name="scan_" + "_".join(m.chunk_fn.__name__.strip("_").split("_")[0] for m in mixers),
        grid=(bsz // nb, n_chunks),
        in_specs=in_specs,
        out_specs=[pl.BlockSpec((nb, CHUNK, GROUP_W), idx) for _ in mixers for idx in (idx_f, idx_b)],
        out_shape=[jax.ShapeDtypeStruct((bsz, seq, GROUP_W), F32)] * (N_DIR * len(mixers)),
        scratch_shapes=scratch,
        compiler_params=pltpu.CompilerParams(dimension_semantics=("arbitrary", "arbitrary"),
                                             vmem_limit_bytes=VMEM_LIMIT),
    )(*args)
    return [tuple(outs[N_DIR * i:N_DIR * (i + 1)]) for i in range(len(mixers))]


def _ada_kernel(c_ref, w_ref, b_ref, o_ref):
    o_ref[0] = _mm(_bf(_silu(c_ref[...])), _bf(w_ref[0])) + b_ref[0]


def _ada_mod(cc, ada_w, ada_b):
    depth, d_model, n6 = ada_w.shape
    tn = 1536
    return pl.pallas_call(
        _ada_kernel,
        name="ada_mod",
        grid=(depth, n6 // tn),
        in_specs=[pl.BlockSpec(cc.shape, lambda l, j: (0, 0)),
                  pl.BlockSpec((1, d_model, tn), lambda l, j: (l, 0, j)),
                  pl.BlockSpec((1, 1, tn), lambda l, j: (l, 0, j))],
        out_specs=pl.BlockSpec((1, cc.shape[0], tn), lambda l, j: (l, 0, j)),
        out_shape=jax.ShapeDtypeStruct((depth, cc.shape[0], n6), F32),
        compiler_params=pltpu.CompilerParams(dimension_semantics=("arbitrary", "arbitrary"),
                                             vmem_limit_bytes=VMEM_LIMIT),
    )(cc, ada_w, ada_b.reshape(depth, 1, n6))


def _rms_rows(x, gain):
    return x * lax.rsqrt(jnp.mean(x * x, axis=-1, keepdims=True) + EPS) * gain


def _pick_mod(modx_ref, modc_ref, is_ctx, row):
    return jnp.where(is_ctx, modc_ref[row:row + 1, :], modx_ref[0, row:row + 1, :])


def _in_proj_kernel(n_ctx, tm, widths, tok_ref, modx_ref, modc_ref, nw_ref, w_ref, *out_refs):
    t = pl.program_id(1)
    x = tok_ref[0]
    is_ctx = (t * tm + _iota((tm, 1), 0)) < n_ctx
    shift = _pick_mod(modx_ref, modc_ref, is_ctx, 0)
    scale = _pick_mod(modx_ref, modc_ref, is_ctx, 1)
    h = _bf(_rms_rows(x, nw_ref[...]) * (1.0 + scale) + shift)
    c0 = 0
    for o_ref, w in zip(out_refs, widths):
        o_ref[0] = _mm(h, w_ref[:, c0:c0 + w])
        c0 += w


def _in_proj(tok, modx, modc, nw, w_p, widths, n_ctx, tm):
    bsz, seq, d_model = tok.shape
    kern = functools.partial(_in_proj_kernel, n_ctx, tm, widths)
    return pl.pallas_call(
        kern,
        name="in_proj",
        grid=(bsz, seq // tm),
        in_specs=[pl.BlockSpec((1, tm, d_model), lambda b, t: (b, t, 0)),
                  pl.BlockSpec((1, 8, d_model), lambda b, t: (b, 0, 0)),
                  pl.BlockSpec((8, d_model), lambda b, t: (0, 0)),
                  pl.BlockSpec((1, d_model), lambda b, t: (0, 0)),
                  pl.BlockSpec(w_p.shape, lambda b, t: (0, 0), pipeline_mode=pl.Buffered(1))],
        out_specs=[pl.BlockSpec((1, tm, w), lambda b, t: (b, t, 0)) for w in widths],
        out_shape=[jax.ShapeDtypeStruct((bsz, seq, w), F32) for w in widths],
        compiler_params=pltpu.CompilerParams(dimension_semantics=("arbitrary", "arbitrary"),
                                             vmem_limit_bytes=VMEM_LIMIT),
    )(tok, modx, modc, nw, w_p)


def _row_tile(seq, cap, mult):
    best = mult
    for r in range(mult, min(seq, cap) + 1, mult):
        if seq % r == 0:
            best = r
    return best


def _rw_prep_kernel(n_ctx, seq, rows, z_ref, zp_ref, zn_ref, mu_ref, rp_ref, lora_ref, gup_ref,
                    base_ref, d0_ref, d1_ref, post_ref):
    t = pl.program_id(1)
    z = z_ref[0]
    row = _iota((rows, 1), 0)
    pos = t * rows + row
    z_prev = jnp.where(row == 0, zp_ref[0, 7:8, :], pltpu.roll(z, 1, 0))
    z_prev = jnp.where((pos != 0) & (pos != n_ctx), z_prev, 0.0)
    z_next = jnp.where(row == rows - 1, zn_ref[0, 0:1, :], pltpu.roll(z, rows - 1, 0))
    z_next = jnp.where((pos != seq - 1) & (pos != n_ctx - 1), z_next, 0.0)
    zm = z + mu_ref[0:1, :] * (z_prev - z) + mu_ref[1:2, :] * (z_next - z)
    r = zm[:, 0:GROUP_W]
    k = zm[:, GROUP_W:2 * GROUP_W]
    v = zm[:, 2 * GROUP_W:3 * GROUP_W]
    lo = zm[:, 3 * GROUP_W:3 * GROUP_W + 128]
    gl = zm[:, 3 * GROUP_W + 128:4 * GROUP_W]
    lo = jnp.where(_iota(lo.shape, 1) < 64, jnp.tanh(lo), lo)
    pre = _mm3(lo, lora_ref[...])
    rp = rp_ref[...]
    kk = _l2n_heads(k * rp[4:5, :])
    lw0 = -RW_DECAY_SCALE * _sigmoid(rp[0:1, :] + pre[:, 0:GROUP_W])
    a0 = _sigmoid(rp[2:3, :] + pre[:, GROUP_W:2 * GROUP_W])
    lw1 = -RW_DECAY_SCALE * _sigmoid(rp[1:2, :] + pre[:, 2 * GROUP_W:3 * GROUP_W])
    a1 = _sigmoid(rp[3:4, :] + pre[:, 3 * GROUP_W:4 * GROUP_W])
    k_a = rp[5:6, :]
    k_sum = k * (1.0 + (a0 - 1.0) * k_a) + k * (1.0 + (a1 - 1.0) * k_a)
    gate = _mm(_bf(_sigmoid(gl)), _bf(gup_ref[...]))
    bonus = _block_sum(r * k_sum * rp[6:7, :]) * v
    base_ref[0] = jnp.concatenate([r, k, v, kk], axis=1)
    d0_ref[0] = jnp.concatenate([lw0, a0], axis=1)
    d1_ref[0] = jnp.concatenate([lw1, a1], axis=1)
    post_ref[0] = jnp.concatenate([gate, bonus], axis=1)


def _rw_prep(z_rw, mu, rp, lora, gup, n_ctx):
    bsz, seq, w = z_rw.shape
    rows = _row_tile(seq, 640, 8)
    sub, n_sub = rows // 8, seq // 8
    kern = functools.partial(_rw_prep_kernel, n_ctx, seq, rows)
    full = lambda a: pl.BlockSpec(a.shape, lambda b, t, nd=a.ndim: (0,) * nd)
    widths = (4 * GROUP_W, 2 * GROUP_W, 2 * GROUP_W, 2 * GROUP_W)
    return pl.pallas_call(
        kern,
        name="rw_prep",
        grid=(bsz, seq // rows),
        in_specs=[pl.BlockSpec((1, rows, w), lambda b, t: (b, t, 0)),
                  pl.BlockSpec((1, 8, w), lambda b, t: (b, jnp.maximum(t * sub - 1, 0), 0)),
                  pl.BlockSpec((1, 8, w), lambda b, t: (b, jnp.minimum((t + 1) * sub, n_sub - 1), 0)),
                  full(mu), full(rp), full(lora), full(gup)],
        out_specs=[pl.BlockSpec((1, rows, ow), lambda b, t: (b, t, 0)) for ow in widths],
        out_shape=[jax.ShapeDtypeStruct((bsz, seq, ow), F32) for ow in widths],
        compiler_params=pltpu.CompilerParams(dimension_semantics=("arbitrary", "arbitrary"),
                                             vmem_limit_bytes=VMEM_LIMIT),
    )(z_rw, z_rw, z_rw, mu, rp, lora, gup)


def _gd_prep_kernel(n_ctx, seq, rows, zc_ref, zu_ref, zd_ref, cw_ref, o_ref):
    t = pl.program_id(1)
    cw = cw_ref[...]
    ext = jnp.concatenate([zu_ref[0], zc_ref[0], zd_ref[0]], axis=0)
    n_ext = rows + 2 * CHUNK
    pos_e = t * rows - CHUNK + _iota((n_ext, 1), 0)
    col = pos_e & (CHUNK - 1)
    src_ctx = pos_e < n_ctx
    has_l = jnp.where(src_ctx, pos_e, col) >= 1
    has_r = jnp.where(src_ctx, pos_e - (n_ctx - 2), col - (CHUNK - 2)) <= 0
    e_l = jnp.where(has_l, pltpu.roll(ext, 1, 0), 0.0)
    e_r = jnp.where(has_r, pltpu.roll(ext, n_ext - 1, 0), 0.0)

    def taps(di):
        lo = CHUNK * (1 + di)
        i = 3 * (1 + di)
        return (cw[i:i + 1, :] * e_l[lo:lo + rows] + cw[i + 1:i + 2, :] * ext[lo:lo + rows]
                + cw[i + 2:i + 3, :] * e_r[lo:lo + rows])

    pos_o = t * rows + _iota((rows, 1), 0)
    acc = taps(0)
    acc = acc + jnp.where(pos_o - CHUNK >= n_ctx, taps(-1), 0.0)
    acc = acc + jnp.where((pos_o >= n_ctx) & (pos_o + CHUNK < seq), taps(1), 0.0)
    y = _silu(acc)
    q = _l2n_heads(y[:, 0:GROUP_W])
    k = _l2n_heads(y[:, GROUP_W:2 * GROUP_W])
    o_ref[0] = jnp.concatenate([q, k, y[:, 2 * GROUP_W:]], axis=1)


def _gd_prep(z_gd, cw, n_ctx):
    bsz, seq, _ = z_gd.shape
    w = 3 * GROUP_W
    rows = _row_tile(seq, 1088, CHUNK)
    per, n_chunks = rows // CHUNK, seq // CHUNK
    kern = functools.partial(_gd_prep_kernel, n_ctx, seq, rows)
    return pl.pallas_call(
        kern,
        name="gd_prep",
        grid=(bsz, seq // rows),
        in_specs=[pl.BlockSpec((1, rows, w), lambda b, t: (b, t, 0)),
                  pl.BlockSpec((1, CHUNK, w), lambda b, t: (b, jnp.maximum(t * per - 1, 0), 0)),
                  pl.BlockSpec((1, CHUNK, w), lambda b, t: (b, jnp.minimum((t + 1) * per, n_chunks - 1), 0)),
                  pl.BlockSpec(cw.shape, lambda b, t: (0, 0))],
        out_specs=pl.BlockSpec((1, rows, w), lambda b, t: (b, t, 0)),
        out_shape=jax.ShapeDtypeStruct((bsz, seq, w), F32),
        compiler_params=pltpu.CompilerParams(dimension_semantics=("arbitrary", "arbitrary"),
                                             vmem_limit_bytes=VMEM_LIMIT),
    )(z_gd, z_gd, z_gd, cw)


def _head_rms(h, gain):
    return h * lax.rsqrt(_block_sum(h * h) * (1.0 / HEAD_V) + EPS) * gain


def _head_groupnorm(h, gain, bias):
    dlt = h - _block_sum(h) * (1.0 / HEAD_V)
    return dlt * lax.rsqrt(_block_sum(dlt * dlt) * (1.0 / HEAD_V) + GN_EPS) * gain + bias


def _out_mlp_kernel(n_ctx, tm, ff_chunk, final, tok_ref, hml_f, hml_b, hrw_f, hrw_b, hgl_f, hgl_b, hgd_f, hgd_b, oml_ref,
                    prw_ref, ogl_ref, zgd_ref, modx_ref, modc_ref, np_ref, n2_ref, fn_ref, wo_ref, w1_ref, w2_ref, o_ref):
    t = pl.program_id(1)
    x = tok_ref[0]
    is_ctx = (t * tm + _iota((tm, 1), 0)) < n_ctx
    np_ = np_ref[...]
    prw = prw_ref[0]
    mixed = jnp.concatenate([
        _head_rms(hml_f[0] + hml_b[0], np_[0:1, :]) * _sigmoid(oml_ref[0]),
        (_head_groupnorm(hrw_f[0] + hrw_b[0], np_[1:2, :], np_[2:3, :]) + prw[:, GROUP_W:]) * prw[:, :GROUP_W],
        _head_rms(hgl_f[0] + hgl_b[0], np_[3:4, :]) * _silu(ogl_ref[0]),
        _head_rms(hgd_f[0] + hgd_b[0], np_[4:5, :]) * _silu(zgd_ref[0]),
    ], axis=1)
    x = x + _pick_mod(modx_ref, modc_ref, is_ctx, 2) * _mm(_bf(mixed), wo_ref[...])
    shift = _pick_mod(modx_ref, modc_ref, is_ctx, 3)
    scale = _pick_mod(modx_ref, modc_ref, is_ctx, 4)
    h = _bf(_rms_rows(x, n2_ref[...]) * (1.0 + scale) + shift)
    d_ff = w1_ref.shape[1]
    acc = jnp.zeros(x.shape, F32)
    for c0 in range(0, d_ff, ff_chunk):
        a = jnp.maximum(_mm(h, w1_ref[:, c0:c0 + ff_chunk]), 0.0)
        acc = acc + _mm(_bf(a * a), w2_ref[c0:c0 + ff_chunk, :])
    x = x + _pick_mod(modx_ref, modc_ref, is_ctx, 5) * acc
    if final:
        x = _rms_rows(x, fn_ref[...])
    o_ref[0] = x


def _out_mlp(tok, hs, z_ml, rw_post, z_gl, z_gd, modx, modc, normp, n2, fnw, wo, w1, w2,
             n_ctx, tm, final):
    bsz, seq, d_model = tok.shape
    kern = functools.partial(_out_mlp_kernel, n_ctx, tm, 1024, final)
    row = lambda w, j=0: pl.BlockSpec((1, tm, w), lambda b, t, j=j: (b, t, j))
    const = lambda a: pl.BlockSpec(a.shape, lambda b, t, nd=a.ndim: (0,) * nd)
    wconst = lambda a: pl.BlockSpec(a.shape, lambda b, t: (0, 0), pipeline_mode=pl.Buffered(1))
    return pl.pallas_call(
        kern,
        name="out_mlp",
        grid=(bsz, seq // tm),
        in_specs=[row(d_model)] + [row(GROUP_W)] * len(hs) + [
                  row(GROUP_W, 2), row(2 * GROUP_W), row(GROUP_W, 2), row(GROUP_W, 3),
                  pl.BlockSpec((1, 8, d_model), lambda b, t: (b, 0, 0)), const(modc), const(normp), const(n2),
                  const(fnw), wconst(wo), wconst(w1), wconst(w2)],
        out_specs=row(d_model),
        out_shape=jax.ShapeDtypeStruct((bsz, seq, d_model), F32),
        compiler_params=pltpu.CompilerParams(dimension_semantics=("arbitrary", "arbitrary"),
                                             vmem_limit_bytes=VMEM_LIMIT),
    )(tok, *hs, z_ml, rw_post, z_gl, z_gd, modx, modc, normp, n2, fnw, wo, w1, w2)


def _pad_rows(a, rows=8):
    return jnp.concatenate([a, jnp.zeros((rows - a.shape[0],) + a.shape[1:], a.dtype)], axis=0)


def _misc_row(pairs):
    row = jnp.zeros((MISC_W,), F32)
    for col0, p in pairs:
        row = row.at[col0:col0 + N_DIR * HEADS].set(p.reshape(-1))
    return row


def _permute_w_in(w):
    d_model = w.shape[0]
    ml_w, rw_w, gl_w = 784, 1024, 784
    o_ml, o_rw, o_gl, o_gd = 0, ml_w, ml_w + rw_w, ml_w + rw_w + gl_w
    misc = jnp.concatenate([
        w[:, o_ml + 768:o_ml + 784],
        w[:, o_gl + 512:o_gl + 528],
        w[:, o_gd + 1024:o_gd + 1040],
        jnp.zeros((d_model, MISC_W - 48), w.dtype)], axis=1)
    return jnp.concatenate([
        w[:, o_ml:o_ml + 768],
        w[:, o_rw:o_rw + 1024],
        w[:, o_gl:o_gl + 512], w[:, o_gl + 528:o_gl + 784],
        w[:, o_gd:o_gd + 1024],
        misc], axis=1)


Z_WIDTHS = (768, 1024, 768, 1024, MISC_W)


SCAN_NB = 8
SCAN_GROUPS = (('ml', 'rw'), ('gl', 'gd'))


def _mix_layer(z_ml, z_rw, z_gl, z_gd, z_misc, p, n_ctx):
    nb = SCAN_NB if z_ml.shape[0] % SCAN_NB == 0 else 1
    qkv = _gd_prep(z_gd, p['gd_cw'], n_ctx)
    base, d0, d1, rw_post = _rw_prep(z_rw, p['rw_mu'], p['rw_rp'], p['rw_lora'], p['rw_gup'], n_ctx)
    mixers = {
        'ml': _Mixer(_ml_chunk, [z_ml, z_misc], [z_ml, z_misc], [p['ml_gb']],
                     (((QK_W, 2 * GROUP_W), 0.0), ((1, GROUP_W), M_INIT))),
        'rw': _Mixer(_rw_chunk, [base, d0], [base, d1], [p['rw_rp']], (((GROUP_W, GROUP_W), 0.0),)),
        'gl': _Mixer(_gl_chunk, [z_gl, z_misc], [z_gl, z_misc], [p['gl_gup'], p['gl_gb']], (((GROUP_W, QK_W), 0.0),)),
        'gd': _Mixer(_gd_chunk, [qkv, z_misc], [qkv, z_misc], [p['gd_gp']], (((GROUP_W, GROUP_W), 0.0),)),
    }
    h = {}
    for group in SCAN_GROUPS:
        for name, out in zip(group, _bidir_scan([mixers[g] for g in group], n_ctx, nb)):
            h[name] = out
    return (*h['ml'], *h['rw'], *h['gl'], *h['gd']), rw_post


def _layer_params(layer, w_in, ml_ig_b, ml_fg_b, ml_norm_w, rw_mu_prev, rw_mu_next, rw_w0, rw_w_up, rw_a0,
                  rw_a_up, rw_g_up, rw_k_k, rw_k_a, rw_r_k, rw_gn_w, rw_gn_b, gl_gate_up, gl_gate_b, gl_norm_w,
                  gd_conv_w, gd_a_log, gd_dt_bias, gd_norm_w):
    l = layer
    z64 = jnp.zeros((64, GROUP_W), F32)
    lora = jnp.concatenate([
        jnp.concatenate([rw_w_up[l, 0], z64, rw_w_up[l, 1], z64], axis=1),
        jnp.concatenate([z64, rw_a_up[l, 0], z64, rw_a_up[l, 1]], axis=1)], axis=0)
    gup = jnp.zeros((N_DIR, MISC_W, QK_W), F32).at[:, MISC_GL_AL:MISC_GL_AL + 16, :].set(gl_gate_up[l])
    return {
        'w_in': _bf(_permute_w_in(w_in[l])),
        'ml_gb': _pad_rows(_misc_row([(MISC_ML_IG, ml_ig_b[l]), (MISC_ML_FG, ml_fg_b[l])])[None, :]),
        'gl_gup': gup,
        'gl_gb': _pad_rows(gl_gate_b[l]),
        'gd_cw': _pad_rows(gd_conv_w[l].reshape(9, 3 * GROUP_W), 16),
        'gd_gp': _pad_rows(jnp.stack([_misc_row([(MISC_GD_AL, gd_dt_bias[l])]), _misc_row([(MISC_GD_AL, gd_a_log[l])])])),
        'rw_mu': _pad_rows(jnp.stack([rw_mu_prev[l], rw_mu_next[l]])),
        'rw_rp': _pad_rows(jnp.stack([rw_w0[l, 0], rw_w0[l, 1], rw_a0[l, 0], rw_a0[l, 1],
                                      rw_k_k[l], rw_k_a[l], rw_r_k[l]])),
        'rw_lora': lora,
        'rw_gup': rw_g_up[l],
        'normp': _pad_rows(jnp.stack([ml_norm_w[l], rw_gn_w[l], rw_gn_b[l], gl_norm_w[l], gd_norm_w[l]])),
    }


def kernel(x, c, ctx, c_ctx, ada_w, ada_b, norm1_w, norm2_w, w_in, w_out, ml_ig_b, ml_fg_b, ml_norm_w, rw_mu_prev, rw_mu_next, rw_w0, rw_w_up, rw_a0, rw_a_up, rw_g_up, rw_k_k, rw_k_a, rw_r_k, rw_gn_w, rw_gn_b, gl_gate_up, gl_gate_b, gl_norm_w, gd_conv_w, gd_a_log, gd_dt_bias, gd_norm_w, mlp_w1, mlp_w2, final_norm_w):
    bsz, seq_x, d_model = x.shape
    n_ctx = ctx.shape[1]
    depth = w_in.shape[0]
    seq = n_ctx + seq_x
    tm = seq // 8 if seq % 64 == 0 else seq
    tok = jnp.concatenate([ctx, x], axis=1)
    cc = _pad_rows(jnp.concatenate([c, c_ctx[None, :]], axis=0), ((bsz + 1 + 7) // 8) * 8)
    mod = _ada_mod(cc, ada_w, ada_b)
    fnw = final_norm_w[None, :]
    for layer in range(depth):
        p = _layer_params(layer, w_in, ml_ig_b, ml_fg_b, ml_norm_w, rw_mu_prev, rw_mu_next, rw_w0, rw_w_up, rw_a0,
                          rw_a_up, rw_g_up, rw_k_k, rw_k_a, rw_r_k, rw_gn_w, rw_gn_b, gl_gate_up, gl_gate_b,
                          gl_norm_w, gd_conv_w, gd_a_log, gd_dt_bias, gd_norm_w)
        modx = jnp.pad(mod[layer, :bsz].reshape(bsz, 6, d_model), ((0, 0), (0, 2), (0, 0)))
        modc = _pad_rows(mod[layer, bsz].reshape(6, d_model))
        z_ml, z_rw, z_gl, z_gd, z_misc = _in_proj(tok, modx, modc, norm1_w[layer][None, :], p['w_in'], Z_WIDTHS,
                                                  n_ctx, tm)
        hs, rw_post = _mix_layer(z_ml, z_rw, z_gl, z_gd, z_misc, p, n_ctx)
        tok = _out_mlp(tok, hs, z_ml, rw_post, z_gl, z_gd, modx, modc, p['normp'],
                       norm2_w[layer][None, :], fnw, _bf(w_out[layer]), _bf(mlp_w1[layer]), _bf(mlp_w2[layer]),
                       n_ctx, tm, layer == depth - 1)
    return tok[:, n_ctx:]
```

```python
import functools

import jax
import jax.numpy as jnp
from jax import lax
from jax.experimental import pallas as pl
from jax.experimental.pallas import tpu as pltpu

F32 = jnp.float32
BF16 = jnp.bfloat16

CHUNK = 64
HEADS = 4
HEAD_V = 64
HEAD_QK = 32
GROUP_W = HEADS * HEAD_V
QK_W = HEADS * HEAD_QK
N_DIR = 2
EPS = 1e-6
GN_EPS = 64e-5
M_INIT = -1e30
RW_DECAY_SCALE = 0.6065306597126334
GLA_LOGIT_NORM = 16.0
MISC_W = 128
MISC_ML_IG, MISC_ML_FG, MISC_GL_AL, MISC_GD_BL, MISC_GD_AL = 0, 8, 16, 32, 40
VMEM_LIMIT = 56 * 1024 * 1024


def _mm(a, b, prec=None):
    return lax.dot_general(a, b, (((1,), (0,)), ((), ())), precision=prec, preferred_element_type=F32)


def _mm_nt(a, b, prec=None):
    return lax.dot_general(a, b, (((1,), (1,)), ((), ())), precision=prec, preferred_element_type=F32)


def _mm_tn(a, b, prec=None):
    return lax.dot_general(a, b, (((0,), (0,)), ((), ())), precision=prec, preferred_element_type=F32)


def _bf(x):
    return x.astype(BF16)


def _iota(shape, dim):
    return lax.broadcasted_iota(jnp.int32, shape, dim)


def _sigmoid(x):
    return 1.0 / (1.0 + jnp.exp(-x))


def _softplus(x):
    return jnp.maximum(x, 0.0) + jnp.log1p(jnp.exp(-jnp.abs(x)))


def _log_sigmoid(x):
    return -_softplus(-x)


def _silu(x):
    return x * _sigmoid(x)


def _time_mask(d, width, strict):
    i = _iota((CHUNK, width), 0)
    j = _iota((CHUNK, width), 1) & (CHUNK - 1)
    if d == 0:
        return (j < i) if strict else (j <= i)
    return (j > i) if strict else (j >= i)


def _cumsum_mat(d):
    return jnp.where(_time_mask(d, CHUNK, False), 1.0, 0.0).astype(F32)


def _eye_cat(width):
    i = _iota((CHUNK, width), 0)
    j = _iota((CHUNK, width), 1) & (CHUNK - 1)
    return jnp.where(i == j, 1.0, 0.0).astype(F32)


def _bd_mask(rows, cols, row_shift, col_shift, col_and=None):
    r = _iota((rows, cols), 0) >> row_shift
    c = _iota((rows, cols), 1)
    if col_and is not None:
        c = c & col_and
    return r == (c >> col_shift)


def _stack_bd(x, col_shift):
    w = x.shape[1]
    m = _bd_mask(HEADS * CHUNK, w, 6, col_shift)
    return _bf(jnp.where(m, jnp.concatenate([x.astype(F32)] * HEADS, axis=0), 0.0))


def _split(x, n):
    out, r = [], x
    for i in range(n):
        p = r.astype(BF16)
        out.append(p)
        if i + 1 < n:
            r = r - p.astype(F32)
    return out


def _sel_left(t, x, n=2):
    return _mm(jnp.concatenate([_bf(t)] * n, axis=1), jnp.concatenate(_split(x, n), axis=0))


def _sel_right(x, e, n=2):
    return _mm(jnp.concatenate(_split(x, n), axis=1), jnp.concatenate([_bf(e)] * n, axis=0))


def _mm3(a, b):
    ah, al = _split(a, 2)
    bh, bl = _split(b, 2)
    return _mm(jnp.concatenate([ah, ah, al], axis=1), jnp.concatenate([bh, bl, bh], axis=0))


PAIR_W = 2 * HEAD_V


def _bd2(x):
    lo = _iota(x.shape, 1) < HEAD_V
    x = x.astype(F32)
    return _bf(jnp.concatenate([jnp.where(lo, x, 0.0), jnp.where(lo, 0.0, x)], axis=0))


def _bd_mm(lhs, rhs, nt=False):
    outs = []
    for g in range(GROUP_W // PAIR_W):
        sl = slice(g * PAIR_W, (g + 1) * PAIR_W)
        l = jnp.concatenate([_bf(a)[:, sl] for a in lhs], axis=1)
        r = jnp.concatenate([_bd2(b[:, sl]) for b in rhs], axis=1 if nt else 0)
        outs.append(_mm_nt(l, r) if nt else _mm(l, r))
    return jnp.concatenate(outs, axis=1)


def _cumsum_t(d, x):
    return _sel_left(_cumsum_mat(d), x)


def _expand_mat(col0s):
    r = _iota((MISC_W, GROUP_W), 0)
    hcol = _iota((MISC_W, GROUP_W), 1) >> 6
    return jnp.concatenate([jnp.where(r == hcol + c0, 1.0, 0.0) for c0 in col0s], axis=1).astype(F32)


def _to_cols(x):
    return _sel_left(jnp.ones((CHUNK, CHUNK), F32), x * _eye_cat(GROUP_W))


def _block_sum(x):
    w = x.shape[1]
    ones_bd = jnp.where(_bd_mask(w, w, 6, 6), 1.0, 0.0).astype(F32)
    return _sel_right(x, ones_bd, 2)


def _block_max(x):
    blk = _iota(x.shape, 1) >> 6
    out = jnp.zeros_like(x)
    for h in range(HEADS):
        sel = blk == h
        mh = jnp.max(jnp.where(sel, x, -jnp.inf), axis=1, keepdims=True)
        out = jnp.where(sel, mh, out)
    return out


def _inv_unit(n):
    i = _iota((CHUNK, GROUP_W), 0)
    j = _iota((CHUNK, GROUP_W), 1) & (CHUNK - 1)
    x = _eye_cat(GROUP_W)
    for lvl in range(6):
        c = jnp.where(((i >> lvl) ^ (j >> lvl)) == 1, n, 0.0)
        if lvl == 0:
            x = x + c
        else:
            y = _bd_mm([c], [x])
            yield
            x = x + _bd_mm([x], [y])
            yield
    return x


def _l2n_heads(t):
    return t * lax.rsqrt(_block_sum(t * t) + EPS)


def _ml_chunk(d, xs, cs, st):
    z, misc = xs
    (gb,) = cs
    s_mat, m_row = st
    last = CHUNK - 1 if d == 0 else 0
    q = z[:, 0:QK_W] * (HEAD_QK ** -0.5)
    k = z[:, QK_W:2 * QK_W]
    v = z[:, 2 * QK_W:2 * QK_W + GROUP_W]
    t = misc + gb[0:1, :]
    is_fg = (_iota(misc.shape, 1) >> 3) == (MISC_ML_FG >> 3)
    y = jnp.where(is_fg, _log_sigmoid(t), t)
    cum = _cumsum_t(d, y)
    a = _mm_nt(_bf(q), _stack_bd(k, 5))
    inter = _mm(_bf(q), _bf(s_mat))
    yield
    ex = _sel_right(jnp.where(is_fg, cum, y), _expand_mat((MISC_ML_IG + HEADS * d, MISC_ML_FG + HEADS * d)))
    yield
    ig, bcum = ex[:, :GROUP_W], ex[:, GROUP_W:]
    b_end = bcum[last:last + 1, :]
    w_end = b_end - bcum + ig
    m_loc = jnp.max(w_end, axis=0, keepdims=True)
    e_end = jnp.exp(w_end - m_loc)
    cols = _to_cols(ig - bcum)
    loc = _mm_tn(_bf(k), _bf(jnp.concatenate([v * e_end, e_end], axis=1)))
    yield
    loc = jnp.where(_bd_mask(QK_W, 2 * GROUP_W, 5, 6, GROUP_W - 1), loc, 0.0)
    m_prev_s = b_end + m_row
    m_new = jnp.maximum(m_prev_s, m_loc)
    f_s = jnp.exp(m_prev_s - m_new)
    f_l = jnp.exp(m_loc - m_new)
    s_new = jnp.concatenate([f_s, f_s], axis=1) * s_mat + jnp.concatenate([f_l, f_l], axis=1) * loc
    logd = jnp.where(_time_mask(d, GROUP_W, False), bcum + cols, -jnp.inf)
    m_prev = bcum + m_row
    m_i = jnp.maximum(m_prev, _block_max(logd))
    sm = a * jnp.exp(logd - m_i)
    e_prev = jnp.exp(m_prev - m_i)
    num = _bd_mm([sm], [v])
    den = _bd_mm([sm], [jnp.ones((CHUNK, GROUP_W), F32)])
    yield
    num = num + e_prev * inter[:, :GROUP_W]
    den = den + e_prev * inter[:, GROUP_W:]
    h = num / jnp.maximum(jnp.abs(den), jnp.exp(-m_i))
    return h, (s_new, m_new)


def _gl_chunk(d, xs, cs, st):
    z, misc = xs
    gup, gb = cs
    (s_mat,) = st
    last = CHUNK - 1 if d == 0 else 0
    mid = CHUNK // 2 if d == 0 else CHUNK - 1 - CHUNK // 2
    q = z[:, 0:QK_W] * (HEAD_QK ** -0.5)
    k = z[:, QK_W:2 * QK_W]
    v = z[:, 2 * QK_W:2 * QK_W + GROUP_W]
    pre = _mm3(misc, gup[d])
    yield
    la = _log_sigmoid(pre + gb[d:d + 1, :]) * (1.0 / GLA_LOGIT_NORM)
    g = _cumsum_t(d, la)
    yield
    g_end = g[last:last + 1, :]
    g_mid = g[mid:mid + 1, :]
    qd = q * jnp.exp(g - g_mid)
    kd = k * jnp.exp(g_mid - g)
    att = _mm_nt(_bf(qd), _stack_bd(kd, 5))
    s_loc = _mm_tn(_bf(v), _bf(k * jnp.exp(g_end - g)))
    o_inter = _mm_nt(_bf(q * jnp.exp(g)), _bf(s_mat))
    yield
    att = jnp.where(_time_mask(d, GROUP_W, False), att, 0.0)
    s_new = jnp.exp(g_end) * s_mat + jnp.where(_bd_mask(GROUP_W, QK_W, 6, 5), s_loc, 0.0)
    o_intra = _bd_mm([att], [v])
    yield
    return o_inter + o_intra, (s_new,)


def _gd_chunk(d, xs, cs, st):
    qkv, misc = xs
    (gp,) = cs
    (s_mat,) = st
    last = CHUNK - 1 if d == 0 else 0
    q = qkv[:, 0:GROUP_W] * (HEAD_V ** -0.5)
    k = qkv[:, GROUP_W:2 * GROUP_W]
    v = qkv[:, 2 * GROUP_W:3 * GROUP_W]
    t = misc + gp[0:1, :]
    is_al = (_iota(misc.shape, 1) >> 3) == (MISC_GD_AL >> 3)
    y = jnp.where(is_al, -jnp.exp(gp[1:2, :]) * _softplus(t), _sigmoid(t))
    cum = _cumsum_t(d, y)
    x2 = _bd_mm([jnp.concatenate([k, q], axis=0)], [k], nt=True)
    yield
    ex = _sel_right(jnp.where(is_al, cum, y), _expand_mat((MISC_GD_BL + HEADS * d, MISC_GD_AL + HEADS * d)))
    yield
    beta, g = ex[:, :GROUP_W], ex[:, GROUP_W:]
    g_end = g[last:last + 1, :]
    cols = _to_cols(g)
    yield
    dec0 = jnp.where(_time_mask(d, GROUP_W, False), jnp.exp(g - cols), 0.0)
    kk, qk = x2[:CHUNK], x2[CHUNK:]
    mm = jnp.where(_time_mask(d, GROUP_W, True), beta * kk * dec0, 0.0)
    ainv = yield from _inv_unit(-mm)
    eg = jnp.exp(g)
    u = _bd_mm([ainv], [v * beta])
    w = _bd_mm([ainv], [k * (beta * eg)])
    yield
    ws = _mm(_bf(jnp.concatenate([w, q * eg], axis=0)), _bf(s_mat))
    yield
    v_new = u - ws[:CHUNK]
    o_inter = ws[CHUNK:]
    k_end = k * jnp.exp(g_end - g)
    upd = _mm_tn(_bf(k_end), _bf(v_new))
    o_intra = _bd_mm([qk * dec0], [v_new])
    yield
    s_new = jnp.exp(g_end) * s_mat + jnp.where(_bd_mask(GROUP_W, GROUP_W, 6, 6), upd, 0.0)
    return o_inter + o_intra, (s_new,)


def _rw_chunk(d, xs, cs, st):
    base, dirp = xs
    (rp,) = cs
    (s_mat,) = st
    last = CHUNK - 1 if d == 0 else 0
    r = base[:, 0:GROUP_W]
    k = base[:, GROUP_W:2 * GROUP_W]
    v = base[:, 2 * GROUP_W:3 * GROUP_W]
    kk = base[:, 3 * GROUP_W:4 * GROUP_W]
    lw = dirp[:, 0:GROUP_W]
    a = dirp[:, GROUP_W:2 * GROUP_W]
    k_a = rp[5:6, :]
    kt = k * (1.0 + (a - 1.0) * k_a)
    ap = kk * a
    g = _cumsum_t(d, lw)
    yield
    g_end = g[last:last + 1, :]
    eg = jnp.exp(g)
    eng = jnp.exp(-g)
    r_h = r * eg
    a_h = ap * eng
    k_h = kt * eng
    b_h = -kk * jnp.exp(g - lw)
    br = jnp.concatenate([b_h, r_h], axis=0)
    xa = _bd_mm([br], [a_h], nt=True)
    xk = _bd_mm([br], [k_h], nt=True)
    yield
    tm1 = _time_mask(d, GROUP_W, True)
    tm0 = _time_mask(d, GROUP_W, False)
    m_ab = jnp.where(tm1, xa[:CHUNK], 0.0)
    m_bk = jnp.where(tm1, xk[:CHUNK], 0.0)
    att_a = jnp.where(tm0, xa[CHUNK:], 0.0)
    att_k = jnp.where(tm0, xk[CHUNK:], 0.0)
    rhs2 = _bd_mm([m_bk], [v])
    ainv = yield from _inv_unit(m_ab)
    w_mat = _bd_mm([ainv], [b_h])
    u0 = _bd_mm([ainv], [rhs2])
    yield
    ws = _mm_nt(_bf(jnp.concatenate([w_mat, r_h], axis=0)), _bf(s_mat))
    yield
    u = u0 + ws[:CHUNK]
    y_inter = ws[CHUNK:]
    dec_end = jnp.exp(g_end - g)
    upd = _mm_tn(_bf(jnp.concatenate([u, v], axis=0)), _bf(jnp.concatenate([ap * dec_end, kt * dec_end], axis=0)))
    y_intra = _bd_mm([att_a, att_k], [u, v])
    yield
    s_new = jnp.exp(g_end) * s_mat + jnp.where(_bd_mask(GROUP_W, GROUP_W, 6, 6), upd, 0.0)
    return y_inter + y_intra, (s_new,)


class _Mixer:
    def __init__(self, chunk_fn, ins_f, ins_b, csts, state_init):
        self.chunk_fn, self.ins_f, self.ins_b, self.csts, self.state_init = chunk_fn, ins_f, ins_b, csts, state_init


def _scan_kernel(layout, nb, *refs):
    pos = 0
    parts = []
    for chunk_fn, n_in, n_cst, state_init in layout:
        ins = (refs[pos:pos + n_in], refs[pos + n_in:pos + 2 * n_in])
        csts = refs[pos + 2 * n_in:pos + 2 * n_in + n_cst]
        pos += 2 * n_in + n_cst
        parts.append([chunk_fn, ins, csts, state_init])
    for p in parts:
        p.append(refs[pos:pos + N_DIR])
        pos += N_DIR
    for p in parts:
        p.append(refs[pos:pos + len(p[3])])
        pos += len(p[3])

    @pl.when(pl.program_id(1) == 0)
    def _():
        for _, _, _, state_init, _, st_refs in parts:
            for i, (shape, val) in enumerate(state_init):
                st_refs[i][...] = jnp.full((N_DIR * nb,) + shape, val, F32)

    chains, gens = [], []
    for bi in range(nb):
        for pi, (chunk_fn, ins, csts, state_init, _, st_refs) in enumerate(parts):
            cs = tuple(c[...] for c in csts)
            for d in range(N_DIR):
                xs = tuple(r[bi] for r in ins[d])
                st = tuple(st_refs[i][d * nb + bi] for i in range(len(state_init)))
                chains.append((pi, bi, d))
                gens.append(chunk_fn(d, xs, cs, st))
    results = [None] * len(chains)
    while any(r is None for r in results):
        for ci, gen in enumerate(gens):
            if results[ci] is None:
                try:
                    next(gen)
                except StopIteration as stop:
                    results[ci] = stop.value
    for (pi, bi, d), (y, st_new) in zip(chains, results):
        o_refs, st_refs = parts[pi][4], parts[pi][5]
        for i, v in enumerate(st_new):
            st_refs[i][d * nb + bi] = v
        o_refs[d][bi] = y


def _bidir_scan(mixers, n_ctx, nb):
    bsz, seq = mixers[0].ins_f[0].shape[0], mixers[0].ins_f[0].shape[1]
    n_chunks, n_ctx_chunks = seq // CHUNK, n_ctx // CHUNK

    def idx_f(b, s):
        return (b, s, 0)

    def idx_b(b, s):
        return (b, jnp.where(s < n_ctx_chunks, n_ctx_chunks - 1 - s, n_chunks - 1 + n_ctx_chunks - s), 0)

    in_specs, args, layout, scratch = [], [], [], []
    for m in mixers:
        in_specs += [pl.BlockSpec((nb, CHUNK, a.shape[2]), idx_f) for a in m.ins_f]
        in_specs += [pl.BlockSpec((nb, CHUNK, a.shape[2]), idx_b) for a in m.ins_b]
        in_specs += [pl.BlockSpec(c.shape, lambda b, s, nd=c.ndim: (0,) * nd) for c in m.csts]
        args += [*m.ins_f, *m.ins_b, *m.csts]
        layout.append((m.chunk_fn, len(m.ins_f), len(m.csts), m.state_init))
        scratch += [pltpu.VMEM((N_DIR * nb,) + shape, F32) for (shape, _) in m.state_init]
    outs = pl.pallas_call(
        functools.partial(_scan_kernel, tuple(layout), nb),
        name="scan_" + "_".join(m.chunk_fn.__name__.strip("_").split("_")[0] for m in mixers),
        grid=(bsz // nb, n_chunks),
        in_specs=in_specs,
        out_specs=[pl.BlockSpec((nb, CHUNK, GROUP_W), idx) for _ in mixers for idx in (idx_f, idx_b)],
        out_shape=[jax.ShapeDtypeStruct((bsz, seq, GROUP_W), F32)] * (N_DIR * len(mixers)),
        scratch_shapes=scratch,
        compiler_params=pltpu.CompilerParams(dimension_semantics=("arbitrary", "arbitrary"),
                                             vmem_limit_bytes=VMEM_LIMIT),
    )(*args)
    return [tuple(outs[N_DIR * i:N_DIR * (i + 1)]) for i in range(len(mixers))]


def _ada_kernel(c_ref, w_ref, b_ref, o_ref):
    o_ref[0] = _mm(_bf(_silu(c_ref[...])), _bf(w_ref[0])) + b_ref[0]


def _ada_mod(cc, ada_w, ada_b):
    depth, d_model, n6 = ada_w.shape
    tn = 1536
    return pl.pallas_call(
        _ada_kernel,
        name="ada_mod",
        grid=(depth, n6 // tn),
        in_specs=[pl.BlockSpec(cc.shape, lambda l, j: (0, 0)),
                  pl.BlockSpec((1, d_model, tn), lambda l, j: (l, 0, j)),
                  pl.BlockSpec((1, 1, tn), lambda l, j: (l, 0, j))],
        out_specs=pl.BlockSpec((1, cc.shape[0], tn), lambda l, j: (l, 0, j)),
        out_shape=jax.ShapeDtypeStruct((depth, cc.shape[0], n6), F32),
        compiler_params=pltpu.CompilerParams(dimension_semantics=("arbitrary", "arbitrary"),
                                             vmem_limit_bytes=VMEM_LIMIT),
    )(cc, ada_w, ada_b.reshape(depth, 1, n6))


def _rms_rows(x, gain):
    return x * lax.rsqrt(jnp.mean(x * x, axis=-1, keepdims=True) + EPS) * gain


def _pick_mod(modx_ref, modc_ref, is_ctx, row):
    return jnp.where(is_ctx, modc_ref[row:row + 1, :], modx_ref[0, row:row + 1, :])


def _modulated_rows(x, pos, n_ctx, modx_ref, modc_ref, nw):
    is_ctx = pos < n_ctx
    shift = _pick_mod(modx_ref, modc_ref, is_ctx, 0)
    scale = _pick_mod(modx_ref, modc_ref, is_ctx, 1)
    return _bf(_rms_rows(x, nw) * (1.0 + scale) + shift)


def _rw_token_prep(z, z_before, z_after, pos, n_ctx, seq, mu, rp, lora, gup):
    rows = z.shape[0]
    row = _iota((rows, 1), 0)
    z_prev = jnp.where(row == 0, z_before, pltpu.roll(z, 1, 0))
    z_prev = jnp.where((pos != 0) & (pos != n_ctx), z_prev, 0.0)
    z_next = jnp.where(row == rows - 1, z_after, pltpu.roll(z, rows - 1, 0))
    z_next = jnp.where((pos != seq - 1) & (pos != n_ctx - 1), z_next, 0.0)
    zm = z + mu[0:1, :] * (z_prev - z) + mu[1:2, :] * (z_next - z)
    yield
    r = zm[:, 0:GROUP_W]
    k = zm[:, GROUP_W:2 * GROUP_W]
    v = zm[:, 2 * GROUP_W:3 * GROUP_W]
    lo = zm[:, 3 * GROUP_W:3 * GROUP_W + 128]
    gl = zm[:, 3 * GROUP_W + 128:4 * GROUP_W]
    lo = jnp.where(_iota(lo.shape, 1) < 64, jnp.tanh(lo), lo)
    pre = _mm3(lo, lora)
    yield
    kk = _l2n_heads(k * rp[4:5, :])
    gate = _mm(_bf(_sigmoid(gl)), _bf(gup))
    yield
    lw0 = -RW_DECAY_SCALE * _sigmoid(rp[0:1, :] + pre[:, 0:GROUP_W])
    a0 = _sigmoid(rp[2:3, :] + pre[:, GROUP_W:2 * GROUP_W])
    lw1 = -RW_DECAY_SCALE * _sigmoid(rp[1:2, :] + pre[:, 2 * GROUP_W:3 * GROUP_W])
    a1 = _sigmoid(rp[3:4, :] + pre[:, 3 * GROUP_W:4 * GROUP_W])
    k_a = rp[5:6, :]
    k_sum = k * (1.0 + (a0 - 1.0) * k_a) + k * (1.0 + (a1 - 1.0) * k_a)
    bonus = _block_sum(r * k_sum * rp[6:7, :]) * v
    yield
    return (jnp.concatenate([r, k, v, kk], axis=1), jnp.concatenate([lw0, a0], axis=1),
            jnp.concatenate([lw1, a1], axis=1), jnp.concatenate([gate, bonus], axis=1))


def _in_proj_kernel(n_ctx, seq, tm, cols, tok_ref, tokp_ref, tokn_ref, modx_ref, modc_ref, nw_ref, w_ref, mu_ref,
                    rp_ref, lora_ref, gup_ref, zml_ref, base_ref, d0_ref, d1_ref, post_ref, zgl_ref, zgd_ref,
                    misc_ref):
    t = pl.program_id(1)
    nw = nw_ref[...]
    pos = t * tm + _iota((tm, 1), 0)
    pos8 = _iota((8, 1), 0)
    h = _modulated_rows(tok_ref[0], pos, n_ctx, modx_ref, modc_ref, nw)
    c0, c1 = cols['rw']
    z_rw = _mm(h, w_ref[:, c0:c1])
    h_before = _modulated_rows(tokp_ref[0], t * tm - 8 + pos8, n_ctx, modx_ref, modc_ref, nw)
    h_after = _modulated_rows(tokn_ref[0], (t + 1) * tm + pos8, n_ctx, modx_ref, modc_ref, nw)
    z_before = _mm(h_before, w_ref[:, c0:c1])[7:8, :]
    z_after = _mm(h_after, w_ref[:, c0:c1])[0:1, :]
    prep = _rw_token_prep(z_rw, z_before, z_after, pos, n_ctx, seq, mu_ref[...], rp_ref[...], lora_ref[...],
                          gup_ref[...])
    rw_out = None
    for name, o_ref in (('ml', zml_ref), ('gl', zgl_ref), ('gd', zgd_ref), ('misc', misc_ref)):
        c0, c1 = cols[name]
        o_ref[0] = _mm(h, w_ref[:, c0:c1])
        if rw_out is None:
            try:
                next(prep)
            except StopIteration as stop:
                rw_out = stop.value
    while rw_out is None:
        try:
            next(prep)
        except StopIteration as stop:
            rw_out = stop.value
    base_ref[0], d0_ref[0], d1_ref[0], post_ref[0] = rw_out


def _in_proj(tok, modx, modc, nw, w_p, mu, rp, lora, gup, n_ctx, tm):
    bsz, seq, d_model = tok.shape
    cols, c0 = {}, 0
    for name, w in zip(('ml', 'rw', 'gl', 'gd', 'misc'), Z_WIDTHS):
        cols[name] = (c0, c0 + w)
        c0 += w
    widths = (Z_WIDTHS[0], 4 * GROUP_W, 2 * GROUP_W, 2 * GROUP_W, 2 * GROUP_W, Z_WIDTHS[2], Z_WIDTHS[3], Z_WIDTHS[4])
    sub, n_sub = tm // 8, seq // 8
    const = lambda a: pl.BlockSpec(a.shape, lambda b, t, nd=a.ndim: (0,) * nd)
    kern = functools.partial(_in_proj_kernel, n_ctx, seq, tm, cols)
    return pl.pallas_call(
        kern,
        name="in_proj",
        grid=(bsz, seq // tm),
        in_specs=[pl.BlockSpec((1, tm, d_model), lambda b, t: (b, t, 0)),
                  pl.BlockSpec((1, 8, d_model), lambda b, t: (b, jnp.maximum(t * sub - 1, 0), 0)),
                  pl.BlockSpec((1, 8, d_model), lambda b, t: (b, jnp.minimum((t + 1) * sub, n_sub - 1), 0)),
                  pl.BlockSpec((1, 8, d_model), lambda b, t: (b, 0, 0)),
                  const(modc), const(nw),
                  pl.BlockSpec(w_p.shape, lambda b, t: (0, 0), pipeline_mode=pl.Buffered(1)),
                  const(mu), const(rp), const(lora), const(gup)],
        out_specs=[pl.BlockSpec((1, tm, w), lambda b, t: (b, t, 0)) for w in widths],
        out_shape=[jax.ShapeDtypeStruct((bsz, seq, w), F32) for w in widths],
        compiler_params=pltpu.CompilerParams(dimension_semantics=("arbitrary", "arbitrary"),
                                             vmem_limit_bytes=VMEM_LIMIT),
    )(tok, tok, tok, modx, modc, nw, w_p, mu, rp, lora, gup)


def _row_tile(seq, cap, mult):
    best = mult
    for r in range(mult, min(seq, cap) + 1, mult):
        if seq % r == 0:
            best = r
    return best


def _gd_prep_kernel(n_ctx, seq, rows, zc_ref, zu_ref, zd_ref, cw_ref, o_ref):
    t = pl.program_id(1)
    cw = cw_ref[...]
    ext = jnp.concatenate([zu_ref[0], zc_ref[0], zd_ref[0]], axis=0)
    n_ext = rows + 2 * CHUNK
    pos_e = t * rows - CHUNK + _iota((n_ext, 1), 0)
    col = pos_e & (CHUNK - 1)
    src_ctx = pos_e < n_ctx
    has_l = jnp.where(src_ctx, pos_e, col) >= 1
    has_r = jnp.where(src_ctx, pos_e - (n_ctx - 2), col - (CHUNK - 2)) <= 0
    e_l = jnp.where(has_l, pltpu.roll(ext, 1, 0), 0.0)
    e_r = jnp.where(has_r, pltpu.roll(ext, n_ext - 1, 0), 0.0)

    def taps(di):
        lo = CHUNK * (1 + di)
        i = 3 * (1 + di)
        return (cw[i:i + 1, :] * e_l[lo:lo + rows] + cw[i + 1:i + 2, :] * ext[lo:lo + rows]
                + cw[i + 2:i + 3, :] * e_r[lo:lo + rows])

    pos_o = t * rows + _iota((rows, 1), 0)
    acc = taps(0)
    acc = acc + jnp.where(pos_o - CHUNK >= n_ctx, taps(-1), 0.0)
    acc = acc + jnp.where((pos_o >= n_ctx) & (pos_o + CHUNK < seq), taps(1), 0.0)
    y = _silu(acc)
    q = _l2n_heads(y[:, 0:GROUP_W])
    k = _l2n_heads(y[:, GROUP_W:2 * GROUP_W])
    o_ref[0] = jnp.concatenate([q, k, y[:, 2 * GROUP_W:]], axis=1)


def _gd_prep(z_gd, cw, n_ctx):
    bsz, seq, _ = z_gd.shape
    w = 3 * GROUP_W
    rows = _row_tile(seq, 1088, CHUNK)
    per, n_chunks = rows // CHUNK, seq // CHUNK
    kern = functools.partial(_gd_prep_kernel, n_ctx, seq, rows)
    return pl.pallas_call(
        kern,
        name="gd_prep",
        grid=(bsz, seq // rows),
        in_specs=[pl.BlockSpec((1, rows, w), lambda b, t: (b, t, 0)),
                  pl.BlockSpec((1, CHUNK, w), lambda b, t: (b, jnp.maximum(t * per - 1, 0), 0)),
                  pl.BlockSpec((1, CHUNK, w), lambda b, t: (b, jnp.minimum((t + 1) * per, n_chunks - 1), 0)),
                  pl.BlockSpec(cw.shape, lambda b, t: (0, 0))],
        out_specs=pl.BlockSpec((1, rows, w), lambda b, t: (b, t, 0)),
        out_shape=jax.ShapeDtypeStruct((bsz, seq, w), F32),
        compiler_params=pltpu.CompilerParams(dimension_semantics=("arbitrary", "arbitrary"),
                                             vmem_limit_bytes=VMEM_LIMIT),
    )(z_gd, z_gd, z_gd, cw)


def _head_rms(h, gain):
    return h * lax.rsqrt(_block_sum(h * h) * (1.0 / HEAD_V) + EPS) * gain


def _head_groupnorm(h, gain, bias):
    dlt = h - _block_sum(h) * (1.0 / HEAD_V)
    return dlt * lax.rsqrt(_block_sum(dlt * dlt) * (1.0 / HEAD_V) + GN_EPS) * gain + bias


def _out_mlp_kernel(n_ctx, tm, ff_chunk, final, tok_ref, hml_f, hml_b, hrw_f, hrw_b, hgl_f, hgl_b, hgd_f, hgd_b, oml_ref,
                    prw_ref, ogl_ref, zgd_ref, modx_ref, modc_ref, np_ref, n2_ref, fn_ref, wo_ref, w1_ref, w2_ref, o_ref):
    t = pl.program_id(1)
    x = tok_ref[0]
    is_ctx = (t * tm + _iota((tm, 1), 0)) < n_ctx
    np_ = np_ref[...]
    prw = prw_ref[0]
    mixed = jnp.concatenate([
        _head_rms(hml_f[0] + hml_b[0], np_[0:1, :]) * _sigmoid(oml_ref[0]),
        (_head_groupnorm(hrw_f[0] + hrw_b[0], np_[1:2, :], np_[2:3, :]) + prw[:, GROUP_W:]) * prw[:, :GROUP_W],
        _head_rms(hgl_f[0] + hgl_b[0], np_[3:4, :]) * _silu(ogl_ref[0]),
        _head_rms(hgd_f[0] + hgd_b[0], np_[4:5, :]) * _silu(zgd_ref[0]),
    ], axis=1)
    x = x + _pick_mod(modx_ref, modc_ref, is_ctx, 2) * _mm(_bf(mixed), wo_ref[...])
    shift = _pick_mod(modx_ref, modc_ref, is_ctx, 3)
    scale = _pick_mod(modx_ref, modc_ref, is_ctx, 4)
    h = _bf(_rms_rows(x, n2_ref[...]) * (1.0 + scale) + shift)
    d_ff = w1_ref.shape[1]
    acc = jnp.zeros(x.shape, F32)
    for c0 in range(0, d_ff, ff_chunk):
        a = jnp.maximum(_mm(h, w1_ref[:, c0:c0 + ff_chunk]), 0.0)
        acc = acc + _mm(_bf(a * a), w2_ref[c0:c0 + ff_chunk, :])
    x = x + _pick_mod(modx_ref, modc_ref, is_ctx, 5) * acc
    if final:
        x = _rms_rows(x, fn_ref[...])
    o_ref[0] = x


def _out_mlp(tok, hs, z_ml, rw_post, z_gl, z_gd, modx, modc, normp, n2, fnw, wo, w1, w2,
             n_ctx, tm, final):
    bsz, seq, d_model = tok.shape
    kern = functools.partial(_out_mlp_kernel, n_ctx, tm, 1024, final)
    row = lambda w, j=0: pl.BlockSpec((1, tm, w), lambda b, t, j=j: (b, t, j))
    const = lambda a: pl.BlockSpec(a.shape, lambda b, t, nd=a.ndim: (0,) * nd)
    wconst = lambda a: pl.BlockSpec(a.shape, lambda b, t: (0, 0), pipeline_mode=pl.Buffered(1))
    return pl.pallas_call(
        kern,
        name="out_mlp",
        grid=(bsz, seq // tm),
        in_specs=[row(d_model)] + [row(GROUP_W)] * len(hs) + [
                  row(GROUP_W, 2), row(2 * GROUP_W), row(GROUP_W, 2), row(GROUP_W, 3),
                  pl.BlockSpec((1, 8, d_model), lambda b, t: (b, 0, 0)), const(modc), const(normp), const(n2),
                  const(fnw), wconst(wo), wconst(w1), wconst(w2)],
        out_specs=row(d_model),
        out_shape=jax.ShapeDtypeStruct((bsz, seq, d_model), F32),
        compiler_params=pltpu.CompilerParams(dimension_semantics=("arbitrary", "arbitrary"),
                                             vmem_limit_bytes=VMEM_LIMIT),
    )(tok, *hs, z_ml, rw_post, z_gl, z_gd, modx, modc, normp, n2, fnw, wo, w1, w2)


def _pad_rows(a, rows=8):
    return jnp.concatenate([a, jnp.zeros((rows - a.shape[0],) + a.shape[1:], a.dtype)], axis=0)


def _misc_row(pairs):
    row = jnp.zeros((MISC_W,), F32)
    for col0, p in pairs:
        row = row.at[col0:col0 + N_DIR * HEADS].set(p.reshape(-1))
    return row


def _permute_w_in(w):
    d_model = w.shape[0]
    ml_w, rw_w, gl_w = 784, 1024, 784
    o_ml, o_rw, o_gl, o_gd = 0, ml_w, ml_w + rw_w, ml_w + rw_w + gl_w
    misc = jnp.concatenate([
        w[:, o_ml + 768:o_ml + 784],
        w[:, o_gl + 512:o_gl + 528],
        w[:, o_gd + 1024:o_gd + 1040],
        jnp.zeros((d_model, MISC_W - 48), w.dtype)], axis=1)
    return jnp.concatenate([
        w[:, o_ml:o_ml + 768],
        w[:, o_rw:o_rw + 1024],
        w[:, o_gl:o_gl + 512], w[:, o_gl + 528:o_gl + 784],
        w[:, o_gd:o_gd + 1024],
        misc], axis=1)


Z_WIDTHS = (768, 1024, 768, 1024, MISC_W)


SCAN_NB = 8
SCAN_GROUPS = (('ml', 'rw'), ('gl', 'gd'))


def _mix_layer(z_ml, base, d0, d1, z_gl, z_gd, z_misc, p, n_ctx):
    nb = SCAN_NB if z_ml.shape[0] % SCAN_NB == 0 else 1
    qkv = _gd_prep(z_gd, p['gd_cw'], n_ctx)
    mixers = {
        'ml': _Mixer(_ml_chunk, [z_ml, z_misc], [z_ml, z_misc], [p['ml_gb']],
                     (((QK_W, 2 * GROUP_W), 0.0), ((1, GROUP_W), M_INIT))),
        'rw': _Mixer(_rw_chunk, [base, d0], [base, d1], [p['rw_rp']], (((GROUP_W, GROUP_W), 0.0),)),
        'gl': _Mixer(_gl_chunk, [z_gl, z_misc], [z_gl, z_misc], [p['gl_gup'], p['gl_gb']], (((GROUP_W, QK_W), 0.0),)),
        'gd': _Mixer(_gd_chunk, [qkv, z_misc], [qkv, z_misc], [p['gd_gp']], (((GROUP_W, GROUP_W), 0.0),)),
    }
    h = {}
    for group in SCAN_GROUPS:
        for name, out in zip(group, _bidir_scan([mixers[g] for g in group], n_ctx, nb)):
            h[name] = out
    return (*h['ml'], *h['rw'], *h['gl'], *h['gd'])


def _layer_params(layer, w_in, ml_ig_b, ml_fg_b, ml_norm_w, rw_mu_prev, rw_mu_next, rw_w0, rw_w_up, rw_a0,
                  rw_a_up, rw_g_up, rw_k_k, rw_k_a, rw_r_k, rw_gn_w, rw_gn_b, gl_gate_up, gl_gate_b, gl_norm_w,
                  gd_conv_w, gd_a_log, gd_dt_bias, gd_norm_w):
    l = layer
    z64 = jnp.zeros((64, GROUP_W), F32)
    lora = jnp.concatenate([
        jnp.concatenate([rw_w_up[l, 0], z64, rw_w_up[l, 1], z64], axis=1),
        jnp.concatenate([z64, rw_a_up[l, 0], z64, rw_a_up[l, 1]], axis=1)], axis=0)
    gup = jnp.zeros((N_DIR, MISC_W, QK_W), F32).at[:, MISC_GL_AL:MISC_GL_AL + 16, :].set(gl_gate_up[l])
    return {
        'w_in': _bf(_permute_w_in(w_in[l])),
        'ml_gb': _pad_rows(_misc_row([(MISC_ML_IG, ml_ig_b[l]), (MISC_ML_FG, ml_fg_b[l])])[None, :]),
        'gl_gup': gup,
        'gl_gb': _pad_rows(gl_gate_b[l]),
        'gd_cw': _pad_rows(gd_conv_w[l].reshape(9, 3 * GROUP_W), 16),
        'gd_gp': _pad_rows(jnp.stack([_misc_row([(MISC_GD_AL, gd_dt_bias[l])]), _misc_row([(MISC_GD_AL, gd_a_log[l])])])),
        'rw_mu': _pad_rows(jnp.stack([rw_mu_prev[l], rw_mu_next[l]])),
        'rw_rp': _pad_rows(jnp.stack([rw_w0[l, 0], rw_w0[l, 1], rw_a0[l, 0], rw_a0[l, 1],
                                      rw_k_k[l], rw_k_a[l], rw_r_k[l]])),
        'rw_lora': lora,
        'rw_gup': rw_g_up[l],
        'normp': _pad_rows(jnp.stack([ml_norm_w[l], rw_gn_w[l], rw_gn_b[l], gl_norm_w[l], gd_norm_w[l]])),
    }


def kernel(x, c, ctx, c_ctx, ada_w, ada_b, norm1_w, norm2_w, w_in, w_out, ml_ig_b, ml_fg_b, ml_norm_w, rw_mu_prev, rw_mu_next, rw_w0, rw_w_up, rw_a0, rw_a_up, rw_g_up, rw_k_k, rw_k_a, rw_r_k, rw_gn_w, rw_gn_b, gl_gate_up, gl_gate_b, gl_norm_w, gd_conv_w, gd_a_log, gd_dt_bias, gd_norm_w, mlp_w1, mlp_w2, final_norm_w):
    bsz, seq_x, d_model = x.shape
    n_ctx = ctx.shape[1]
    depth = w_in.shape[0]
    seq = n_ctx + seq_x
    tm = seq // 8 if seq % 64 == 0 else seq
    tok = jnp.concatenate([ctx, x], axis=1)
    cc = _pad_rows(jnp.concatenate([c, c_ctx[None, :]], axis=0), ((bsz + 1 + 7) // 8) * 8)
    mod = _ada_mod(cc, ada_w, ada_b)
    fnw = final_norm_w[None, :]
    for layer in range(depth):
        p = _layer_params(layer, w_in, ml_ig_b, ml_fg_b, ml_norm_w, rw_mu_prev, rw_mu_next, rw_w0, rw_w_up, rw_a0,
                          rw_a_up, rw_g_up, rw_k_k, rw_k_a, rw_r_k, rw_gn_w, rw_gn_b, gl_gate_up, gl_gate_b,
                          gl_norm_w, gd_conv_w, gd_a_log, gd_dt_bias, gd_norm_w)
        modx = jnp.pad(mod[layer, :bsz].reshape(bsz, 6, d_model), ((0, 0), (0, 2), (0, 0)))
        modc = _pad_rows(mod[layer, bsz].reshape(6, d_model))
        z_ml, base, d0, d1, rw_post, z_gl, z_gd, z_misc = _in_proj(
            tok, modx, modc, norm1_w[layer][None, :], p['w_in'], p['rw_mu'], p['rw_rp'], p['rw_lora'], p['rw_gup'],
            n_ctx, tm)
        hs = _mix_layer(z_ml, base, d0, d1, z_gl, z_gd, z_misc, p, n_ctx)
        tok = _out_mlp(tok, hs, z_ml, rw_post, z_gl, z_gd, modx, modc, p['normp'],
                       norm2_w[layer][None, :], fnw, _bf(w_out[layer]), _bf(mlp_w1[layer]), _bf(mlp_w2[layer]),
                       n_ctx, tm, layer == depth - 1)
    return tok[:, n_ctx:]
```

```python
import functools

import jax
import jax.numpy as jnp
from jax import lax
from jax.experimental import pallas as pl
from jax.experimental.pallas import tpu as pltpu

F32 = jnp.float32
BF16 = jnp.bfloat16

CHUNK = 64
HEADS = 4
HEAD_V = 64
HEAD_QK = 32
GROUP_W = HEADS * HEAD_V
QK_W = HEADS * HEAD_QK
N_DIR = 2
EPS = 1e-6
GN_EPS = 64e-5
M_INIT = -1e30
RW_DECAY_SCALE = 0.6065306597126334
GLA_LOGIT_NORM = 16.0
MISC_W = 128
MISC_ML_IG, MISC_ML_FG, MISC_GL_AL, MISC_GD_BL, MISC_GD_AL = 0, 8, 16, 32, 40
VMEM_LIMIT = 56 * 1024 * 1024


def _mm(a, b, prec=None):
    return lax.dot_general(a, b, (((1,), (0,)), ((), ())), precision=prec, preferred_element_type=F32)


def _mm_nt(a, b, prec=None):
    return lax.dot_general(a, b, (((1,), (1,)), ((), ())), precision=prec, preferred_element_type=F32)


def _mm_tn(a, b, prec=None):
    return lax.dot_general(a, b, (((0,), (0,)), ((), ())), precision=prec, preferred_element_type=F32)


def _bf(x):
    return x.astype(BF16)


def _iota(shape, dim):
    return lax.broadcasted_iota(jnp.int32, shape, dim)


def _sigmoid(x):
    return 1.0 / (1.0 + jnp.exp(-x))


def _softplus(x):
    return jnp.maximum(x, 0.0) + jnp.log1p(jnp.exp(-jnp.abs(x)))


def _log_sigmoid(x):
    return -_softplus(-x)


def _silu(x):
    return x * _sigmoid(x)


def _time_mask(d, width, strict):
    i = _iota((CHUNK, width), 0)
    j = _iota((CHUNK, width), 1) & (CHUNK - 1)
    if d == 0:
        return (j < i) if strict else (j <= i)
    return (j > i) if strict else (j >= i)


def _cumsum_mat(d):
    return jnp.where(_time_mask(d, CHUNK, False), 1.0, 0.0).astype(F32)


def _eye_cat(width):
    i = _iota((CHUNK, width), 0)
    j = _iota((CHUNK, width), 1) & (CHUNK - 1)
    return jnp.where(i == j, 1.0, 0.0).astype(F32)


def _bd_mask(rows, cols, row_shift, col_shift, col_and=None):
    r = _iota((rows, cols), 0) >> row_shift
    c = _iota((rows, cols), 1)
    if col_and is not None:
        c = c & col_and
    return r == (c >> col_shift)


def _stack_bd(x, col_shift):
    w = x.shape[1]
    m = _bd_mask(HEADS * CHUNK, w, 6, col_shift)
    return _bf(jnp.where(m, jnp.concatenate([x.astype(F32)] * HEADS, axis=0), 0.0))


def _split(x, n):
    out, r = [], x
    for i in range(n):
        p = r.astype(BF16)
        out.append(p)
        if i + 1 < n:
            r = r - p.astype(F32)
    return out


def _sel_left(t, x, n=2):
    return _mm(jnp.concatenate([_bf(t)] * n, axis=1), jnp.concatenate(_split(x, n), axis=0))


def _sel_right(x, e, n=2):
    return _mm(jnp.concatenate(_split(x, n), axis=1), jnp.concatenate([_bf(e)] * n, axis=0))


def _mm3(a, b):
    ah, al = _split(a, 2)
    bh, bl = _split(b, 2)
    return _mm(jnp.concatenate([ah, ah, al], axis=1), jnp.concatenate([bh, bl, bh], axis=0))


PAIR_W = 2 * HEAD_V


def _bd2(x):
    lo = _iota(x.shape, 1) < HEAD_V
    x = x.astype(F32)
    return _bf(jnp.concatenate([jnp.where(lo, x, 0.0), jnp.where(lo, 0.0, x)], axis=0))


def _bd_mm(lhs, rhs, nt=False):
    outs = []
    for g in range(GROUP_W // PAIR_W):
        sl = slice(g * PAIR_W, (g + 1) * PAIR_W)
        l = jnp.concatenate([_bf(a)[:, sl] for a in lhs], axis=1)
        r = jnp.concatenate([_bd2(b[:, sl]) for b in rhs], axis=1 if nt else 0)
        outs.append(_mm_nt(l, r) if nt else _mm(l, r))
    return jnp.concatenate(outs, axis=1)


def _cumsum_t(d, x):
    return _sel_left(_cumsum_mat(d), x)


def _expand_mat(col0s):
    r = _iota((MISC_W, GROUP_W), 0)
    hcol = _iota((MISC_W, GROUP_W), 1) >> 6
    return jnp.concatenate([jnp.where(r == hcol + c0, 1.0, 0.0) for c0 in col0s], axis=1).astype(F32)


def _to_cols(x):
    return _sel_left(jnp.ones((CHUNK, CHUNK), F32), x * _eye_cat(GROUP_W))


def _block_sum(x):
    w = x.shape[1]
    ones_bd = jnp.where(_bd_mask(w, w, 6, 6), 1.0, 0.0).astype(F32)
    return _sel_right(x, ones_bd, 2)


def _block_max(x):
    blk = _iota(x.shape, 1) >> 6
    out = jnp.zeros_like(x)
    for h in range(HEADS):
        sel = blk == h
        mh = jnp.max(jnp.where(sel, x, -jnp.inf), axis=1, keepdims=True)
        out = jnp.where(sel, mh, out)
    return out


def _inv_unit(n):
    i = _iota((CHUNK, GROUP_W), 0)
    j = _iota((CHUNK, GROUP_W), 1) & (CHUNK - 1)
    x = _eye_cat(GROUP_W)
    for lvl in range(6):
        c = jnp.where(((i >> lvl) ^ (j >> lvl)) == 1, n, 0.0)
        if lvl == 0:
            x = x + c
        else:
            y = _bd_mm([c], [x])
            yield
            x = x + _bd_mm([x], [y])
            yield
    return x


def _l2n_heads(t):
    return t * lax.rsqrt(_block_sum(t * t) + EPS)


def _ml_chunk(d, xs, cs, st):
    z, misc = xs
    (gb,) = cs
    s_mat, m_row = st
    last = CHUNK - 1 if d == 0 else 0
    q = z[:, 0:QK_W] * (HEAD_QK ** -0.5)
    k = z[:, QK_W:2 * QK_W]
    v = z[:, 2 * QK_W:2 * QK_W + GROUP_W]
    t = misc + gb[0:1, :]
    is_fg = (_iota(misc.shape, 1) >> 3) == (MISC_ML_FG >> 3)
    y = jnp.where(is_fg, _log_sigmoid(t), t)
    cum = _cumsum_t(d, y)
    a = _mm_nt(_bf(q), _stack_bd(k, 5))
    inter = _mm(_bf(q), _bf(s_mat))
    yield
    ex = _sel_right(jnp.where(is_fg, cum, y), _expand_mat((MISC_ML_IG + HEADS * d, MISC_ML_FG + HEADS * d)))
    yield
    ig, bcum = ex[:, :GROUP_W], ex[:, GROUP_W:]
    b_end = bcum[last:last + 1, :]
    w_end = b_end - bcum + ig
    m_loc = jnp.max(w_end, axis=0, keepdims=True)
    e_end = jnp.exp(w_end - m_loc)
    cols = _to_cols(ig - bcum)
    loc = _mm_tn(_bf(k), _bf(jnp.concatenate([v * e_end, e_end], axis=1)))
    yield
    loc = jnp.where(_bd_mask(QK_W, 2 * GROUP_W, 5, 6, GROUP_W - 1), loc, 0.0)
    m_prev_s = b_end + m_row
    m_new = jnp.maximum(m_prev_s, m_loc)
    f_s = jnp.exp(m_prev_s - m_new)
    f_l = jnp.exp(m_loc - m_new)
    s_new = jnp.concatenate([f_s, f_s], axis=1) * s_mat + jnp.concatenate([f_l, f_l], axis=1) * loc
    logd = jnp.where(_time_mask(d, GROUP_W, False), bcum + cols, -jnp.inf)
    m_prev = bcum + m_row
    m_i = jnp.maximum(m_prev, _block_max(logd))
    sm = a * jnp.exp(logd - m_i)
    e_prev = jnp.exp(m_prev - m_i)
    num = _bd_mm([sm], [v])
    den = _bd_mm([sm], [jnp.ones((CHUNK, GROUP_W), F32)])
    yield
    num = num + e_prev * inter[:, :GROUP_W]
    den = den + e_prev * inter[:, GROUP_W:]
    h = num / jnp.maximum(jnp.abs(den), jnp.exp(-m_i))
    return h, (s_new, m_new)


def _gl_chunk(d, xs, cs, st):
    z, misc = xs
    gup, gb = cs
    (s_mat,) = st
    last = CHUNK - 1 if d == 0 else 0
    mid = CHUNK // 2 if d == 0 else CHUNK - 1 - CHUNK // 2
    q = z[:, 0:QK_W] * (HEAD_QK ** -0.5)
    k = z[:, QK_W:2 * QK_W]
    v = z[:, 2 * QK_W:2 * QK_W + GROUP_W]
    pre = _mm3(misc, gup[d])
    yield
    la = _log_sigmoid(pre + gb[d:d + 1, :]) * (1.0 / GLA_LOGIT_NORM)
    g = _cumsum_t(d, la)
    yield
    g_end = g[last:last + 1, :]
    g_mid = g[mid:mid + 1, :]
    qd = q * jnp.exp(g - g_mid)
    kd = k * jnp.exp(g_mid - g)
    att = _mm_nt(_bf(qd), _stack_bd(kd, 5))
    s_loc = _mm_tn(_bf(v), _bf(k * jnp.exp(g_end - g)))
    o_inter = _mm_nt(_bf(q * jnp.exp(g)), _bf(s_mat))
    yield
    att = jnp.where(_time_mask(d, GROUP_W, False), att, 0.0)
    s_new = jnp.exp(g_end) * s_mat + jnp.where(_bd_mask(GROUP_W, QK_W, 6, 5), s_loc, 0.0)
    o_intra = _bd_mm([att], [v])
    yield
    return o_inter + o_intra, (s_new,)


def _gd_chunk(d, xs, cs, st):
    qkv, misc = xs
    (gp,) = cs
    (s_mat,) = st
    last = CHUNK - 1 if d == 0 else 0
    q = qkv[:, 0:GROUP_W] * (HEAD_V ** -0.5)
    k = qkv[:, GROUP_W:2 * GROUP_W]
    v = qkv[:, 2 * GROUP_W:3 * GROUP_W]
    t = misc + gp[0:1, :]
    is_al = (_iota(misc.shape, 1) >> 3) == (MISC_GD_AL >> 3)
    y = jnp.where(is_al, -jnp.exp(gp[1:2, :]) * _softplus(t), _sigmoid(t))
    cum = _cumsum_t(d, y)
    x2 = _bd_mm([jnp.concatenate([k, q], axis=0)], [k], nt=True)
    yield
    ex = _sel_right(jnp.where(is_al, cum, y), _expand_mat((MISC_GD_BL + HEADS * d, MISC_GD_AL + HEADS * d)))
    yield
    beta, g = ex[:, :GROUP_W], ex[:, GROUP_W:]
    g_end = g[last:last + 1, :]
    cols = _to_cols(g)
    yield
    dec0 = jnp.where(_time_mask(d, GROUP_W, False), jnp.exp(g - cols), 0.0)
    kk, qk = x2[:CHUNK], x2[CHUNK:]
    mm = jnp.where(_time_mask(d, GROUP_W, True), beta * kk * dec0, 0.0)
    ainv = yield from _inv_unit(-mm)
    eg = jnp.exp(g)
    u = _bd_mm([ainv], [v * beta])
    w = _bd_mm([ainv], [k * (beta * eg)])
    yield
    ws = _mm(_bf(jnp.concatenate([w, q * eg], axis=0)), _bf(s_mat))
    yield
    v_new = u - ws[:CHUNK]
    o_inter = ws[CHUNK:]
    k_end = k * jnp.exp(g_end - g)
    upd = _mm_tn(_bf(k_end), _bf(v_new))
    o_intra = _bd_mm([qk * dec0], [v_new])
    yield
    s_new = jnp.exp(g_end) * s_mat + jnp.where(_bd_mask(GROUP_W, GROUP_W, 6, 6), upd, 0.0)
    return o_inter + o_intra, (s_new,)


def _rw_chunk(d, xs, cs, st):
    base, dirp = xs
    (rp,) = cs
    (s_mat,) = st
    last = CHUNK - 1 if d == 0 else 0
    r = base[:, 0:GROUP_W]
    k = base[:, GROUP_W:2 * GROUP_W]
    v = base[:, 2 * GROUP_W:3 * GROUP_W]
    kk = base[:, 3 * GROUP_W:4 * GROUP_W]
    lw = dirp[:, 0:GROUP_W]
    a = dirp[:, GROUP_W:2 * GROUP_W]
    k_a = rp[5:6, :]
    kt = k * (1.0 + (a - 1.0) * k_a)
    ap = kk * a
    g = _cumsum_t(d, lw)
    yield
    g_end = g[last:last + 1, :]
    eg = jnp.exp(g)
    eng = jnp.exp(-g)
    r_h = r * eg
    a_h = ap * eng
    k_h = kt * eng
    b_h = -kk * jnp.exp(g - lw)
    br = jnp.concatenate([b_h, r_h], axis=0)
    xa = _bd_mm([br], [a_h], nt=True)
    xk = _bd_mm([br], [k_h], nt=True)
    yield
    tm1 = _time_mask(d, GROUP_W, True)
    tm0 = _time_mask(d, GROUP_W, False)
    m_ab = jnp.where(tm1, xa[:CHUNK], 0.0)
    m_bk = jnp.where(tm1, xk[:CHUNK], 0.0)
    att_a = jnp.where(tm0, xa[CHUNK:], 0.0)
    att_k = jnp.where(tm0, xk[CHUNK:], 0.0)
    rhs2 = _bd_mm([m_bk], [v])
    ainv = yield from _inv_unit(m_ab)
    w_mat = _bd_mm([ainv], [b_h])
    u0 = _bd_mm([ainv], [rhs2])
    yield
    ws = _mm_nt(_bf(jnp.concatenate([w_mat, r_h], axis=0)), _bf(s_mat))
    yield
    u = u0 + ws[:CHUNK]
    y_inter = ws[CHUNK:]
    dec_end = jnp.exp(g_end - g)
    upd = _mm_tn(_bf(jnp.concatenate([u, v], axis=0)), _bf(jnp.concatenate([ap * dec_end, kt * dec_end], axis=0)))
    y_intra = _bd_mm([att_a, att_k], [u, v])
    yield
    s_new = jnp.exp(g_end) * s_mat + jnp.where(_bd_mask(GROUP_W, GROUP_W, 6, 6), upd, 0.0)
    return y_inter + y_intra, (s_new,)


class _Mixer:
    def __init__(self, chunk_fn, ins_f, ins_b, csts, state_init):
        self.chunk_fn, self.ins_f, self.ins_b, self.csts, self.state_init = chunk_fn, ins_f, ins_b, csts, state_init


def _scan_kernel(layout, nb, *refs):
    pos = 0
    parts = []
    for chunk_fn, n_in, n_cst, state_init in layout:
        ins = (refs[pos:pos + n_in], refs[pos + n_in:pos + 2 * n_in])
        csts = refs[pos + 2 * n_in:pos + 2 * n_in + n_cst]
        pos += 2 * n_in + n_cst
        parts.append([chunk_fn, ins, csts, state_init])
    for p in parts:
        p.append(refs[pos:pos + N_DIR])
        pos += N_DIR
    for p in parts:
        p.append(refs[pos:pos + len(p[3])])
        pos += len(p[3])

    @pl.when(pl.program_id(1) == 0)
    def _():
        for _, _, _, state_init, _, st_refs in parts:
            for i, (shape, val) in enumerate(state_init):
                st_refs[i][...] = jnp.full((N_DIR * nb,) + shape, val, F32)

    chains, gens = [], []
    for bi in range(nb):
        for pi, (chunk_fn, ins, csts, state_init, _, st_refs) in enumerate(parts):
            cs = tuple(c[...] for c in csts)
            for d in range(N_DIR):
                xs = tuple(r[bi] for r in ins[d])
                st = tuple(st_refs[i][d * nb + bi] for i in range(len(state_init)))
                chains.append((pi, bi, d))
                gens.append(chunk_fn(d, xs, cs, st))
    results = [None] * len(chains)
    while any(r is None for r in results):
        for ci, gen in enumerate(gens):
            if results[ci] is None:
                try:
                    next(gen)
                except StopIteration as stop:
                    results[ci] = stop.value
    for (pi, bi, d), (y, st_new) in zip(chains, results):
        o_refs, st_refs = parts[pi][4], parts[pi][5]
        for i, v in enumerate(st_new):
            st_refs[i][d * nb + bi] = v
        o_refs[d][bi] = y


def _bidir_scan(mixers, n_ctx, nb):
    bsz, seq = mixers[0].ins_f[0].shape[0], mixers[0].ins_f[0].shape[1]
    n_chunks, n_ctx_chunks = seq // CHUNK, n_ctx // CHUNK

    def idx_f(b, s):
        return (b, s, 0)

    def idx_b(b, s):
        return (b, jnp.where(s < n_ctx_chunks, n_ctx_chunks - 1 - s, n_chunks - 1 + n_ctx_chunks - s), 0)

    in_specs, args, layout, scratch = [], [], [], []
    for m in mixers:
        in_specs += [pl.BlockSpec((nb, CHUNK, a.shape[2]), idx_f) for a in m.ins_f]
        in_specs += [pl.BlockSpec((nb, CHUNK, a.shape[2]), idx_b) for a in m.ins_b]
        in_specs += [pl.BlockSpec(c.shape, lambda b, s, nd=c.ndim: (0,) * nd) for c in m.csts]
        args += [*m.ins_f, *m.ins_b, *m.csts]
        layout.append((m.chunk_fn, len(m.ins_f), len(m.csts), m.state_init))
        scratch += [pltpu.VMEM((N_DIR * nb,) + shape, F32) for (shape, _) in m.state_init]
    outs = pl.pallas_call(
        functools.partial(_scan_kernel, tuple(layout), nb),
        name="scan_" + "_".join(m.chunk_fn.__name__.strip("_").split("_")[0] for m in mixers),
        grid=(bsz // nb, n_chunks),
        in_specs=in_specs,
        out_specs=[pl.BlockSpec((nb, CHUNK, GROUP_W), idx) for _ in mixers for idx in (idx_f, idx_b)],
        out_shape=[jax.ShapeDtypeStruct((bsz, seq, GROUP_W), F32)] * (N_DIR * len(mixers)),
        scratch_shapes=scratch,
        compiler_params=pltpu.CompilerParams(dimension_semantics=("arbitrary", "arbitrary"),
                                             vmem_limit_bytes=VMEM_LIMIT),
    )(*args)
    return [tuple(outs[N_DIR * i:N_DIR * (i + 1)]) for i in range(len(mixers))]


def _ada_kernel(c_ref, w_ref, b_ref, o_ref):
    o_ref[0] = _mm(_bf(_silu(c_ref[...])), _bf(w_ref[0])) + b_ref[0]


def _ada_mod(cc, ada_w, ada_b):
    depth, d_model, n6 = ada_w.shape
    tn = 1536
    return pl.pallas_call(
        _ada_kernel,
        name="ada_mod",
        grid=(depth, n6 // tn),
        in_specs=[pl.BlockSpec(cc.shape, lambda l, j: (0, 0)),
                  pl.BlockSpec((1, d_model, tn), lambda l, j: (l, 0, j)),
                  pl.BlockSpec((1, 1, tn), lambda l, j: (l, 0, j))],
        out_specs=pl.BlockSpec((1, cc.shape[0], tn), lambda l, j: (l, 0, j)),
        out_shape=jax.ShapeDtypeStruct((depth, cc.shape[0], n6), F32),
        compiler_params=pltpu.CompilerParams(dimension_semantics=("arbitrary", "arbitrary"),
                                             vmem_limit_bytes=VMEM_LIMIT),
    )(cc, ada_w, ada_b.reshape(depth, 1, n6))


def _rms_rows(x, gain):
    return x * lax.rsqrt(jnp.mean(x * x, axis=-1, keepdims=True) + EPS) * gain


def _pick_mod(modx_ref, modc_ref, is_ctx, row):
    return jnp.where(is_ctx, modc_ref[row:row + 1, :], modx_ref[0, row:row + 1, :])


def _modulated_rows(x, pos, n_ctx, modx_ref, modc_ref, nw):
    is_ctx = pos < n_ctx
    shift = _pick_mod(modx_ref, modc_ref, is_ctx, 0)
    scale = _pick_mod(modx_ref, modc_ref, is_ctx, 1)
    return _bf(_rms_rows(x, nw) * (1.0 + scale) + shift)


def _rw_token_prep(z, z_before, z_after, pos, n_ctx, seq, mu, rp, lora, gup):
    rows = z.shape[0]
    row = _iota((rows, 1), 0)
    z_prev = jnp.where(row == 0, z_before, pltpu.roll(z, 1, 0))
    z_prev = jnp.where((pos != 0) & (pos != n_ctx), z_prev, 0.0)
    z_next = jnp.where(row == rows - 1, z_after, pltpu.roll(z, rows - 1, 0))
    z_next = jnp.where((pos != seq - 1) & (pos != n_ctx - 1), z_next, 0.0)
    zm = z + mu[0:1, :] * (z_prev - z) + mu[1:2, :] * (z_next - z)
    yield
    r = zm[:, 0:GROUP_W]
    k = zm[:, GROUP_W:2 * GROUP_W]
    v = zm[:, 2 * GROUP_W:3 * GROUP_W]
    lo = zm[:, 3 * GROUP_W:3 * GROUP_W + 128]
    gl = zm[:, 3 * GROUP_W + 128:4 * GROUP_W]
    lo = jnp.where(_iota(lo.shape, 1) < 64, jnp.tanh(lo), lo)
    pre = _mm3(lo, lora)
    yield
    kk = _l2n_heads(k * rp[4:5, :])
    gate = _mm(_bf(_sigmoid(gl)), _bf(gup))
    yield
    lw0 = -RW_DECAY_SCALE * _sigmoid(rp[0:1, :] + pre[:, 0:GROUP_W])
    a0 = _sigmoid(rp[2:3, :] + pre[:, GROUP_W:2 * GROUP_W])
    lw1 = -RW_DECAY_SCALE * _sigmoid(rp[1:2, :] + pre[:, 2 * GROUP_W:3 * GROUP_W])
    a1 = _sigmoid(rp[3:4, :] + pre[:, 3 * GROUP_W:4 * GROUP_W])
    k_a = rp[5:6, :]
    k_sum = k * (1.0 + (a0 - 1.0) * k_a) + k * (1.0 + (a1 - 1.0) * k_a)
    bonus = _block_sum(r * k_sum * rp[6:7, :]) * v
    yield
    return (jnp.concatenate([r, k, v, kk], axis=1), jnp.concatenate([lw0, a0], axis=1),
            jnp.concatenate([lw1, a1], axis=1), jnp.concatenate([gate, bonus], axis=1))


def _in_proj_kernel(n_ctx, seq, tm, cols, tok_ref, tokp_ref, tokn_ref, modx_ref, modc_ref, nw_ref, w_ref, mu_ref,
                    rp_ref, lora_ref, gup_ref, zml_ref, base_ref, d0_ref, d1_ref, post_ref, zgl_ref, zgd_ref,
                    misc_ref):
    t = pl.program_id(1)
    nw = nw_ref[...]
    pos = t * tm + _iota((tm, 1), 0)
    pos8 = _iota((8, 1), 0)
    h = _modulated_rows(tok_ref[0], pos, n_ctx, modx_ref, modc_ref, nw)
    c0, c1 = cols['rw']
    z_rw = _mm(h, w_ref[:, c0:c1])
    h_before = _modulated_rows(tokp_ref[0], t * tm - 8 + pos8, n_ctx, modx_ref, modc_ref, nw)
    h_after = _modulated_rows(tokn_ref[0], (t + 1) * tm + pos8, n_ctx, modx_ref, modc_ref, nw)
    z_before = _mm(h_before, w_ref[:, c0:c1])[7:8, :]
    z_after = _mm(h_after, w_ref[:, c0:c1])[0:1, :]
    prep = _rw_token_prep(z_rw, z_before, z_after, pos, n_ctx, seq, mu_ref[...], rp_ref[...], lora_ref[...],
                          gup_ref[...])
    rw_out = None
    for name, o_ref in (('ml', zml_ref), ('gl', zgl_ref), ('gd', zgd_ref), ('misc', misc_ref)):
        c0, c1 = cols[name]
        o_ref[0] = _mm(h, w_ref[:, c0:c1])
        if rw_out is None:
            try:
                next(prep)
            except StopIteration as stop:
                rw_out = stop.value
    while rw_out is None:
        try:
            next(prep)
        except StopIteration as stop:
            rw_out = stop.value
    base_ref[0], d0_ref[0], d1_ref[0], post_ref[0] = rw_out


def _in_proj(tok, modx, modc, nw, w_p, mu, rp, lora, gup, n_ctx, tm):
    bsz, seq, d_model = tok.shape
    cols, c0 = {}, 0
    for name, w in zip(('ml', 'rw', 'gl', 'gd', 'misc'), Z_WIDTHS):
        cols[name] = (c0, c0 + w)
        c0 += w
    widths = (Z_WIDTHS[0], 4 * GROUP_W, 2 * GROUP_W, 2 * GROUP_W, 2 * GROUP_W, Z_WIDTHS[2], Z_WIDTHS[3], Z_WIDTHS[4])
    sub, n_sub = tm // 8, seq // 8
    const = lambda a: pl.BlockSpec(a.shape, lambda b, t, nd=a.ndim: (0,) * nd)
    kern = functools.partial(_in_proj_kernel, n_ctx, seq, tm, cols)
    return pl.pallas_call(
        kern,
        name="in_proj",
        grid=(bsz, seq // tm),
        in_specs=[pl.BlockSpec((1, tm, d_model), lambda b, t: (b, t, 0)),
                  pl.BlockSpec((1, 8, d_model), lambda b, t: (b, jnp.maximum(t * sub - 1, 0), 0)),
                  pl.BlockSpec((1, 8, d_model), lambda b, t: (b, jnp.minimum((t + 1) * sub, n_sub - 1), 0)),
                  pl.BlockSpec((1, 8, d_model), lambda b, t: (b, 0, 0)),
                  const(modc), const(nw),
                  pl.BlockSpec(w_p.shape, lambda b, t: (0, 0), pipeline_mode=pl.Buffered(1)),
                  const(mu), const(rp), const(lora), const(gup)],
        out_specs=[pl.BlockSpec((1, tm, w), lambda b, t: (b, t, 0)) for w in widths],
        out_shape=[jax.ShapeDtypeStruct((bsz, seq, w), F32) for w in widths],
        compiler_params=pltpu.CompilerParams(dimension_semantics=("arbitrary", "arbitrary"),
                                             vmem_limit_bytes=VMEM_LIMIT),
    )(tok, tok, tok, modx, modc, nw, w_p, mu, rp, lora, gup)


def _row_tile(seq, cap, mult):
    best = mult
    for r in range(mult, min(seq, cap) + 1, mult):
        if seq % r == 0:
            best = r
    return best


def _gd_prep_kernel(n_ctx, seq, rows, zc_ref, zu_ref, zd_ref, cw_ref, o_ref):
    t = pl.program_id(1)
    cw = cw_ref[...]
    ext = jnp.concatenate([zu_ref[0], zc_ref[0], zd_ref[0]], axis=0)
    n_ext = rows + 2 * CHUNK
    pos_e = t * rows - CHUNK + _iota((n_ext, 1), 0)
    col = pos_e & (CHUNK - 1)
    src_ctx = pos_e < n_ctx
    has_l = jnp.where(src_ctx, pos_e, col) >= 1
    has_r = jnp.where(src_ctx, pos_e - (n_ctx - 2), col - (CHUNK - 2)) <= 0
    e_l = jnp.where(has_l, pltpu.roll(ext, 1, 0), 0.0)
    e_r = jnp.where(has_r, pltpu.roll(ext, n_ext - 1, 0), 0.0)

    def taps(di):
        lo = CHUNK * (1 + di)
        i = 3 * (1 + di)
        return (cw[i:i + 1, :] * e_l[lo:lo + rows] + cw[i + 1:i + 2, :] * ext[lo:lo + rows]
                + cw[i + 2:i + 3, :] * e_r[lo:lo + rows])

    pos_o = t * rows + _iota((rows, 1), 0)
    acc = taps(0)
    acc = acc + jnp.where(pos_o - CHUNK >= n_ctx, taps(-1), 0.0)
    acc = acc + jnp.where((pos_o >= n_ctx) & (pos_o + CHUNK < seq), taps(1), 0.0)
    y = _silu(acc)
    q = _l2n_heads(y[:, 0:GROUP_W])
    k = _l2n_heads(y[:, GROUP_W:2 * GROUP_W])
    o_ref[0] = jnp.concatenate([q, k, y[:, 2 * GROUP_W:]], axis=1)


def _gd_prep(z_gd, cw, n_ctx):
    bsz, seq, _ = z_gd.shape
    w = 3 * GROUP_W
    rows = _row_tile(seq, 1088, CHUNK)
    per, n_chunks = rows // CHUNK, seq // CHUNK
    kern = functools.partial(_gd_prep_kernel, n_ctx, seq, rows)
    return pl.pallas_call(
        kern,
        name="gd_prep",
        grid=(bsz, seq // rows),
        in_specs=[pl.BlockSpec((1, rows, w), lambda b, t: (b, t, 0)),
                  pl.BlockSpec((1, CHUNK, w), lambda b, t: (b, jnp.maximum(t * per - 1, 0), 0)),
                  pl.BlockSpec((1, CHUNK, w), lambda b, t: (b, jnp.minimum((t + 1) * per, n_chunks - 1), 0)),
                  pl.BlockSpec(cw.shape, lambda b, t: (0, 0))],
        out_specs=pl.BlockSpec((1, rows, w), lambda b, t: (b, t, 0)),
        out_shape=jax.ShapeDtypeStruct((bsz, seq, w), F32),
        compiler_params=pltpu.CompilerParams(dimension_semantics=("arbitrary", "arbitrary"),
                                             vmem_limit_bytes=VMEM_LIMIT),
    )(z_gd, z_gd, z_gd, cw)


def _head_rms(h, gain):
    return h * lax.rsqrt(_block_sum(h * h) * (1.0 / HEAD_V) + EPS) * gain


def _head_groupnorm(h, gain, bias):
    dlt = h - _block_sum(h) * (1.0 / HEAD_V)
    return dlt * lax.rsqrt(_block_sum(dlt * dlt) * (1.0 / HEAD_V) + GN_EPS) * gain + bias


N_ROW_INPUTS = 13


def _out_mlp_kernel(n_ctx, tm, ff_chunk, final, n_part, row0, *refs):
    def stacked(i):
        parts = [p[0] for p in refs[i * n_part:(i + 1) * n_part]]
        return parts[0] if n_part == 1 else jnp.concatenate(parts, axis=0)

    x, hml_f, hml_b, hrw_f, hrw_b, hgl_f, hgl_b, hgd_f, hgd_b, oml, prw, ogl, zgd = (
        stacked(i) for i in range(N_ROW_INPUTS))
    modx_ref, modc_ref, np_ref, n2_ref, fn_ref, wo_ref, w1_ref, w2_ref, o_ref = refs[N_ROW_INPUTS * n_part:]
    t = pl.program_id(1)
    is_ctx = (row0 + t * tm + _iota((tm, 1), 0)) < n_ctx
    np_ = np_ref[...]
    mixed = jnp.concatenate([
        _head_rms(hml_f + hml_b, np_[0:1, :]) * _sigmoid(oml),
        (_head_groupnorm(hrw_f + hrw_b, np_[1:2, :], np_[2:3, :]) + prw[:, GROUP_W:]) * prw[:, :GROUP_W],
        _head_rms(hgl_f + hgl_b, np_[3:4, :]) * _silu(ogl),
        _head_rms(hgd_f + hgd_b, np_[4:5, :]) * _silu(zgd),
    ], axis=1)
    x = x + _pick_mod(modx_ref, modc_ref, is_ctx, 2) * _mm(_bf(mixed), wo_ref[...])
    shift = _pick_mod(modx_ref, modc_ref, is_ctx, 3)
    scale = _pick_mod(modx_ref, modc_ref, is_ctx, 4)
    h = _bf(_rms_rows(x, n2_ref[...]) * (1.0 + scale) + shift)
    d_ff = w1_ref.shape[1]
    acc = jnp.zeros(x.shape, F32)
    for c0 in range(0, d_ff, ff_chunk):
        a = jnp.maximum(_mm(h, w1_ref[:, c0:c0 + ff_chunk]), 0.0)
        acc = acc + _mm(_bf(a * a), w2_ref[c0:c0 + ff_chunk, :])
    x = x + _pick_mod(modx_ref, modc_ref, is_ctx, 5) * acc
    if final:
        x = _rms_rows(x, fn_ref[...])
    o_ref[0] = x


def _out_mlp(tok, hs, z_ml, rw_post, z_gl, z_gd, modx, modc, normp, n2, fnw, wo, w1, w2,
             n_ctx, tm, final):
    bsz, seq, d_model = tok.shape
    half = next((hb for hb in (256, 128, 64, 32, 16, 8) if n_ctx % hb == 0 and (seq - n_ctx) % (2 * hb) == 0), 0)
    latent_only = final and half > 0
    if latent_only:
        n_part, tm, row0, out_rows = 2, 2 * half, n_ctx, seq - n_ctx
        first = n_ctx // half

        def row(w, j=0):
            return [pl.BlockSpec((1, half, w), lambda b, t, j=j, p=p: (b, first + 2 * t + p, j)) for p in range(2)]
    else:
        n_part, row0, out_rows = 1, 0, seq

        def row(w, j=0):
            return [pl.BlockSpec((1, tm, w), lambda b, t, j=j: (b, t, j))]
    kern = functools.partial(_out_mlp_kernel, n_ctx, tm, 1024, final, n_part, row0)
    const = lambda a: pl.BlockSpec(a.shape, lambda b, t, nd=a.ndim: (0,) * nd)
    wconst = lambda a: pl.BlockSpec(a.shape, lambda b, t: (0, 0), pipeline_mode=pl.Buffered(1))
    row_arrays = [tok, *hs, z_ml, rw_post, z_gl, z_gd]
    row_specs = (row(d_model) + [s for _ in hs for s in row(GROUP_W)]
                 + row(GROUP_W, 2) + row(2 * GROUP_W) + row(GROUP_W, 2) + row(GROUP_W, 3))
    return pl.pallas_call(
        kern,
        name="out_mlp",
        grid=(bsz, out_rows // tm),
        in_specs=row_specs + [
                  pl.BlockSpec((1, 8, d_model), lambda b, t: (b, 0, 0)), const(modc), const(normp), const(n2),
                  const(fnw), wconst(wo), wconst(w1), wconst(w2)],
        out_specs=pl.BlockSpec((1, tm, d_model), lambda b, t: (b, t, 0)),
        out_shape=jax.ShapeDtypeStruct((bsz, out_rows, d_model), F32),
        compiler_params=pltpu.CompilerParams(dimension_semantics=("arbitrary", "arbitrary"),
                                             vmem_limit_bytes=VMEM_LIMIT),
    )(*[a for a in row_arrays for _ in range(n_part)], modx, modc, normp, n2, fnw, wo, w1, w2)


def _pad_rows(a, rows=8):
    return jnp.concatenate([a, jnp.zeros((rows - a.shape[0],) + a.shape[1:], a.dtype)], axis=0)


def _misc_row(pairs):
    row = jnp.zeros((MISC_W,), F32)
    for col0, p in pairs:
        row = row.at[col0:col0 + N_DIR * HEADS].set(p.reshape(-1))
    return row


def _permute_w_in(w):
    d_model = w.shape[0]
    ml_w, rw_w, gl_w = 784, 1024, 784
    o_ml, o_rw, o_gl, o_gd = 0, ml_w, ml_w + rw_w, ml_w + rw_w + gl_w
    misc = jnp.concatenate([
        w[:, o_ml + 768:o_ml + 784],
        w[:, o_gl + 512:o_gl + 528],
        w[:, o_gd + 1024:o_gd + 1040],
        jnp.zeros((d_model, MISC_W - 48), w.dtype)], axis=1)
    return jnp.concatenate([
        w[:, o_ml:o_ml + 768],
        w[:, o_rw:o_rw + 1024],
        w[:, o_gl:o_gl + 512], w[:, o_gl + 528:o_gl + 784],
        w[:, o_gd:o_gd + 1024],
        misc], axis=1)


Z_WIDTHS = (768, 1024, 768, 1024, MISC_W)


SCAN_NB = 8
SCAN_GROUPS = (('ml', 'rw'), ('gl', 'gd'))


def _mix_layer(z_ml, base, d0, d1, z_gl, z_gd, z_misc, p, n_ctx):
    nb = SCAN_NB if z_ml.shape[0] % SCAN_NB == 0 else 1
    qkv = _gd_prep(z_gd, p['gd_cw'], n_ctx)
    mixers = {
        'ml': _Mixer(_ml_chunk, [z_ml, z_misc], [z_ml, z_misc], [p['ml_gb']],
                     (((QK_W, 2 * GROUP_W), 0.0), ((1, GROUP_W), M_INIT))),
        'rw': _Mixer(_rw_chunk, [base, d0], [base, d1], [p['rw_rp']], (((GROUP_W, GROUP_W), 0.0),)),
        'gl': _Mixer(_gl_chunk, [z_gl, z_misc], [z_gl, z_misc], [p['gl_gup'], p['gl_gb']], (((GROUP_W, QK_W), 0.0),)),
        'gd': _Mixer(_gd_chunk, [qkv, z_misc], [qkv, z_misc], [p['gd_gp']], (((GROUP_W, GROUP_W), 0.0),)),
    }
    h = {}
    for group in SCAN_GROUPS:
        for name, out in zip(group, _bidir_scan([mixers[g] for g in group], n_ctx, nb)):
            h[name] = out
    return (*h['ml'], *h['rw'], *h['gl'], *h['gd'])


def _layer_params(layer, w_in, ml_ig_b, ml_fg_b, ml_norm_w, rw_mu_prev, rw_mu_next, rw_w0, rw_w_up, rw_a0,
                  rw_a_up, rw_g_up, rw_k_k, rw_k_a, rw_r_k, rw_gn_w, rw_gn_b, gl_gate_up, gl_gate_b, gl_norm_w,
                  gd_conv_w, gd_a_log, gd_dt_bias, gd_norm_w):
    l = layer
    z64 = jnp.zeros((64, GROUP_W), F32)
    lora = jnp.concatenate([
        jnp.concatenate([rw_w_up[l, 0], z64, rw_w_up[l, 1], z64], axis=1),
        jnp.concatenate([z64, rw_a_up[l, 0], z64, rw_a_up[l, 1]], axis=1)], axis=0)
    gup = jnp.zeros((N_DIR, MISC_W, QK_W), F32).at[:, MISC_GL_AL:MISC_GL_AL + 16, :].set(gl_gate_up[l])
    return {
        'w_in': _bf(_permute_w_in(w_in[l])),
        'ml_gb': _pad_rows(_misc_row([(MISC_ML_IG, ml_ig_b[l]), (MISC_ML_FG, ml_fg_b[l])])[None, :]),
        'gl_gup': gup,
        'gl_gb': _pad_rows(gl_gate_b[l]),
        'gd_cw': _pad_rows(gd_conv_w[l].reshape(9, 3 * GROUP_W), 16),
        'gd_gp': _pad_rows(jnp.stack([_misc_row([(MISC_GD_AL, gd_dt_bias[l])]), _misc_row([(MISC_GD_AL, gd_a_log[l])])])),
        'rw_mu': _pad_rows(jnp.stack([rw_mu_prev[l], rw_mu_next[l]])),
        'rw_rp': _pad_rows(jnp.stack([rw_w0[l, 0], rw_w0[l, 1], rw_a0[l, 0], rw_a0[l, 1],
                                      rw_k_k[l], rw_k_a[l], rw_r_k[l]])),
        'rw_lora': lora,
        'rw_gup': rw_g_up[l],
        'normp': _pad_rows(jnp.stack([ml_norm_w[l], rw_gn_w[l], rw_gn_b[l], gl_norm_w[l], gd_norm_w[l]])),
    }


def kernel(x, c, ctx, c_ctx, ada_w, ada_b, norm1_w, norm2_w, w_in, w_out, ml_ig_b, ml_fg_b, ml_norm_w, rw_mu_prev, rw_mu_next, rw_w0, rw_w_up, rw_a0, rw_a_up, rw_g_up, rw_k_k, rw_k_a, rw_r_k, rw_gn_w, rw_gn_b, gl_gate_up, gl_gate_b, gl_norm_w, gd_conv_w, gd_a_log, gd_dt_bias, gd_norm_w, mlp_w1, mlp_w2, final_norm_w):
    bsz, seq_x, d_model = x.shape
    n_ctx = ctx.shape[1]
    depth = w_in.shape[0]
    seq = n_ctx + seq_x
    tm = seq // 8 if seq % 64 == 0 else seq
    tok = jnp.concatenate([ctx, x], axis=1)
    cc = _pad_rows(jnp.concatenate([c, c_ctx[None, :]], axis=0), ((bsz + 1 + 7) // 8) * 8)
    mod = _ada_mod(cc, ada_w, ada_b)
    fnw = final_norm_w[None, :]
    for layer in range(depth):
        p = _layer_params(layer, w_in, ml_ig_b, ml_fg_b, ml_norm_w, rw_mu_prev, rw_mu_next, rw_w0, rw_w_up, rw_a0,
                          rw_a_up, rw_g_up, rw_k_k, rw_k_a, rw_r_k, rw_gn_w, rw_gn_b, gl_gate_up, gl_gate_b,
                          gl_norm_w, gd_conv_w, gd_a_log, gd_dt_bias, gd_norm_w)
        modx = jnp.pad(mod[layer, :bsz].reshape(bsz, 6, d_model), ((0, 0), (0, 2), (0, 0)))
        modc = _pad_rows(mod[layer, bsz].reshape(6, d_model))
        z_ml, base, d0, d1, rw_post, z_gl, z_gd, z_misc = _in_proj(
            tok, modx, modc, norm1_w[layer][None, :], p['w_in'], p['rw_mu'], p['rw_rp'], p['rw_lora'], p['rw_gup'],
            n_ctx, tm)
        hs = _mix_layer(z_ml, base, d0, d1, z_gl, z_gd, z_misc, p, n_ctx)
        tok = _out_mlp(tok, hs, z_ml, rw_post, z_gl, z_gd, modx, modc, p['normp'],
                       norm2_w[layer][None, :], fnw, _bf(w_out[layer]), _bf(mlp_w1[layer]), _bf(mlp_w2[layer]),
                       n_ctx, tm, layer == depth - 1)
    return tok if tok.shape[1] == seq_x else tok[:, n_ctx:]
```

```python
import functools

import jax
import jax.numpy as jnp
from jax import lax
from jax.experimental import pallas as pl
from jax.experimental.pallas import tpu as pltpu

F32 = jnp.float32
BF16 = jnp.bfloat16

CHUNK = 64
HEADS = 4
HEAD_V = 64
HEAD_QK = 32
GROUP_W = HEADS * HEAD_V
QK_W = HEADS * HEAD_QK
N_DIR = 2
EPS = 1e-6
GN_EPS = 64e-5
M_INIT = -1e30
RW_DECAY_SCALE = 0.6065306597126334
GLA_LOGIT_NORM = 16.0
MISC_W = 128
MISC_ML_IG, MISC_ML_FG, MISC_GL_AL, MISC_GD_BL, MISC_GD_AL = 0, 8, 16, 32, 40
VMEM_LIMIT = 56 * 1024 * 1024


def _mm(a, b, prec=None):
    return lax.dot_general(a, b, (((1,), (0,)), ((), ())), precision=prec, preferred_element_type=F32)


def _mm_nt(a, b, prec=None):
    return lax.dot_general(a, b, (((1,), (1,)), ((), ())), precision=prec, preferred_element_type=F32)


def _mm_tn(a, b, prec=None):
    return lax.dot_general(a, b, (((0,), (0,)), ((), ())), precision=prec, preferred_element_type=F32)


def _bf(x):
    return x.astype(BF16)


def _iota(shape, dim):
    return lax.broadcasted_iota(jnp.int32, shape, dim)


def _sigmoid(x):
    return 1.0 / (1.0 + jnp.exp(-x))


def _softplus(x):
    return jnp.maximum(x, 0.0) + jnp.log1p(jnp.exp(-jnp.abs(x)))


def _log_sigmoid(x):
    return -_softplus(-x)


def _silu(x):
    return x * _sigmoid(x)


def _time_mask(d, width, strict):
    i = _iota((CHUNK, width), 0)
    j = _iota((CHUNK, width), 1) & (CHUNK - 1)
    if d == 0:
        return (j < i) if strict else (j <= i)
    return (j > i) if strict else (j >= i)


def _cumsum_mat(d):
    return jnp.where(_time_mask(d, CHUNK, False), 1.0, 0.0).astype(F32)


def _eye_cat(width):
    i = _iota((CHUNK, width), 0)
    j = _iota((CHUNK, width), 1) & (CHUNK - 1)
    return jnp.where(i == j, 1.0, 0.0).astype(F32)


def _bd_mask(rows, cols, row_shift, col_shift, col_and=None):
    r = _iota((rows, cols), 0) >> row_shift
    c = _iota((rows, cols), 1)
    if col_and is not None:
        c = c & col_and
    return r == (c >> col_shift)


def _stack_bd(x, col_shift):
    w = x.shape[1]
    m = _bd_mask(HEADS * CHUNK, w, 6, col_shift)
    return _bf(jnp.where(m, jnp.concatenate([x.astype(F32)] * HEADS, axis=0), 0.0))


def _split(x, n):
    out, r = [], x
    for i in range(n):
        p = r.astype(BF16)
        out.append(p)
        if i + 1 < n:
            r = r - p.astype(F32)
    return out


def _sel_left(t, x, n=2):
    return _mm(jnp.concatenate([_bf(t)] * n, axis=1), jnp.concatenate(_split(x, n), axis=0))


def _sel_right(x, e, n=2):
    return _mm(jnp.concatenate(_split(x, n), axis=1), jnp.concatenate([_bf(e)] * n, axis=0))


def _mm3(a, b):
    ah, al = _split(a, 2)
    bh, bl = _split(b, 2)
    return _mm(jnp.concatenate([ah, ah, al], axis=1), jnp.concatenate([bh, bl, bh], axis=0))


PAIR_W = 2 * HEAD_V


def _bd2(x):
    lo = _iota(x.shape, 1) < HEAD_V
    x = x.astype(F32)
    return _bf(jnp.concatenate([jnp.where(lo, x, 0.0), jnp.where(lo, 0.0, x)], axis=0))


def _bd_mm(lhs, rhs, nt=False):
    return _bd_mm_shared(lhs, [rhs], nt)[0]


def _bd_mm_shared(lhs, rhs_list, nt=False):
    n_out = len(rhs_list)
    outs = [[] for _ in range(n_out)]
    for g in range(GROUP_W // PAIR_W):
        sl = slice(g * PAIR_W, (g + 1) * PAIR_W)
        l = jnp.concatenate([_bf(a)[:, sl] for a in lhs], axis=1)
        rs = [jnp.concatenate([_bd2(b[:, sl]) for b in rhs], axis=1 if nt else 0) for rhs in rhs_list]
        r = jnp.concatenate(rs, axis=0 if nt else 1)
        o = _mm_nt(l, r) if nt else _mm(l, r)
        for i in range(n_out):
            outs[i].append(o[:, i * PAIR_W:(i + 1) * PAIR_W])
    return [jnp.concatenate(o, axis=1) for o in outs]


def _cumsum_t(d, x):
    return _sel_left(_cumsum_mat(d), x)


def _expand_mat(col0s):
    r = _iota((MISC_W, GROUP_W), 0)
    hcol = _iota((MISC_W, GROUP_W), 1) >> 6
    return jnp.concatenate([jnp.where(r == hcol + c0, 1.0, 0.0) for c0 in col0s], axis=1).astype(F32)


def _to_cols(x):
    return _sel_left(jnp.ones((CHUNK, CHUNK), F32), x * _eye_cat(GROUP_W))


def _block_sum(x):
    w = x.shape[1]
    ones_bd = jnp.where(_bd_mask(w, w, 6, 6), 1.0, 0.0).astype(F32)
    return _sel_right(x, ones_bd, 2)


def _block_max(x):
    blk = _iota(x.shape, 1) >> 6
    out = jnp.zeros_like(x)
    for h in range(HEADS):
        sel = blk == h
        mh = jnp.max(jnp.where(sel, x, -jnp.inf), axis=1, keepdims=True)
        out = jnp.where(sel, mh, out)
    return out


def _inv_unit(n):
    i = _iota((CHUNK, GROUP_W), 0)
    j = _iota((CHUNK, GROUP_W), 1) & (CHUNK - 1)
    x = _eye_cat(GROUP_W)
    for lvl in range(6):
        c = jnp.where(((i >> lvl) ^ (j >> lvl)) == 1, n, 0.0)
        if lvl == 0:
            x = x + c
        else:
            y = _bd_mm([c], [x])
            yield
            x = x + _bd_mm([x], [y])
            yield
    return x


def _l2n_heads(t):
    return t * lax.rsqrt(_block_sum(t * t) + EPS)


def _ml_chunk(d, xs, cs, st):
    z, misc = xs
    (gb,) = cs
    s_mat, m_row = st
    last = CHUNK - 1 if d == 0 else 0
    q = z[:, 0:QK_W] * (HEAD_QK ** -0.5)
    k = z[:, QK_W:2 * QK_W]
    v = z[:, 2 * QK_W:2 * QK_W + GROUP_W]
    t = misc + gb[0:1, :]
    is_fg = (_iota(misc.shape, 1) >> 3) == (MISC_ML_FG >> 3)
    y = jnp.where(is_fg, _log_sigmoid(t), t)
    cum = _cumsum_t(d, y)
    a = _mm_nt(_bf(q), _stack_bd(k, 5))
    inter = _mm(_bf(q), _bf(s_mat))
    yield
    ex = _sel_right(jnp.where(is_fg, cum, y), _expand_mat((MISC_ML_IG + HEADS * d, MISC_ML_FG + HEADS * d)))
    yield
    ig, bcum = ex[:, :GROUP_W], ex[:, GROUP_W:]
    b_end = bcum[last:last + 1, :]
    w_end = b_end - bcum + ig
    m_loc = jnp.max(w_end, axis=0, keepdims=True)
    e_end = jnp.exp(w_end - m_loc)
    cols = _to_cols(ig - bcum)
    loc = _mm_tn(_bf(k), _bf(jnp.concatenate([v * e_end, e_end], axis=1)))
    yield
    loc = jnp.where(_bd_mask(QK_W, 2 * GROUP_W, 5, 6, GROUP_W - 1), loc, 0.0)
    m_prev_s = b_end + m_row
    m_new = jnp.maximum(m_prev_s, m_loc)
    f_s = jnp.exp(m_prev_s - m_new)
    f_l = jnp.exp(m_loc - m_new)
    s_new = jnp.concatenate([f_s, f_s], axis=1) * s_mat + jnp.concatenate([f_l, f_l], axis=1) * loc
    logd = jnp.where(_time_mask(d, GROUP_W, False), bcum + cols, -jnp.inf)
    m_prev = bcum + m_row
    m_i = jnp.maximum(m_prev, _block_max(logd))
    sm = a * jnp.exp(logd - m_i)
    e_prev = jnp.exp(m_prev - m_i)
    num, den = _bd_mm_shared([sm], [[v], [jnp.ones((CHUNK, GROUP_W), F32)]])
    yield
    num = num + e_prev * inter[:, :GROUP_W]
    den = den + e_prev * inter[:, GROUP_W:]
    h = num / jnp.maximum(jnp.abs(den), jnp.exp(-m_i))
    return h, (s_new, m_new)


def _gl_chunk(d, xs, cs, st):
    z, misc = xs
    gup, gb = cs
    (s_mat,) = st
    last = CHUNK - 1 if d == 0 else 0
    mid = CHUNK // 2 if d == 0 else CHUNK - 1 - CHUNK // 2
    q = z[:, 0:QK_W] * (HEAD_QK ** -0.5)
    k = z[:, QK_W:2 * QK_W]
    v = z[:, 2 * QK_W:2 * QK_W + GROUP_W]
    pre = _mm3(misc, gup[d])
    yield
    la = _log_sigmoid(pre + gb[d:d + 1, :]) * (1.0 / GLA_LOGIT_NORM)
    g = _cumsum_t(d, la)
    yield
    g_end = g[last:last + 1, :]
    g_mid = g[mid:mid + 1, :]
    qd = q * jnp.exp(g - g_mid)
    kd = k * jnp.exp(g_mid - g)
    att = _mm_nt(_bf(qd), _stack_bd(kd, 5))
    s_loc = _mm_tn(_bf(v), _bf(k * jnp.exp(g_end - g)))
    o_inter = _mm_nt(_bf(q * jnp.exp(g)), _bf(s_mat))
    yield
    att = jnp.where(_time_mask(d, GROUP_W, False), att, 0.0)
    s_new = jnp.exp(g_end) * s_mat + jnp.where(_bd_mask(GROUP_W, QK_W, 6, 5), s_loc, 0.0)
    o_intra = _bd_mm([att], [v])
    yield
    return o_inter + o_intra, (s_new,)


def _gd_chunk(d, xs, cs, st):
    qkv, misc = xs
    (gp,) = cs
    (s_mat,) = st
    last = CHUNK - 1 if d == 0 else 0
    q = qkv[:, 0:GROUP_W] * (HEAD_V ** -0.5)
    k = qkv[:, GROUP_W:2 * GROUP_W]
    v = qkv[:, 2 * GROUP_W:3 * GROUP_W]
    t = misc + gp[0:1, :]
    is_al = (_iota(misc.shape, 1) >> 3) == (MISC_GD_AL >> 3)
    y = jnp.where(is_al, -jnp.exp(gp[1:2, :]) * _softplus(t), _sigmoid(t))
    cum = _cumsum_t(d, y)
    x2 = _bd_mm([jnp.concatenate([k, q], axis=0)], [k], nt=True)
    yield
    ex = _sel_right(jnp.where(is_al, cum, y), _expand_mat((MISC_GD_BL + HEADS * d, MISC_GD_AL + HEADS * d)))
    yield
    beta, g = ex[:, :GROUP_W], ex[:, GROUP_W:]
    g_end = g[last:last + 1, :]
    cols = _to_cols(g)
    yield
    dec0 = jnp.where(_time_mask(d, GROUP_W, False), jnp.exp(g - cols), 0.0)
    kk, qk = x2[:CHUNK], x2[CHUNK:]
    mm = jnp.where(_time_mask(d, GROUP_W, True), beta * kk * dec0, 0.0)
    ainv = yield from _inv_unit(-mm)
    eg = jnp.exp(g)
    u, w = _bd_mm_shared([ainv], [[v * beta], [k * (beta * eg)]])
    yield
    ws = _mm(_bf(jnp.concatenate([w, q * eg], axis=0)), _bf(s_mat))
    yield
    v_new = u - ws[:CHUNK]
    o_inter = ws[CHUNK:]
    k_end = k * jnp.exp(g_end - g)
    upd = _mm_tn(_bf(k_end), _bf(v_new))
    o_intra = _bd_mm([qk * dec0], [v_new])
    yield
    s_new = jnp.exp(g_end) * s_mat + jnp.where(_bd_mask(GROUP_W, GROUP_W, 6, 6), upd, 0.0)
    return o_inter + o_intra, (s_new,)


def _rw_chunk(d, xs, cs, st):
    base, dirp = xs
    (rp,) = cs
    (s_mat,) = st
    last = CHUNK - 1 if d == 0 else 0
    r = base[:, 0:GROUP_W]
    k = base[:, GROUP_W:2 * GROUP_W]
    v = base[:, 2 * GROUP_W:3 * GROUP_W]
    kk = base[:, 3 * GROUP_W:4 * GROUP_W]
    lw = dirp[:, 0:GROUP_W]
    a = dirp[:, GROUP_W:2 * GROUP_W]
    k_a = rp[5:6, :]
    kt = k * (1.0 + (a - 1.0) * k_a)
    ap = kk * a
    g = _cumsum_t(d, lw)
    yield
    g_end = g[last:last + 1, :]
    eg = jnp.exp(g)
    eng = jnp.exp(-g)
    r_h = r * eg
    a_h = ap * eng
    k_h = kt * eng
    b_h = -kk * jnp.exp(g - lw)
    br = jnp.concatenate([b_h, r_h], axis=0)
    xa, xk = _bd_mm_shared([br], [[a_h], [k_h]], nt=True)
    yield
    tm1 = _time_mask(d, GROUP_W, True)
    tm0 = _time_mask(d, GROUP_W, False)
    m_ab = jnp.where(tm1, xa[:CHUNK], 0.0)
    m_bk = jnp.where(tm1, xk[:CHUNK], 0.0)
    att_a = jnp.where(tm0, xa[CHUNK:], 0.0)
    att_k = jnp.where(tm0, xk[CHUNK:], 0.0)
    rhs2 = _bd_mm([m_bk], [v])
    ainv = yield from _inv_unit(m_ab)
    w_mat, u0 = _bd_mm_shared([ainv], [[b_h], [rhs2]])
    yield
    ws = _mm_nt(_bf(jnp.concatenate([w_mat, r_h], axis=0)), _bf(s_mat))
    yield
    u = u0 + ws[:CHUNK]
    y_inter = ws[CHUNK:]
    dec_end = jnp.exp(g_end - g)
    upd = _mm_tn(_bf(jnp.concatenate([u, v], axis=0)), _bf(jnp.concatenate([ap * dec_end, kt * dec_end], axis=0)))
    y_intra = _bd_mm([att_a, att_k], [u, v])
    yield
    s_new = jnp.exp(g_end) * s_mat + jnp.where(_bd_mask(GROUP_W, GROUP_W, 6, 6), upd, 0.0)
    return y_inter + y_intra, (s_new,)


class _Mixer:
    def __init__(self, chunk_fn, ins_f, ins_b, csts, state_init):
        self.chunk_fn, self.ins_f, self.ins_b, self.csts, self.state_init = chunk_fn, ins_f, ins_b, csts, state_init


def _scan_kernel(layout, nb, *refs):
    pos = 0
    parts = []
    for chunk_fn, n_in, n_cst, state_init in layout:
        ins = (refs[pos:pos + n_in], refs[pos + n_in:pos + 2 * n_in])
        csts = refs[pos + 2 * n_in:pos + 2 * n_in + n_cst]
        pos += 2 * n_in + n_cst
        parts.append([chunk_fn, ins, csts, state_init])
    for p in parts:
        p.append(refs[pos:pos + N_DIR])
        pos += N_DIR
    for p in parts:
        p.append(refs[pos:pos + len(p[3])])
        pos += len(p[3])

    @pl.when(pl.program_id(1) == 0)
    def _():
        for _, _, _, state_init, _, st_refs in parts:
            for i, (shape, val) in enumerate(state_init):
                st_refs[i][...] = jnp.full((N_DIR * nb,) + shape, val, F32)

    chains, gens = [], []
    for bi in range(nb):
        for pi, (chunk_fn, ins, csts, state_init, _, st_refs) in enumerate(parts):
            cs = tuple(c[...] for c in csts)
            for d in range(N_DIR):
                xs = tuple(r[bi] for r in ins[d])
                st = tuple(st_refs[i][d * nb + bi] for i in range(len(state_init)))
                chains.append((pi, bi, d))
                gens.append(chunk_fn(d, xs, cs, st))
    results = [None] * len(chains)
    while any(r is None for r in results):
        for ci, gen in enumerate(gens):
            if results[ci] is None:
                try:
                    next(gen)
                except StopIteration as stop:
                    results[ci] = stop.value
    for (pi, bi, d), (y, st_new) in zip(chains, results):
        o_refs, st_refs = parts[pi][4], parts[pi][5]
        for i, v in enumerate(st_new):
            st_refs[i][d * nb + bi] = v
        o_refs[d][bi] = y


def _bidir_scan(mixers, n_ctx, nb):
    bsz, seq = mixers[0].ins_f[0].shape[0], mixers[0].ins_f[0].shape[1]
    n_chunks, n_ctx_chunks = seq // CHUNK, n_ctx // CHUNK

    def idx_f(b, s):
        return (b, s, 0)

    def idx_b(b, s):
        return (b, jnp.where(s < n_ctx_chunks, n_ctx_chunks - 1 - s, n_chunks - 1 + n_ctx_chunks - s), 0)

    in_specs, args, layout, scratch = [], [], [], []
    for m in mixers:
        in_specs += [pl.BlockSpec((nb, CHUNK, a.shape[2]), idx_f) for a in m.ins_f]
        in_specs += [pl.BlockSpec((nb, CHUNK, a.shape[2]), idx_b) for a in m.ins_b]
        in_specs += [pl.BlockSpec(c.shape, lambda b, s, nd=c.ndim: (0,) * nd) for c in m.csts]
        args += [*m.ins_f, *m.ins_b, *m.csts]
        layout.append((m.chunk_fn, len(m.ins_f), len(m.csts), m.state_init))
        scratch += [pltpu.VMEM((N_DIR * nb,) + shape, F32) for (shape, _) in m.state_init]
    outs = pl.pallas_call(
        functools.partial(_scan_kernel, tuple(layout), nb),
        name="scan_" + "_".join(m.chunk_fn.__name__.strip("_").split("_")[0] for m in mixers),
        grid=(bsz // nb, n_chunks),
        in_specs=in_specs,
        out_specs=[pl.BlockSpec((nb, CHUNK, GROUP_W), idx) for _ in mixers for idx in (idx_f, idx_b)],
        out_shape=[jax.ShapeDtypeStruct((bsz, seq, GROUP_W), F32)] * (N_DIR * len(mixers)),
        scratch_shapes=scratch,
        compiler_params=pltpu.CompilerParams(dimension_semantics=("arbitrary", "arbitrary"),
                                             vmem_limit_bytes=VMEM_LIMIT),
    )(*args)
    return [tuple(outs[N_DIR * i:N_DIR * (i + 1)]) for i in range(len(mixers))]


def _ada_kernel(c_ref, w_ref, b_ref, o_ref):
    o_ref[0] = _mm(_bf(_silu(c_ref[...])), _bf(w_ref[0])) + b_ref[0]


def _ada_mod(cc, ada_w, ada_b):
    depth, d_model, n6 = ada_w.shape
    tn = 1536
    return pl.pallas_call(
        _ada_kernel,
        name="ada_mod",
        grid=(depth, n6 // tn),
        in_specs=[pl.BlockSpec(cc.shape, lambda l, j: (0, 0)),
                  pl.BlockSpec((1, d_model, tn), lambda l, j: (l, 0, j)),
                  pl.BlockSpec((1, 1, tn), lambda l, j: (l, 0, j))],
        out_specs=pl.BlockSpec((1, cc.shape[0], tn), lambda l, j: (l, 0, j)),
        out_shape=jax.ShapeDtypeStruct((depth, cc.shape[0], n6), F32),
        compiler_params=pltpu.CompilerParams(dimension_semantics=("arbitrary", "arbitrary"),
                                             vmem_limit_bytes=VMEM_LIMIT),
    )(cc, ada_w, ada_b.reshape(depth, 1, n6))


def _rms_rows(x, gain):
    return x * lax.rsqrt(jnp.mean(x * x, axis=-1, keepdims=True) + EPS) * gain


def _pick_mod(modx_ref, modc_ref, is_ctx, row):
    return jnp.where(is_ctx, modc_ref[row:row + 1, :], modx_ref[0, row:row + 1, :])


def _modulated_rows(x, pos, n_ctx, modx_ref, modc_ref, nw):
    is_ctx = pos < n_ctx
    shift = _pick_mod(modx_ref, modc_ref, is_ctx, 0)
    scale = _pick_mod(modx_ref, modc_ref, is_ctx, 1)
    return _bf(_rms_rows(x, nw) * (1.0 + scale) + shift)


def _rw_token_prep(z, z_before, z_after, pos, n_ctx, seq, mu, rp, lora, gup):
    rows = z.shape[0]
    row = _iota((rows, 1), 0)
    z_prev = jnp.where(row == 0, z_before, pltpu.roll(z, 1, 0))
    z_prev = jnp.where((pos != 0) & (pos != n_ctx), z_prev, 0.0)
    z_next = jnp.where(row == rows - 1, z_after, pltpu.roll(z, rows - 1, 0))
    z_next = jnp.where((pos != seq - 1) & (pos != n_ctx - 1), z_next, 0.0)
    zm = z + mu[0:1, :] * (z_prev - z) + mu[1:2, :] * (z_next - z)
    yield
    r = zm[:, 0:GROUP_W]
    k = zm[:, GROUP_W:2 * GROUP_W]
    v = zm[:, 2 * GROUP_W:3 * GROUP_W]
    lo = zm[:, 3 * GROUP_W:3 * GROUP_W + 128]
    gl = zm[:, 3 * GROUP_W + 128:4 * GROUP_W]
    lo = jnp.where(_iota(lo.shape, 1) < 64, jnp.tanh(lo), lo)
    pre = _mm3(lo, lora)
    yield
    kk = _l2n_heads(k * rp[4:5, :])
    gate = _mm(_bf(_sigmoid(gl)), _bf(gup))
    yield
    lw0 = -RW_DECAY_SCALE * _sigmoid(rp[0:1, :] + pre[:, 0:GROUP_W])
    a0 = _sigmoid(rp[2:3, :] + pre[:, GROUP_W:2 * GROUP_W])
    lw1 = -RW_DECAY_SCALE * _sigmoid(rp[1:2, :] + pre[:, 2 * GROUP_W:3 * GROUP_W])
    a1 = _sigmoid(rp[3:4, :] + pre[:, 3 * GROUP_W:4 * GROUP_W])
    k_a = rp[5:6, :]
    k_sum = k * (1.0 + (a0 - 1.0) * k_a) + k * (1.0 + (a1 - 1.0) * k_a)
    bonus = _block_sum(r * k_sum * rp[6:7, :]) * v
    yield
    return (jnp.concatenate([r, k, v, kk], axis=1), jnp.concatenate([lw0, a0], axis=1),
            jnp.concatenate([lw1, a1], axis=1), jnp.concatenate([gate, bonus], axis=1))


def _in_proj_kernel(n_ctx, seq, tm, cols, tok_ref, tokp_ref, tokn_ref, modx_ref, modc_ref, nw_ref, w_ref, mu_ref,
                    rp_ref, lora_ref, gup_ref, zml_ref, base_ref, d0_ref, d1_ref, post_ref, zgl_ref, zgd_ref,
                    misc_ref):
    t = pl.program_id(1)
    nw = nw_ref[...]
    pos = t * tm + _iota((tm, 1), 0)
    pos8 = _iota((8, 1), 0)
    h = _modulated_rows(tok_ref[0], pos, n_ctx, modx_ref, modc_ref, nw)
    c0, c1 = cols['rw']
    z_rw = _mm(h, w_ref[:, c0:c1])
    h_before = _modulated_rows(tokp_ref[0], t * tm - 8 + pos8, n_ctx, modx_ref, modc_ref, nw)
    h_after = _modulated_rows(tokn_ref[0], (t + 1) * tm + pos8, n_ctx, modx_ref, modc_ref, nw)
    z_before = _mm(h_before, w_ref[:, c0:c1])[7:8, :]
    z_after = _mm(h_after, w_ref[:, c0:c1])[0:1, :]
    prep = _rw_token_prep(z_rw, z_before, z_after, pos, n_ctx, seq, mu_ref[...], rp_ref[...], lora_ref[...],
                          gup_ref[...])
    rw_out = None
    for name, o_ref in (('ml', zml_ref), ('gl', zgl_ref), ('gd', zgd_ref), ('misc', misc_ref)):
        c0, c1 = cols[name]
        o_ref[0] = _mm(h, w_ref[:, c0:c1])
        if rw_out is None:
            try:
                next(prep)
            except StopIteration as stop:
                rw_out = stop.value
    while rw_out is None:
        try:
            next(prep)
        except StopIteration as stop:
            rw_out = stop.value
    base_ref[0], d0_ref[0], d1_ref[0], post_ref[0] = rw_out


def _in_proj(tok, modx, modc, nw, w_p, mu, rp, lora, gup, n_ctx, tm):
    bsz, seq, d_model = tok.shape
    cols, c0 = {}, 0
    for name, w in zip(('ml', 'rw', 'gl', 'gd', 'misc'), Z_WIDTHS):
        cols[name] = (c0, c0 + w)
        c0 += w
    widths = (Z_WIDTHS[0], 4 * GROUP_W, 2 * GROUP_W, 2 * GROUP_W, 2 * GROUP_W, Z_WIDTHS[2], Z_WIDTHS[3], Z_WIDTHS[4])
    sub, n_sub = tm // 8, seq // 8
    const = lambda a: pl.BlockSpec(a.shape, lambda b, t, nd=a.ndim: (0,) * nd)
    kern = functools.partial(_in_proj_kernel, n_ctx, seq, tm, cols)
    return pl.pallas_call(
        kern,
        name="in_proj",
        grid=(bsz, seq // tm),
        in_specs=[pl.BlockSpec((1, tm, d_model), lambda b, t: (b, t, 0)),
                  pl.BlockSpec((1, 8, d_model), lambda b, t: (b, jnp.maximum(t * sub - 1, 0), 0)),
                  pl.BlockSpec((1, 8, d_model), lambda b, t: (b, jnp.minimum((t + 1) * sub, n_sub - 1), 0)),
                  pl.BlockSpec((1, 8, d_model), lambda b, t: (b, 0, 0)),
                  const(modc), const(nw),
                  pl.BlockSpec(w_p.shape, lambda b, t: (0, 0), pipeline_mode=pl.Buffered(1)),
                  const(mu), const(rp), const(lora), const(gup)],
        out_specs=[pl.BlockSpec((1, tm, w), lambda b, t: (b, t, 0)) for w in widths],
        out_shape=[jax.ShapeDtypeStruct((bsz, seq, w), F32) for w in widths],
        compiler_params=pltpu.CompilerParams(dimension_semantics=("arbitrary", "arbitrary"),
                                             vmem_limit_bytes=VMEM_LIMIT),
    )(tok, tok, tok, modx, modc, nw, w_p, mu, rp, lora, gup)


def _row_tile(seq, cap, mult):
    best = mult
    for r in range(mult, min(seq, cap) + 1, mult):
        if seq % r == 0:
            best = r
    return best


def _gd_prep_kernel(n_ctx, seq, rows, zc_ref, zu_ref, zd_ref, cw_ref, o_ref):
    t = pl.program_id(1)
    cw = cw_ref[...]
    ext = jnp.concatenate([zu_ref[0], zc_ref[0], zd_ref[0]], axis=0)
    n_ext = rows + 2 * CHUNK
    pos_e = t * rows - CHUNK + _iota((n_ext, 1), 0)
    col = pos_e & (CHUNK - 1)
    src_ctx = pos_e < n_ctx
    has_l = jnp.where(src_ctx, pos_e, col) >= 1
    has_r = jnp.where(src_ctx, pos_e - (n_ctx - 2), col - (CHUNK - 2)) <= 0
    e_l = jnp.where(has_l, pltpu.roll(ext, 1, 0), 0.0)
    e_r = jnp.where(has_r, pltpu.roll(ext, n_ext - 1, 0), 0.0)

    def taps(di):
        lo = CHUNK * (1 + di)
        i = 3 * (1 + di)
        return (cw[i:i + 1, :] * e_l[lo:lo + rows] + cw[i + 1:i + 2, :] * ext[lo:lo + rows]
                + cw[i + 2:i + 3, :] * e_r[lo:lo + rows])

    pos_o = t * rows + _iota((rows, 1), 0)
    acc = taps(0)
    acc = acc + jnp.where(pos_o - CHUNK >= n_ctx, taps(-1), 0.0)
    acc = acc + jnp.where((pos_o >= n_ctx) & (pos_o + CHUNK < seq), taps(1), 0.0)
    y = _silu(acc)
    q = _l2n_heads(y[:, 0:GROUP_W])
    k = _l2n_heads(y[:, GROUP_W:2 * GROUP_W])
    o_ref[0] = jnp.concatenate([q, k, y[:, 2 * GROUP_W:]], axis=1)


def _gd_prep(z_gd, cw, n_ctx):
    bsz, seq, _ = z_gd.shape
    w = 3 * GROUP_W
    rows = _row_tile(seq, 1088, CHUNK)
    per, n_chunks = rows // CHUNK, seq // CHUNK
    kern = functools.partial(_gd_prep_kernel, n_ctx, seq, rows)
    return pl.pallas_call(
        kern,
        name="gd_prep",
        grid=(bsz, seq // rows),
        in_specs=[pl.BlockSpec((1, rows, w), lambda b, t: (b, t, 0)),
                  pl.BlockSpec((1, CHUNK, w), lambda b, t: (b, jnp.maximum(t * per - 1, 0), 0)),
                  pl.BlockSpec((1, CHUNK, w), lambda b, t: (b, jnp.minimum((t + 1) * per, n_chunks - 1), 0)),
                  pl.BlockSpec(cw.shape, lambda b, t: (0, 0))],
        out_specs=pl.BlockSpec((1, rows, w), lambda b, t: (b, t, 0)),
        out_shape=jax.ShapeDtypeStruct((bsz, seq, w), F32),
        compiler_params=pltpu.CompilerParams(dimension_semantics=("arbitrary", "arbitrary"),
                                             vmem_limit_bytes=VMEM_LIMIT),
    )(z_gd, z_gd, z_gd, cw)


def _head_rms(h, gain):
    return h * lax.rsqrt(_block_sum(h * h) * (1.0 / HEAD_V) + EPS) * gain


def _head_groupnorm(h, gain, bias):
    dlt = h - _block_sum(h) * (1.0 / HEAD_V)
    return dlt * lax.rsqrt(_block_sum(dlt * dlt) * (1.0 / HEAD_V) + GN_EPS) * gain + bias


N_ROW_INPUTS = 13


def _out_mlp_kernel(n_ctx, tm, ff_chunk, final, n_part, row0, *refs):
    def stacked(i):
        parts = [p[0] for p in refs[i * n_part:(i + 1) * n_part]]
        return parts[0] if n_part == 1 else jnp.concatenate(parts, axis=0)

    x, hml_f, hml_b, hrw_f, hrw_b, hgl_f, hgl_b, hgd_f, hgd_b, oml, prw, ogl, zgd = (
        stacked(i) for i in range(N_ROW_INPUTS))
    modx_ref, modc_ref, np_ref, n2_ref, fn_ref, wo_ref, w1_ref, w2_ref, o_ref = refs[N_ROW_INPUTS * n_part:]
    t = pl.program_id(1)
    is_ctx = (row0 + t * tm + _iota((tm, 1), 0)) < n_ctx
    np_ = np_ref[...]
    mixed = jnp.concatenate([
        _head_rms(hml_f + hml_b, np_[0:1, :]) * _sigmoid(oml),
        (_head_groupnorm(hrw_f + hrw_b, np_[1:2, :], np_[2:3, :]) + prw[:, GROUP_W:]) * prw[:, :GROUP_W],
        _head_rms(hgl_f + hgl_b, np_[3:4, :]) * _silu(ogl),
        _head_rms(hgd_f + hgd_b, np_[4:5, :]) * _silu(zgd),
    ], axis=1)
    x = x + _pick_mod(modx_ref, modc_ref, is_ctx, 2) * _mm(_bf(mixed), wo_ref[...])
    shift = _pick_mod(modx_ref, modc_ref, is_ctx, 3)
    scale = _pick_mod(modx_ref, modc_ref, is_ctx, 4)
    h = _bf(_rms_rows(x, n2_ref[...]) * (1.0 + scale) + shift)
    d_ff = w1_ref.shape[1]
    acc = jnp.zeros(x.shape, F32)
    for c0 in range(0, d_ff, ff_chunk):
        a = jnp.maximum(_mm(h, w1_ref[:, c0:c0 + ff_chunk]), 0.0)
        acc = acc + _mm(_bf(a * a), w2_ref[c0:c0 + ff_chunk, :])
    x = x + _pick_mod(modx_ref, modc_ref, is_ctx, 5) * acc
    if final:
        x = _rms_rows(x, fn_ref[...])
    o_ref[0] = x


def _out_mlp(tok, hs, z_ml, rw_post, z_gl, z_gd, modx, modc, normp, n2, fnw, wo, w1, w2,
             n_ctx, tm, final):
    bsz, seq, d_model = tok.shape
    half = next((hb for hb in (256, 128, 64, 32, 16, 8) if n_ctx % hb == 0 and (seq - n_ctx) % (2 * hb) == 0), 0)
    latent_only = final and half > 0
    if latent_only:
        n_part, tm, row0, out_rows = 2, 2 * half, n_ctx, seq - n_ctx
        first = n_ctx // half

        def row(w, j=0):
            return [pl.BlockSpec((1, half, w), lambda b, t, j=j, p=p: (b, first + 2 * t + p, j)) for p in range(2)]
    else:
        n_part, row0, out_rows = 1, 0, seq

        def row(w, j=0):
            return [pl.BlockSpec((1, tm, w), lambda b, t, j=j: (b, t, j))]
    kern = functools.partial(_out_mlp_kernel, n_ctx, tm, 1024, final, n_part, row0)
    const = lambda a: pl.BlockSpec(a.shape, lambda b, t, nd=a.ndim: (0,) * nd)
    wconst = lambda a: pl.BlockSpec(a.shape, lambda b, t: (0, 0), pipeline_mode=pl.Buffered(1))
    row_arrays = [tok, *hs, z_ml, rw_post, z_gl, z_gd]
    row_specs = (row(d_model) + [s for _ in hs for s in row(GROUP_W)]
                 + row(GROUP_W, 2) + row(2 * GROUP_W) + row(GROUP_W, 2) + row(GROUP_W, 3))
    return pl.pallas_call(
        kern,
        name="out_mlp",
        grid=(bsz, out_rows // tm),
        in_specs=row_specs + [
                  pl.BlockSpec((1, 8, d_model), lambda b, t: (b, 0, 0)), const(modc), const(normp), const(n2),
                  const(fnw), wconst(wo), wconst(w1), wconst(w2)],
        out_specs=pl.BlockSpec((1, tm, d_model), lambda b, t: (b, t, 0)),
        out_shape=jax.ShapeDtypeStruct((bsz, out_rows, d_model), F32),
        compiler_params=pltpu.CompilerParams(dimension_semantics=("arbitrary", "arbitrary"),
                                             vmem_limit_bytes=VMEM_LIMIT),
    )(*[a for a in row_arrays for _ in range(n_part)], modx, modc, normp, n2, fnw, wo, w1, w2)


def _pad_rows(a, rows=8):
    return jnp.concatenate([a, jnp.zeros((rows - a.shape[0],) + a.shape[1:], a.dtype)], axis=0)


def _misc_row(pairs):
    row = jnp.zeros((MISC_W,), F32)
    for col0, p in pairs:
        row = row.at[col0:col0 + N_DIR * HEADS].set(p.reshape(-1))
    return row


def _permute_w_in(w):
    d_model = w.shape[0]
    ml_w, rw_w, gl_w = 784, 1024, 784
    o_ml, o_rw, o_gl, o_gd = 0, ml_w, ml_w + rw_w, ml_w + rw_w + gl_w
    misc = jnp.concatenate([
        w[:, o_ml + 768:o_ml + 784],
        w[:, o_gl + 512:o_gl + 528],
        w[:, o_gd + 1024:o_gd + 1040],
        jnp.zeros((d_model, MISC_W - 48), w.dtype)], axis=1)
    return jnp.concatenate([
        w[:, o_ml:o_ml + 768],
        w[:, o_rw:o_rw + 1024],
        w[:, o_gl:o_gl + 512], w[:, o_gl + 528:o_gl + 784],
        w[:, o_gd:o_gd + 1024],
        misc], axis=1)


Z_WIDTHS = (768, 1024, 768, 1024, MISC_W)


SCAN_NB = 8
SCAN_GROUPS = (('ml', 'rw'), ('gl', 'gd'))


def _mix_layer(z_ml, base, d0, d1, z_gl, z_gd, z_misc, p, n_ctx):
    nb = SCAN_NB if z_ml.shape[0] % SCAN_NB == 0 else 1
    qkv = _gd_prep(z_gd, p['gd_cw'], n_ctx)
    mixers = {
        'ml': _Mixer(_ml_chunk, [z_ml, z_misc], [z_ml, z_misc], [p['ml_gb']],
                     (((QK_W, 2 * GROUP_W), 0.0), ((1, GROUP_W), M_INIT))),
        'rw': _Mixer(_rw_chunk, [base, d0], [base, d1], [p['rw_rp']], (((GROUP_W, GROUP_W), 0.0),)),
        'gl': _Mixer(_gl_chunk, [z_gl, z_misc], [z_gl, z_misc], [p['gl_gup'], p['gl_gb']], (((GROUP_W, QK_W), 0.0),)),
        'gd': _Mixer(_gd_chunk, [qkv, z_misc], [qkv, z_misc], [p['gd_gp']], (((GROUP_W, GROUP_W), 0.0),)),
    }
    h = {}
    for group in SCAN_GROUPS:
        for name, out in zip(group, _bidir_scan([mixers[g] for g in group], n_ctx, nb)):
            h[name] = out
    return (*h['ml'], *h['rw'], *h['gl'], *h['gd'])


def _layer_params(layer, w_in, ml_ig_b, ml_fg_b, ml_norm_w, rw_mu_prev, rw_mu_next, rw_w0, rw_w_up, rw_a0,
                  rw_a_up, rw_g_up, rw_k_k, rw_k_a, rw_r_k, rw_gn_w, rw_gn_b, gl_gate_up, gl_gate_b, gl_norm_w,
                  gd_conv_w, gd_a_log, gd_dt_bias, gd_norm_w):
    l = layer
    z64 = jnp.zeros((64, GROUP_W), F32)
    lora = jnp.concatenate([
        jnp.concatenate([rw_w_up[l, 0], z64, rw_w_up[l, 1], z64], axis=1),
        jnp.concatenate([z64, rw_a_up[l, 0], z64, rw_a_up[l, 1]], axis=1)], axis=0)
    gup = jnp.zeros((N_DIR, MISC_W, QK_W), F32).at[:, MISC_GL_AL:MISC_GL_AL + 16, :].set(gl_gate_up[l])
    return {
        'w_in': _bf(_permute_w_in(w_in[l])),
        'ml_gb': _pad_rows(_misc_row([(MISC_ML_IG, ml_ig_b[l]), (MISC_ML_FG, ml_fg_b[l])])[None, :]),
        'gl_gup': gup,
        'gl_gb': _pad_rows(gl_gate_b[l]),
        'gd_cw': _pad_rows(gd_conv_w[l].reshape(9, 3 * GROUP_W), 16),
        'gd_gp': _pad_rows(jnp.stack([_misc_row([(MISC_GD_AL, gd_dt_bias[l])]), _misc_row([(MISC_GD_AL, gd_a_log[l])])])),
        'rw_mu': _pad_rows(jnp.stack([rw_mu_prev[l], rw_mu_next[l]])),
        'rw_rp': _pad_rows(jnp.stack([rw_w0[l, 0], rw_w0[l, 1], rw_a0[l, 0], rw_a0[l, 1],
                                      rw_k_k[l], rw_k_a[l], rw_r_k[l]])),
        'rw_lora': lora,
        'rw_gup': rw_g_up[l],
        'normp': _pad_rows(jnp.stack([ml_norm_w[l], rw_gn_w[l], rw_gn_b[l], gl_norm_w[l], gd_norm_w[l]])),
    }


def kernel(x, c, ctx, c_ctx, ada_w, ada_b, norm1_w, norm2_w, w_in, w_out, ml_ig_b, ml_fg_b, ml_norm_w, rw_mu_prev, rw_mu_next, rw_w0, rw_w_up, rw_a0, rw_a_up, rw_g_up, rw_k_k, rw_k_a, rw_r_k, rw_gn_w, rw_gn_b, gl_gate_up, gl_gate_b, gl_norm_w, gd_conv_w, gd_a_log, gd_dt_bias, gd_norm_w, mlp_w1, mlp_w2, final_norm_w):
    bsz, seq_x, d_model = x.shape
    n_ctx = ctx.shape[1]
    depth = w_in.shape[0]
    seq = n_ctx + seq_x
    tm = seq // 8 if seq % 64 == 0 else seq
    tok = jnp.concatenate([ctx, x], axis=1)
    cc = _pad_rows(jnp.concatenate([c, c_ctx[None, :]], axis=0), ((bsz + 1 + 7) // 8) * 8)
    mod = _ada_mod(cc, ada_w, ada_b)
    fnw = final_norm_w[None, :]
    for layer in range(depth):
        p = _layer_params(layer, w_in, ml_ig_b, ml_fg_b, ml_norm_w, rw_mu_prev, rw_mu_next, rw_w0, rw_w_up, rw_a0,
                          rw_a_up, rw_g_up, rw_k_k, rw_k_a, rw_r_k, rw_gn_w, rw_gn_b, gl_gate_up, gl_gate_b,
                          gl_norm_w, gd_conv_w, gd_a_log, gd_dt_bias, gd_norm_w)
        modx = jnp.pad(mod[layer, :bsz].reshape(bsz, 6, d_model), ((0, 0), (0, 2), (0, 0)))
        modc = _pad_rows(mod[layer, bsz].reshape(6, d_model))
        z_ml, base, d0, d1, rw_post, z_gl, z_gd, z_misc = _in_proj(
            tok, modx, modc, norm1_w[layer][None, :], p['w_in'], p['rw_mu'], p['rw_rp'], p['rw_lora'], p['rw_gup'],
            n_ctx, tm)
        hs = _mix_layer(z_ml, base, d0, d1, z_gl, z_gd, z_misc, p, n_ctx)
        tok = _out_mlp(tok, hs, z_ml, rw_post, z_gl, z_gd, modx, modc, p['normp'],
                       norm2_w[layer][None, :], fnw, _bf(w_out[layer]), _bf(mlp_w1[layer]), _bf(mlp_w2[layer]),
                       n_ctx, tm, layer == depth - 1)
    return tok if tok.shape[1] == seq_x else tok[:, n_ctx:]
```
